```python
import math
import jax
import jax.numpy as jnp
from jax import lax
import numpy as np

D_MODEL = 1024
BATCH = 16
SEQ = 256
DEPTH = 4
DEC_BATCH = 4
DEC_SEQ = 4096
PAST_LEN = 512

GRID_W = 64
N_EVEN = (DEPTH + 1) // 2
N_ODD = DEPTH // 2
N_HEADS_A = 8
N_KV_A = 2
HD_A = 64
ROPE_THETA = 10000.0
Q_BLOCK = 128
H_B = 4
DK_B = 64
DV_B = 128
RET_CHUNK = 128
H_C = 8
DK_C = 128
DV_C = 128
CONV_K = 5
GDN_CHUNK = 64
N_EXPERTS = 64
TOP_K = 8
N_GROUPS = 8
TOPK_GROUPS = 4
D_EXPERT = 256
D_SHARED = 256
ROUTED_SCALE = 2.5
MOE_BLOCK = 128
A_Q = N_HEADS_A * HD_A
A_KV = N_KV_A * HD_A
B_QK = H_B * DK_B
B_V = H_B * DV_B
EVEN_IN = A_Q + 2 * A_KV + 2 * B_QK + 2 * B_V
EVEN_OUT = A_Q + B_V
C_QK = H_C * DK_C
C_V = H_C * DV_C
ODD_IN = 2 * C_QK + 2 * C_V + 4 * H_C
ODD_OUT = C_V
DEEPNORM_ALPHA = (2 * DEPTH) ** 0.25
DEEPNORM_BETA = (8 * DEPTH) ** -0.25
EPS = 1e-6

kernel_name = 'hybrid_diffusion_prefix_trunk_step'


def _rms_norm(x, g):
    xf = x.astype(jnp.float32)
    y = xf * lax.rsqrt(jnp.mean(xf * xf, axis=-1, keepdims=True) + EPS)
    return (y * g.astype(jnp.float32)).astype(x.dtype)


def _l2_norm(x):
    xf = x.astype(jnp.float32)
    return (xf * lax.rsqrt(jnp.sum(xf * xf, axis=-1, keepdims=True) + EPS)).astype(x.dtype)


def _layer_norm(x, g, b):
    xf = x.astype(jnp.float32)
    mu = jnp.mean(xf, axis=-1, keepdims=True)
    var = jnp.mean(jnp.square(xf - mu), axis=-1, keepdims=True)
    y = (xf - mu) * lax.rsqrt(var + EPS)
    return (y * g.astype(jnp.float32) + b.astype(jnp.float32)).astype(x.dtype)


def _group_norm_heads(o, g):
    B, T, H, d = o.shape
    of = o.astype(jnp.float32)
    mu = jnp.mean(of, axis=-1, keepdims=True)
    var = jnp.mean(jnp.square(of - mu), axis=-1, keepdims=True)
    y = ((of - mu) * lax.rsqrt(var + EPS)).reshape(B, T, H * d)
    return (y * g.astype(jnp.float32)).astype(o.dtype)


def _axial_rope(x):
    n_tok = x.shape[1]
    rows = n_tok // GRID_W
    row = jnp.repeat(jnp.arange(rows), GRID_W)
    col = jnp.tile(jnp.arange(GRID_W), rows)
    nf = x.shape[-1] // 4
    inv_freq = ROPE_THETA ** (-jnp.arange(nf, dtype=jnp.float32) / nf)

    def rotate(xh, pos):
        ang = pos.astype(jnp.float32)[:, None] * inv_freq[None, :]
        cos = jnp.cos(ang)[None, :, None, :].astype(x.dtype)
        sin = jnp.sin(ang)[None, :, None, :].astype(x.dtype)
        x1, x2 = jnp.split(xh, 2, axis=-1)
        return jnp.concatenate([x1 * cos - x2 * sin, x2 * cos + x1 * sin], axis=-1)

    x_row, x_col = jnp.split(x, 2, axis=-1)
    return jnp.concatenate([rotate(x_row, row), rotate(x_col, col)], axis=-1)


def _block_attention(q, k, v):
    B, T, H, hd = q.shape
    KV = k.shape[2]
    G = H // KV
    nb = T // Q_BLOCK
    qb = q.reshape(B, nb, Q_BLOCK, KV, G, hd).transpose(1, 0, 3, 4, 2, 5)
    scale = hd ** -0.5

    def one_block(q_blk):
        s = jnp.einsum('bkgqd,bskd->bkgqs', q_blk, k).astype(jnp.float32) * scale
        p = jax.nn.softmax(s, axis=-1).astype(v.dtype)
        return jnp.einsum('bkgqs,bskd->bkgqd', p, v)

    o = lax.map(one_block, qb)
    return o.transpose(1, 0, 4, 2, 3, 5).reshape(B, T, H * hd)


def _retention_scan(q, k, v, log_gamma, s0):
    B, H, T, _ = q.shape
    n = T // RET_CHUNK
    lg = log_gamma.astype(jnp.float32)
    i = jnp.arange(RET_CHUNK, dtype=jnp.float32)
    diff = i[:, None] - i[None, :]
    intra = jnp.where(diff >= 0, jnp.exp(jnp.maximum(diff, 0.0)[None] * lg[:, None, None]), 0.0).astype(q.dtype)
    q_dec = jnp.exp((i + 1.0)[None, :] * lg[:, None]).astype(q.dtype)
    k_dec = jnp.exp((RET_CHUNK - 1.0 - i)[None, :] * lg[:, None]).astype(q.dtype)
    c_dec = jnp.exp(RET_CHUNK * lg).astype(q.dtype)

    def to_chunks(t):
        return jnp.moveaxis(t.reshape(B, H, n, RET_CHUNK, t.shape[-1]), 2, 0)

    def step(s, inp):
        qc, kc, vc = inp
        scores = jnp.einsum('bhid,bhjd->bhij', qc, kc) * intra
        o = jnp.einsum('bhij,bhje->bhie', scores, vc) + jnp.einsum('bhid,bhde->bhie', qc * q_dec[:, :, None], s)
        s = s * c_dec[:, None, None] + jnp.einsum('bhjd,bhje->bhde', kc * k_dec[:, :, None], vc)
        return s, o

    s, o = lax.scan(step, s0, (to_chunks(q), to_chunks(k), to_chunks(v)))
    return jnp.moveaxis(o, 0, 2).reshape(B, H, T, -1), s


def _gated_delta_scan(q, k, v, g, beta, s0):
    out_dtype = v.dtype
    B, H, T, dk = q.shape
    C = GDN_CHUNK
    n = T // C

    def chunks(t):
        return t.astype(jnp.float32).reshape((B, H, n, C) + t.shape[3:])

    qc = chunks(q) * (dk ** -0.5)
    kc = chunks(k)
    vc = chunks(v)
    bc = chunks(beta)
    gc = jnp.cumsum(chunks(g), axis=-1)
    idx = jnp.arange(C)
    lower = idx[:, None] >= idx[None, :]
    strict = idx[:, None] > idx[None, :]
    decay = jnp.exp(jnp.where(lower, gc[..., :, None] - gc[..., None, :], -jnp.inf))
    k_beta = kc * bc[..., None]
    v_beta = vc * bc[..., None]
    a = jnp.where(strict, jnp.einsum('bhnid,bhnjd->bhnij', k_beta, kc) * decay, 0.0)
    eye = jnp.eye(C, dtype=jnp.float32)
    t_inv = lax.linalg.triangular_solve(a + eye, jnp.broadcast_to(eye, a.shape), left_side=True,
                                        lower=True, unit_diagonal=True)
    w = jnp.einsum('bhnij,bhnjd->bhnid', t_inv, k_beta * jnp.exp(gc)[..., None])
    u = jnp.einsum('bhnij,bhnje->bhnie', t_inv, v_beta)
    attn = jnp.einsum('bhnid,bhnjd->bhnij', qc, kc) * decay
    q_g = qc * jnp.exp(gc)[..., None]
    k_g = kc * jnp.exp(gc[..., -1:] - gc)[..., None]
    g_last = jnp.exp(gc[..., -1])

    def step(s, inp):
        attn_n, qg_n, kg_n, w_n, u_n, gl_n = inp
        v_new = u_n - jnp.einsum('bhid,bhde->bhie', w_n, s)
        o = jnp.einsum('bhid,bhde->bhie', qg_n, s) + jnp.einsum('bhij,bhje->bhie', attn_n, v_new)
        s = s * gl_n[..., None, None] + jnp.einsum('bhid,bhie->bhde', kg_n, v_new)
        return s, o

    xs = tuple(jnp.moveaxis(t, 2, 0) for t in (attn, q_g, k_g, w, u, g_last))
    s, o = lax.scan(step, s0.astype(jnp.float32), xs)
    o = jnp.moveaxis(o, 0, 2).reshape(B, H, T, -1)
    return o.astype(out_dtype), s.astype(out_dtype)


def _depthwise_conv_centred(x, w):
    K, ch = w.shape
    return lax.conv_general_dilated(x, w.reshape(K, 1, ch).astype(x.dtype), window_strides=(1,),
                                    padding=[((K - 1) // 2, K // 2)],
                                    dimension_numbers=('NWC', 'WIO', 'NWC'), feature_group_count=ch)


def _attn_retention_mixer(h, w_in, w_out, q_norm, k_norm, log_decay, gn_g, ctx):
    B, T, _ = h.shape
    p = h @ w_in
    splits = [int(s) for s in np.cumsum([A_Q, A_KV, A_KV, B_QK, B_QK, B_V])]
    q, k, v, rq, rk, rv, rg = jnp.split(p, splits, axis=-1)
    q = _rms_norm(q.reshape(B, T, N_HEADS_A, HD_A), q_norm)
    k = _rms_norm(k.reshape(B, T, N_KV_A, HD_A), k_norm)
    v = v.reshape(B, T, N_KV_A, HD_A)
    rq = rq.reshape(B, T, H_B, DK_B).transpose(0, 2, 1, 3)
    rk = rk.reshape(B, T, H_B, DK_B).transpose(0, 2, 1, 3) * (DK_B ** -0.5)
    rv = rv.reshape(B, T, H_B, DV_B).transpose(0, 2, 1, 3)
    if ctx is None:
        k_all, v_all = k, v
        s0_f = jnp.zeros((B, H_B, DK_B, DV_B), h.dtype)
        s0_b = s0_f
    else:
        k_ctx, v_ctx, s_ctx = ctx
        q = _axial_rope(q)
        k_all = jnp.concatenate([_axial_rope(k), k_ctx], axis=1)
        v_all = jnp.concatenate([v, v_ctx], axis=1)
        s0_f, s0_b = s_ctx[:, 0], s_ctx[:, 1]
    o_attn = _block_attention(q, k_all, v_all)
    o_f, s_f = _retention_scan(rq, rk, rv, log_decay[0], s0_f)
    o_b, s_b = _retention_scan(rq[:, :, ::-1], rk[:, :, ::-1], rv[:, :, ::-1], log_decay[1], s0_b)
    o_ret = (o_f + o_b[:, :, ::-1]).transpose(0, 2, 1, 3)
    o_ret = _group_norm_heads(o_ret, gn_g) * jax.nn.silu(rg)
    out = jnp.concatenate([o_attn, o_ret], axis=-1) @ w_out
    return out, k, v, jnp.stack([s_f, s_b], axis=1)


def _gated_deltanet_mixer(h, w_in, conv_w, a_log, dt_bias, norm_g, w_out, s_ctx):
    B, T, _ = h.shape
    p = h @ w_in
    qkv, z, ab = jnp.split(p, [2 * C_QK + C_V, 2 * C_QK + 2 * C_V], axis=-1)
    qkv = jax.nn.silu(_depthwise_conv_centred(qkv, conv_w))
    q, k, v = jnp.split(qkv, [C_QK, 2 * C_QK], axis=-1)
    q = _l2_norm(q.reshape(B, T, H_C, DK_C)).transpose(0, 2, 1, 3)
    k = _l2_norm(k.reshape(B, T, H_C, DK_C)).transpose(0, 2, 1, 3)
    v = v.reshape(B, T, H_C, DV_C).transpose(0, 2, 1, 3)
    ab = ab.astype(jnp.float32).reshape(B, T, 2, 2, H_C)
    beta = jax.nn.sigmoid(ab[:, :, 0]).transpose(0, 2, 3, 1)
    g = (-jnp.exp(a_log.astype(jnp.float32))
         * jax.nn.softplus(ab[:, :, 1] + dt_bias.astype(jnp.float32))).transpose(0, 2, 3, 1)
    if s_ctx is None:
        s0_f = jnp.zeros((B, H_C, DK_C, DV_C), h.dtype)
        s0_b = s0_f
    else:
        s0_f, s0_b = s_ctx[:, 0], s_ctx[:, 1]
    o_f, s_f = _gated_delta_scan(q, k, v, g[:, 0], beta[:, 0], s0_f)
    o_b, s_b = _gated_delta_scan(q[:, :, ::-1], k[:, :, ::-1], v[:, :, ::-1],
                                 g[:, 1, :, ::-1], beta[:, 1, :, ::-1], s0_b)
    o = (o_f + o_b[:, :, ::-1]).transpose(0, 2, 1, 3)
    o = _rms_norm(o, norm_g) * jax.nn.silu(z.reshape(B, T, H_C, DV_C))
    return o.reshape(B, T, C_V) @ w_out, jnp.stack([s_f, s_b], axis=1)


def _swiglu(x, w1, w3, w2):
    return (jax.nn.silu(x @ w1) * (x @ w3)) @ w2


def _grouped_experts(x, idx, w, w1, w3, w2):
    n, D = x.shape
    A = n * TOP_K
    e_flat = idx.reshape(-1)
    tok_flat = jnp.arange(A, dtype=jnp.int32) // TOP_K
    w_flat = w.reshape(-1)
    order = jnp.argsort(e_flat)
    e_sorted = e_flat[order]
    counts = jnp.bincount(e_flat, length=N_EXPERTS)
    padded = (counts + MOE_BLOCK - 1) // MOE_BLOCK * MOE_BLOCK
    pad_end = jnp.cumsum(padded)
    pad_start = pad_end - padded
    start = jnp.cumsum(counts) - counts
    dest = pad_start[e_sorted] + jnp.arange(A) - start[e_sorted]
    n_blocks = -(-A // MOE_BLOCK) + N_EXPERTS
    rows = n_blocks * MOE_BLOCK
    row_tok = jnp.full((rows,), n, jnp.int32).at[dest].set(tok_flat[order])
    row_w = jnp.zeros((rows,), x.dtype).at[dest].set(w_flat[order])
    block_expert = jnp.minimum(jnp.searchsorted(pad_end, jnp.arange(n_blocks) * MOE_BLOCK, side='right'),
                               N_EXPERTS - 1)
    x_pad = jnp.concatenate([x, jnp.zeros((1, D), x.dtype)], axis=0)

    def body(acc, inp):
        tok_b, w_b, e = inp
        xb = x_pad[tok_b]
        hid = jax.nn.silu(xb @ w1[e]) * (xb @ w3[e])
        return acc.at[tok_b].add((hid @ w2[e]) * w_b[:, None]), None

    acc, _ = lax.scan(body, jnp.zeros((n + 1, D), x.dtype),
                      (row_tok.reshape(n_blocks, MOE_BLOCK), row_w.reshape(n_blocks, MOE_BLOCK), block_expert))
    return acc[:n]


def _moe(h, router_w, router_bias, w1, w3, w2, ws1, ws3, ws2):
    B, T, D = h.shape
    x = h.reshape(B * T, D)
    n = x.shape[0]
    scores = jax.nn.sigmoid((x @ router_w).astype(jnp.float32))
    sel = scores + router_bias.astype(jnp.float32)
    per_group = N_EXPERTS // N_GROUPS
    grp_score = jnp.sum(lax.top_k(sel.reshape(n, N_GROUPS, per_group), 2)[0], axis=-1)
    _, top_grp = lax.top_k(grp_score, TOPK_GROUPS)
    grp_mask = jnp.sum(jax.nn.one_hot(top_grp, N_GROUPS, dtype=jnp.float32), axis=1) > 0
    exp_mask = jnp.repeat(grp_mask, per_group, axis=1)
    _, idx = lax.top_k(jnp.where(exp_mask, sel, -jnp.inf), TOP_K)
    w = jnp.take_along_axis(scores, idx, axis=1)
    w = (w / jnp.sum(w, axis=-1, keepdims=True) * ROUTED_SCALE).astype(x.dtype)
    routed = _grouped_experts(x, idx, w, w1, w3, w2)
    return (routed + _swiglu(x, ws1, ws3, ws2)).reshape(B, T, D)


def setup_inputs(seed: int = 0) -> dict:
    key = jax.random.key(seed)
    ks = iter(jax.random.split(key, 48))
    D = D_MODEL

    def nrm(shape, s):
        return jax.random.normal(next(ks), shape, jnp.float32) * s

    x_prompt = nrm((BATCH, SEQ, D), 1.0)
    x_sample = nrm((DEC_BATCH, DEC_SEQ, D), 1.0)
    cache_attn_k = nrm((DEC_BATCH, N_EVEN, PAST_LEN, N_KV_A, HD_A), 1.0)
    cache_attn_v = nrm((DEC_BATCH, N_EVEN, PAST_LEN, N_KV_A, HD_A), 1.0)
    state_ret = nrm((DEC_BATCH, N_EVEN, 2, H_B, DK_B, DV_B), 0.1)
    state_gdn = nrm((DEC_BATCH, N_ODD, 2, H_C, DK_C, DV_C), 0.1)
    c = nrm((DEC_BATCH, D), 1.0)
    c_ctx = nrm((D,), 1.0)
    mod_w = nrm((DEPTH, D, 6 * D), 0.5 * D ** -0.5)
    mod_b = nrm((DEPTH, 6 * D), 0.02)
    ln_g = 1.0 + nrm((DEPTH, 2, D), 0.02)
    ln_b = nrm((DEPTH, 2, D), 0.02)
    even_w_in = nrm((N_EVEN, D, EVEN_IN), D ** -0.5)
    even_w_out = nrm((N_EVEN, EVEN_OUT, D), EVEN_OUT ** -0.5 * DEEPNORM_BETA)
    attn_q_norm = 1.0 + nrm((N_EVEN, HD_A), 0.02)
    attn_k_norm = 1.0 + nrm((N_EVEN, HD_A), 0.02)
    decay_exp = jnp.linspace(5.0, 12.0, H_B)[None, None, :] + nrm((N_EVEN, 2, H_B), 0.1)
    ret_log_decay = jnp.log1p(-(2.0 ** (-decay_exp)))
    ret_norm_g = 1.0 + nrm((N_EVEN, B_V), 0.02)
    odd_w_in = nrm((N_ODD, D, ODD_IN), D ** -0.5)
    gdn_conv_w = nrm((N_ODD, CONV_K, 2 * C_QK + C_V), CONV_K ** -0.5)
    gdn_a_log = jnp.log(jax.random.uniform(next(ks), (N_ODD, 2, H_C), jnp.float32, 1.0, 16.0))
    dt = jnp.exp(jax.random.uniform(next(ks), (N_ODD, 2, H_C), jnp.float32, math.log(1e-3), math.log(1e-1)))
    gdn_dt_bias = dt + jnp.log(-jnp.expm1(-dt))
    gdn_norm_g = 1.0 + nrm((N_ODD, DV_C), 0.02)
    odd_w_out = nrm((N_ODD, ODD_OUT, D), ODD_OUT ** -0.5 * DEEPNORM_BETA)
    router_w = nrm((DEPTH, D, N_EXPERTS), D ** -0.5)
    router_bias = nrm((DEPTH, N_EXPERTS), 0.01)
    expert_w1 = nrm((DEPTH, N_EXPERTS, D, D_EXPERT), D ** -0.5)
    expert_w3 = nrm((DEPTH, N_EXPERTS, D, D_EXPERT), D ** -0.5)
    expert_w2 = nrm((DEPTH, N_EXPERTS, D_EXPERT, D), D_EXPERT ** -0.5 * DEEPNORM_BETA)
    shared_w1 = nrm((DEPTH, D, D_SHARED), D ** -0.5)
    shared_w3 = nrm((DEPTH, D, D_SHARED), D ** -0.5)
    shared_w2 = nrm((DEPTH, D_SHARED, D), D_SHARED ** -0.5 * DEEPNORM_BETA)
    return {'x_prompt': x_prompt, 'x_sample': x_sample,
            'cache_attn_k': cache_attn_k, 'cache_attn_v': cache_attn_v,
            'state_ret': state_ret, 'state_gdn': state_gdn,
            'c': c, 'c_ctx': c_ctx, 'mod_w': mod_w, 'mod_b': mod_b, 'ln_g': ln_g, 'ln_b': ln_b,
            'even_w_in': even_w_in, 'even_w_out': even_w_out, 'attn_q_norm': attn_q_norm,
            'attn_k_norm': attn_k_norm, 'ret_log_decay': ret_log_decay, 'ret_norm_g': ret_norm_g,
            'odd_w_in': odd_w_in, 'gdn_conv_w': gdn_conv_w, 'gdn_a_log': gdn_a_log,
            'gdn_dt_bias': gdn_dt_bias, 'gdn_norm_g': gdn_norm_g, 'odd_w_out': odd_w_out,
            'router_w': router_w, 'router_bias': router_bias, 'expert_w1': expert_w1,
            'expert_w3': expert_w3, 'expert_w2': expert_w2, 'shared_w1': shared_w1,
            'shared_w3': shared_w3, 'shared_w2': shared_w2}


def reference(x_prompt, x_sample, cache_attn_k, cache_attn_v, state_ret, state_gdn, c, c_ctx,
              mod_w, mod_b, ln_g, ln_b, even_w_in, even_w_out, attn_q_norm, attn_k_norm,
              ret_log_decay, ret_norm_g, odd_w_in, gdn_conv_w, gdn_a_log, gdn_dt_bias, gdn_norm_g,
              odd_w_out, router_w, router_bias, expert_w1, expert_w3, expert_w2,
              shared_w1, shared_w3, shared_w2):
    yp, ys = x_prompt, x_sample
    new_k, new_v, new_ret, new_gdn = [], [], [], []
    for l in range(DEPTH):
        j = l // 2
        mp = jnp.split((jax.nn.silu(c_ctx) @ mod_w[l] + mod_b[l])[None, None, :], 6, axis=-1)
        ms = jnp.split((jax.nn.silu(c) @ mod_w[l] + mod_b[l])[:, None, :], 6, axis=-1)
        hp = yp * (1.0 + mp[1]) + mp[0]
        hs = ys * (1.0 + ms[1]) + ms[0]
        if l % 2 == 0:
            op, k_p, v_p, s_p = _attn_retention_mixer(hp, even_w_in[j], even_w_out[j], attn_q_norm[j],
                                                      attn_k_norm[j], ret_log_decay[j], ret_norm_g[j], None)
            os_, _, _, _ = _attn_retention_mixer(hs, even_w_in[j], even_w_out[j], attn_q_norm[j],
                                                 attn_k_norm[j], ret_log_decay[j], ret_norm_g[j],
                                                 (cache_attn_k[:, j], cache_attn_v[:, j], state_ret[:, j]))
            new_k.append(k_p)
            new_v.append(v_p)
            new_ret.append(s_p)
        else:
            op, s_p = _gated_deltanet_mixer(hp, odd_w_in[j], gdn_conv_w[j], gdn_a_log[j], gdn_dt_bias[j],
                                            gdn_norm_g[j], odd_w_out[j], None)
            os_, _ = _gated_deltanet_mixer(hs, odd_w_in[j], gdn_conv_w[j], gdn_a_log[j], gdn_dt_bias[j],
                                           gdn_norm_g[j], odd_w_out[j], state_gdn[:, j])
            new_gdn.append(s_p)
        yp = _layer_norm(DEEPNORM_ALPHA * yp + mp[2] * op, ln_g[l, 0], ln_b[l, 0])
        ys = _layer_norm(DEEPNORM_ALPHA * ys + ms[2] * os_, ln_g[l, 0], ln_b[l, 0])
        hp = yp * (1.0 + mp[4]) + mp[3]
        hs = ys * (1.0 + ms[4]) + ms[3]
        fp = _moe(hp, router_w[l], router_bias[l], expert_w1[l], expert_w3[l], expert_w2[l],
                  shared_w1[l], shared_w3[l], shared_w2[l])
        fs = _moe(hs, router_w[l], router_bias[l], expert_w1[l], expert_w3[l], expert_w2[l],
                  shared_w1[l], shared_w3[l], shared_w2[l])
        yp = _layer_norm(DEEPNORM_ALPHA * yp + mp[5] * fp, ln_g[l, 1], ln_b[l, 1])
        ys = _layer_norm(DEEPNORM_ALPHA * ys + ms[5] * fs, ln_g[l, 1], ln_b[l, 1])
    return (yp, ys, jnp.stack(new_k, axis=1), jnp.stack(new_v, axis=1),
            jnp.stack(new_ret, axis=1), jnp.stack(new_gdn, axis=1))
```

```python
import functools
import math

import jax
import jax.numpy as jnp
import numpy as np
from jax import lax
from jax.experimental import pallas as pl
from jax.experimental.pallas import tpu as pltpu

D_MODEL = 1024
BATCH = 16
SEQ = 256
DEPTH = 4
DEC_BATCH = 4
DEC_SEQ = 4096
PAST_LEN = 512
GRID_W = 64
N_HEADS_A = 8
N_KV_A = 2
HD_A = 64
ROPE_THETA = 10000.0
H_B = 4
DK_B = 64
DV_B = 128
RET_CHUNK = 128
H_C = 8
DK_C = 128
DV_C = 128
CONV_K = 5
GDN_CHUNK = 64
N_EXPERTS = 64
TOP_K = 8
N_GROUPS = 8
TOPK_GROUPS = 4
D_EXPERT = 256
D_SHARED = 256
ROUTED_SCALE = 2.5
MOE_BLOCK = 128
A_Q = N_HEADS_A * HD_A
A_KV = N_KV_A * HD_A
B_QK = H_B * DK_B
B_V = H_B * DV_B
EVEN_IN = A_Q + 2 * A_KV + 2 * B_QK + 2 * B_V
C_QK = H_C * DK_C
C_V = H_C * DV_C
DEEPNORM_ALPHA = (2 * DEPTH) ** 0.25
EPS = 1e-6

N_P = BATCH * SEQ
N_S = DEC_BATCH * DEC_SEQ
N_TOK = N_P + N_S
N_MOD = 1 + DEC_BATCH
MOD_ROWS = 8

LANES = 128
VMEM_LIMIT = 56 * 1024 * 1024

F32 = jnp.float32
BF16 = jnp.bfloat16


def _cparams(*sem):
    return pltpu.CompilerParams(dimension_semantics=sem, vmem_limit_bytes=VMEM_LIMIT)


def _bdot(a, b):
    return jnp.dot(a.astype(BF16), b.astype(BF16), preferred_element_type=F32)


def _bdot_t(a, b):
    return lax.dot_general(a.astype(BF16), b.astype(BF16), (((0,), (0,)), ((), ())),
                           preferred_element_type=F32)


def _bdot_nt(a, b):
    return lax.dot_general(a.astype(BF16), b.astype(BF16), (((1,), (1,)), ((), ())),
                           preferred_element_type=F32)


def _split3(a):
    hi = a.astype(BF16)
    r = a - hi.astype(F32)
    mid = r.astype(BF16)
    lo = (r - mid.astype(F32)).astype(BF16)
    return hi, mid, lo


def _dot_hp(a, b):
    a0, a1, a2 = _split3(a)
    b0, b1, b2 = _split3(b)
    d = lambda x, y: jnp.dot(x, y, preferred_element_type=F32)
    small = d(a0, b2) + d(a2, b0) + d(a1, b1)
    return (d(a0, b1) + d(a1, b0)) + small + d(a0, b0)


def _silu(x):
    return x * (1.0 / (1.0 + jnp.exp(-x)))


def _mod_row_of_tile(i, tile):
    tiles_p = N_P // tile
    tiles_per_b = DEC_SEQ // tile
    return jnp.where(i < tiles_p, 0, 1 + (i - tiles_p) // tiles_per_b)


MOD_TN = 1536


def _mod_kernel(c_ref, w_ref, b_ref, o_ref):
    a = _silu(c_ref[...])
    o_ref[0] = _bdot(a, w_ref[0]) + b_ref[0]


def _mod_vectors(cvec, mod_w, mod_b):
    n6 = 6 * D_MODEL
    return pl.pallas_call(
        _mod_kernel,
        grid=(DEPTH, n6 // MOD_TN),
        in_specs=[pl.BlockSpec((MOD_ROWS, D_MODEL), lambda l, j: (0, 0)),
                  pl.BlockSpec((1, D_MODEL, MOD_TN), lambda l, j: (l, 0, j)),
                  pl.BlockSpec((1, 1, MOD_TN), lambda l, j: (l, 0, j))],
        out_specs=pl.BlockSpec((1, MOD_ROWS, MOD_TN), lambda l, j: (l, 0, j)),
        out_shape=jax.ShapeDtypeStruct((DEPTH, MOD_ROWS, n6), F32),
        compiler_params=_cparams("parallel", "parallel"),
    )(cvec, mod_w, mod_b.reshape(DEPTH, 1, n6))


PROJ_TM = 256


def _proj_kernel(x_ref, shift_ref, scale_ref, w_ref, *o_refs, widths):
    h = (x_ref[...] * (1.0 + scale_ref[0]) + shift_ref[0]).astype(BF16)
    off = 0
    for o_ref, wd in zip(o_refs, widths):
        o_ref[...] = jnp.dot(h, w_ref[:, off:off + wd], preferred_element_type=F32).astype(o_ref.dtype)
        off += wd


def _modulated_proj(x, mod3, shift_blk, w_bf16, widths, dtypes):
    n = x.shape[0]
    tm = PROJ_TM
    row = lambda i: _mod_row_of_tile(i, tm)
    return pl.pallas_call(
        functools.partial(_proj_kernel, widths=tuple(widths)),
        grid=(n // tm,),
        in_specs=[pl.BlockSpec((tm, D_MODEL), lambda i: (i, 0)),
                  pl.BlockSpec((1, 1, D_MODEL), lambda i: (row(i), 0, shift_blk)),
                  pl.BlockSpec((1, 1, D_MODEL), lambda i: (row(i), 0, shift_blk + 1)),
                  pl.BlockSpec((D_MODEL, sum(widths)), lambda i: (0, 0))],
        out_specs=[pl.BlockSpec((tm, wd), lambda i: (i, 0)) for wd in widths],
        out_shape=[jax.ShapeDtypeStruct((n, wd), dt) for wd, dt in zip(widths, dtypes)],
        compiler_params=_cparams("parallel"),
    )(x, mod3, mod3, w_bf16)


OUT_TM = 512


def _layer_norm_rows(r, g, b):
    mu = jnp.mean(r, axis=-1, keepdims=True)
    d = r - mu
    var = jnp.mean(d * d, axis=-1, keepdims=True)
    return d * lax.rsqrt(var + EPS) * g + b


def _outproj_kernel(*refs, n_a):
    a_refs = refs[:n_a]
    w_ref, x_ref, gate_ref, g_ref, b_ref, o_ref = refs[n_a:]
    off = 0
    acc = None
    for a_ref in a_refs:
        wd = a_ref.shape[1]
        part = jnp.dot(a_ref[...].astype(BF16), w_ref[off:off + wd, :], preferred_element_type=F32)
        acc = part if acc is None else acc + part
        off += wd
    r = DEEPNORM_ALPHA * x_ref[...] + gate_ref[0] * acc
    o_ref[...] = _layer_norm_rows(r, g_ref[...], b_ref[...])


def _outproj_ln(a_list, w_bf16, x, mod3, gate_blk, ln_g, ln_b):
    n = x.shape[0]
    tm = OUT_TM
    row = lambda i: _mod_row_of_tile(i, tm)
    kdim = w_bf16.shape[0]
    return pl.pallas_call(
        functools.partial(_outproj_kernel, n_a=len(a_list)),
        grid=(n // tm,),
        in_specs=[pl.BlockSpec((tm, a.shape[1]), lambda i: (i, 0)) for a in a_list] + [
            pl.BlockSpec((kdim, D_MODEL), lambda i: (0, 0)),
            pl.BlockSpec((tm, D_MODEL), lambda i: (i, 0)),
            pl.BlockSpec((1, 1, D_MODEL), lambda i: (row(i), 0, gate_blk)),
            pl.BlockSpec((1, D_MODEL), lambda i: (0, 0)),
            pl.BlockSpec((1, D_MODEL), lambda i: (0, 0))],
        out_specs=pl.BlockSpec((tm, D_MODEL), lambda i: (i, 0)),
        out_shape=jax.ShapeDtypeStruct((n, D_MODEL), F32),
        compiler_params=_cparams("parallel"),
    )(*a_list, w_bf16, x, mod3, ln_g.reshape(1, D_MODEL), ln_b.reshape(1, D_MODEL))


ATT_TQ = 256
GROUP_A = N_HEADS_A // N_KV_A


def _attn_kernel(q_ref, kt_ref, v_ref, o_ref):
    kt = kt_ref[0]
    v = v_ref[0]
    lane = lax.broadcasted_iota(jnp.int32, (1, LANES), 1)
    for j in range(GROUP_A):
        outs = []
        for g in range(N_KV_A):
            h = g * GROUP_A + j
            s = jnp.dot(q_ref[:, h * LANES:(h + 1) * LANES], kt, preferred_element_type=F32)
            m = jnp.max(s, axis=-1, keepdims=True)
            p = jnp.exp(s - m)
            l = jnp.sum(p, axis=-1, keepdims=True)
            pv = jnp.dot(p.astype(BF16), v, preferred_element_type=F32)
            outs.append(pv * (1.0 / l))
        o_ref[:, j * LANES:(j + 1) * LANES] = jnp.where(lane < HD_A, outs[0], outs[1]).astype(o_ref.dtype)


def _attention(q_exp, kt, v, batch, t):
    tk = v.shape[1]
    tq = min(ATT_TQ, t)
    nq = t // tq
    return pl.pallas_call(
        _attn_kernel,
        grid=(batch, nq),
        in_specs=[pl.BlockSpec((tq, N_HEADS_A * LANES), lambda b, i: (b * nq + i, 0)),
                  pl.BlockSpec((1, LANES, tk), lambda b, i: (b, 0, 0)),
                  pl.BlockSpec((1, tk, LANES), lambda b, i: (b, 0, 0))],
        out_specs=pl.BlockSpec((tq, A_Q), lambda b, i: (b * nq + i, 0)),
        out_shape=jax.ShapeDtypeStruct((batch * t, A_Q), BF16),
        compiler_params=_cparams("parallel", "parallel"),
    )(q_exp, kt, v)


RC = RET_CHUNK


def _ret_kernel(q_ref, k_ref, v_ref, g_ref, lg_ref, s0_ref, gn_ref, o_ref, s_out_ref,
                ob_ref, st_ref, *, t):
    nc = t // RC
    hp = pl.program_id(1)
    ii = lax.broadcasted_iota(jnp.int32, (RC, RC), 0).astype(F32)
    jj = lax.broadcasted_iota(jnp.int32, (RC, RC), 1).astype(F32)
    col_i = lax.broadcasted_iota(jnp.int32, (RC, 1), 0).astype(F32)
    lane = lax.broadcasted_iota(jnp.int32, (1, LANES), 1)
    masks = [(lane >= hh * DK_B) & (lane < (hh + 1) * DK_B) for hh in range(2)]

    consts = []
    for d in range(2):
        for hh in range(2):
            lg = lg_ref[pl.ds(d * H_B + hp * 2 + hh, 1), :][:, :1]
            if d == 0:
                diff = ii - jj
                qe, ke = col_i + 1.0, (RC - 1.0) - col_i
            else:
                diff = jj - ii
                qe, ke = RC - col_i, col_i
            intra = jnp.where(diff >= 0, jnp.exp(jnp.maximum(diff, 0.0) * lg), 0.0)
            consts.append((intra, jnp.exp(qe * lg), jnp.exp(ke * lg), jnp.exp(RC * lg)))
            s0 = s0_ref[0, d, hh]
            z = jnp.zeros((DK_B, DV_B), F32)
            st_ref[d * 2 + hh] = jnp.concatenate([s0, z] if hh == 0 else [z, s0], axis=0)

    def chunk(c, d):
        r0 = pl.multiple_of(c * RC, RC)
        qc = q_ref[pl.ds(r0, RC), :]
        kc = k_ref[pl.ds(r0, RC), :] * (DK_B ** -0.5)
        outs = []
        for hh in range(2):
            intra, q_dec, k_dec, c_dec = consts[d * 2 + hh]
            vc = v_ref[pl.ds(r0, RC), hh * DV_B:(hh + 1) * DV_B]
            qh = jnp.where(masks[hh], qc, 0.0)
            kh = jnp.where(masks[hh], kc, 0.0)
            s = st_ref[d * 2 + hh]
            scores = _bdot_nt(qh, kh) * intra
            o = _bdot(scores, vc) + _bdot(qh * q_dec, s)
            st_ref[d * 2 + hh] = s * c_dec + _bdot_t(kh * k_dec, vc)
            outs.append(o)
        return r0, jnp.concatenate(outs, axis=1)

    def body(c, carry):
        r0, of = chunk(c, 0)
        o_ref[pl.ds(r0, RC), :] = of
        r1, ob = chunk(nc - 1 - c, 1)
        ob_ref[pl.ds(r1, RC), :] = ob
        return carry

    lax.fori_loop(0, nc, body, 0)

    for d in range(2):
        for hh in range(2):
            s_out_ref[0, d, hh] = st_ref[d * 2 + hh][hh * DK_B:(hh + 1) * DK_B, :]

    def finish(c, carry):
        r0 = pl.multiple_of(c * RC, RC)
        o = o_ref[pl.ds(r0, RC), :] + ob_ref[pl.ds(r0, RC), :]
        gate = _silu(g_ref[pl.ds(r0, RC), :])
        ys = []
        for hh in range(2):
            oh = o[:, hh * DV_B:(hh + 1) * DV_B]
            mu = jnp.mean(oh, axis=-1, keepdims=True)
            dlt = oh - mu
            var = jnp.mean(dlt * dlt, axis=-1, keepdims=True)
            ys.append(dlt * lax.rsqrt(var + EPS))
        o_ref[pl.ds(r0, RC), :] = jnp.concatenate(ys, axis=1) * gn_ref[...] * gate
        return carry

    lax.fori_loop(0, nc, finish, 0)


def _retention(p, row_blk0, batch, t, lg_rows, s0, gn_g):
    qb, kb, vb, gb = (A_Q + 2 * A_KV) // LANES, (A_Q + 2 * A_KV + B_QK) // LANES, \
        (A_Q + 2 * A_KV + 2 * B_QK) // (2 * DV_B), (A_Q + 2 * A_KV + 2 * B_QK + B_V) // (2 * DV_B)
    return pl.pallas_call(
        functools.partial(_ret_kernel, t=t),
        grid=(batch, H_B // 2),
        in_specs=[pl.BlockSpec((t, LANES), lambda b, h: (row_blk0 + b, qb + h)),
                  pl.BlockSpec((t, LANES), lambda b, h: (row_blk0 + b, kb + h)),
                  pl.BlockSpec((t, 2 * DV_B), lambda b, h: (row_blk0 + b, vb + h)),
                  pl.BlockSpec((t, 2 * DV_B), lambda b, h: (row_blk0 + b, gb + h)),
                  pl.BlockSpec((2 * H_B, LANES), lambda b, h: (0, 0)),
                  pl.BlockSpec((1, 2, 2, DK_B, DV_B), lambda b, h: (b, 0, h, 0, 0)),
                  pl.BlockSpec((1, 2 * DV_B), lambda b, h: (0, h))],
        out_specs=[pl.BlockSpec((t, 2 * DV_B), lambda b, h: (b, h)),
                   pl.BlockSpec((1, 2, 2, DK_B, DV_B), lambda b, h: (b, 0, h, 0, 0))],
        out_shape=[jax.ShapeDtypeStruct((batch * t, B_V), F32),
                   jax.ShapeDtypeStruct((batch, 2, H_B, DK_B, DV_B), F32)],
        scratch_shapes=[pltpu.VMEM((t, 2 * DV_B), F32), pltpu.VMEM((4, LANES, DV_B), F32)],
        compiler_params=_cparams("parallel", "parallel"),
    )(p, p, p, p, lg_rows, s0, gn_g.reshape(1, B_V))


GC = GDN_CHUNK


def _unit_tri_inverse(a):
    eye = (lax.broadcasted_iota(jnp.int32, (GC, GC), 0) ==
           lax.broadcasted_iota(jnp.int32, (GC, GC), 1)).astype(F32)
    m = -a
    inv = eye + m
    steps = int(math.log2(GC)) - 1
    for _ in range(steps):
        m = _dot_hp(m, m)
        inv = inv + _dot_hp(inv, m)
    return inv


def _gdn_kernel(q_ref, k_ref, v_ref, z_ref, gb_ref, gr_ref, s0_ref, ng_ref, o_ref, s_out_ref,
                ob_ref, st_ref, *, t):
    nc = t // GC
    h = pl.program_id(1)
    ii = lax.broadcasted_iota(jnp.int32, (GC, GC), 0)
    jj = lax.broadcasted_iota(jnp.int32, (GC, GC), 1)
    incl = [ii >= jj, jj >= ii]
    strict = [ii > jj, jj > ii]
    lane32 = lax.broadcasted_iota(jnp.int32, (1, 4 * H_C), 1)
    for d in range(2):
        st_ref[d] = s0_ref[0, d, 0]

    def col(x, idx):
        return jnp.sum(jnp.where(lane32 == idx, x, 0.0), axis=1, keepdims=True)

    def chunk(c, d):
        r0 = pl.multiple_of(c * GC, GC)
        qc = q_ref[pl.ds(r0, GC), :] * (DK_C ** -0.5)
        kc = k_ref[pl.ds(r0, GC), :]
        vc = v_ref[pl.ds(r0, GC), :]
        gb = gb_ref[pl.ds(r0, GC), :]
        beta = col(gb, d * H_C + h)
        gcol = col(gb, 2 * H_C + d * H_C + h)
        grow = gr_ref[0, d, 0, pl.ds(c, 1), :]
        glast = grow[:, GC - 1:GC] if d == 0 else grow[:, 0:1]
        diff = gcol - grow
        decay = jnp.exp(jnp.where(incl[d], diff, -jnp.inf))
        eg = jnp.exp(gcol)
        kb = kc * beta
        vb = vc * beta
        a = jnp.where(strict[d], _bdot_nt(kb, kc) * decay, 0.0)
        tinv = _unit_tri_inverse(a)
        w = _bdot(tinv, kb * eg)
        u = _bdot(tinv, vb)
        attn = _bdot_nt(qc, kc) * decay
        qg = qc * eg
        kg = kc * jnp.exp(glast - gcol)
        s = st_ref[d]
        v_new = u - _bdot(w, s)
        o = _bdot(qg, s) + _bdot(attn, v_new)
        st_ref[d] = s * jnp.exp(glast) + _bdot_t(kg, v_new)
        return r0, o

    def body(c, carry):
        r0, of = chunk(c, 0)
        o_ref[pl.ds(r0, GC), :] = of
        r1, ob = chunk(nc - 1 - c, 1)
        ob_ref[pl.ds(r1, GC), :] = ob
        return carry

    lax.fori_loop(0, nc, body, 0)

    for d in range(2):
        s_out_ref[0, d, 0] = st_ref[d]

    def finish(c, carry):
        r0 = pl.multiple_of(c * GC, GC)
        o = o_ref[pl.ds(r0, GC), :] + ob_ref[pl.ds(r0, GC), :]
        y = o * lax.rsqrt(jnp.mean(o * o, axis=-1, keepdims=True) + EPS) * ng_ref[...]
        o_ref[pl.ds(r0, GC), :] = y * _silu(z_ref[pl.ds(r0, GC), :])
        return carry

    lax.fori_loop(0, nc, finish, 0)


def _gdn(q, k, v, z, gbeta, grow, row_blk0, batch, t, s0, norm_g):
    nc = t // GC
    tok = lambda b, h: (row_blk0 + b, h)
    return pl.pallas_call(
        functools.partial(_gdn_kernel, t=t),
        grid=(batch, H_C),
        in_specs=[pl.BlockSpec((t, DK_C), tok), pl.BlockSpec((t, DK_C), tok),
                  pl.BlockSpec((t, DV_C), tok), pl.BlockSpec((t, DV_C), tok),
                  pl.BlockSpec((t, 4 * H_C), lambda b, h: (row_blk0 + b, 0)),
                  pl.BlockSpec((1, 2, 1, nc, GC), lambda b, h: (b, 0, h, 0, 0)),
                  pl.BlockSpec((1, 2, 1, DK_C, DV_C), lambda b, h: (b, 0, h, 0, 0)),
                  pl.BlockSpec((1, DV_C), lambda b, h: (0, 0))],
        out_specs=[pl.BlockSpec((t, DV_C), lambda b, h: (b, h)),
                   pl.BlockSpec((1, 2, 1, DK_C, DV_C), lambda b, h: (b, 0, h, 0, 0))],
        out_shape=[jax.ShapeDtypeStruct((batch * t, C_V), F32),
                   jax.ShapeDtypeStruct((batch, 2, H_C, DK_C, DV_C), F32)],
        scratch_shapes=[pltpu.VMEM((t, DV_C), F32), pltpu.VMEM((2, DK_C, DV_C), F32)],
        compiler_params=_cparams("parallel", "parallel"),
    )(q, k, v, z, gbeta, grow, s0, norm_g.reshape(1, DV_C))


FFN_TM = 512


def _ffn_pre_kernel(x_ref, shift_ref, scale_ref, rw_ref, w13_ref, w2_ref, h_ref, logit_ref, sh_ref):
    h = x_ref[...] * (1.0 + scale_ref[0]) + shift_ref[0]
    hb = h.astype(BF16)
    h_ref[...] = hb
    logit_ref[...] = _dot_hp(h, rw_ref[...])
    up = jnp.dot(hb, w13_ref[...], preferred_element_type=F32)
    hid = _silu(up[:, :D_SHARED]) * up[:, D_SHARED:]
    sh_ref[...] = jnp.dot(hid.astype(BF16), w2_ref[...], preferred_element_type=F32)


def _ffn_pre(x, mod3, router_w_pad, ws13, ws2):
    n = x.shape[0]
    tm = FFN_TM
    row = lambda i: _mod_row_of_tile(i, tm)
    return pl.pallas_call(
        _ffn_pre_kernel,
        grid=(n // tm,),
        in_specs=[pl.BlockSpec((tm, D_MODEL), lambda i: (i, 0)),
                  pl.BlockSpec((1, 1, D_MODEL), lambda i: (row(i), 0, 3)),
                  pl.BlockSpec((1, 1, D_MODEL), lambda i: (row(i), 0, 4)),
                  pl.BlockSpec((D_MODEL, LANES), lambda i: (0, 0)),
                  pl.BlockSpec((D_MODEL, 2 * D_SHARED), lambda i: (0, 0)),
                  pl.BlockSpec((D_SHARED, D_MODEL), lambda i: (0, 0))],
        out_specs=[pl.BlockSpec((tm, D_MODEL), lambda i: (i, 0)),
                   pl.BlockSpec((tm, LANES), lambda i: (i, 0)),
                   pl.BlockSpec((tm, D_MODEL), lambda i: (i, 0))],
        out_shape=[jax.ShapeDtypeStruct((n, D_MODEL), BF16),
                   jax.ShapeDtypeStruct((n, LANES), F32),
                   jax.ShapeDtypeStruct((n, D_MODEL), F32)],
        compiler_params=_cparams("parallel"),
    )(x, mod3, mod3, router_w_pad, ws13, ws2)


def _experts_kernel(be_ref, x_ref, w1_ref, w3_ref, w2_ref, o_ref, w1b, w3b, w2b):
    i = pl.program_id(0)
    changed = jnp.logical_or(i == 0, be_ref[i] != be_ref[jnp.maximum(i - 1, 0)])

    @pl.when(changed)
    def _():
        w1b[...] = w1_ref[0].astype(BF16)
        w3b[...] = w3_ref[0].astype(BF16)
        w2b[...] = w2_ref[0].astype(BF16)

    xb = x_ref[...]
    hid = _silu(jnp.dot(xb, w1b[...], preferred_element_type=F32)) * \
        jnp.dot(xb, w3b[...], preferred_element_type=F32)
    o_ref[...] = jnp.dot(hid.astype(BF16), w2b[...], preferred_element_type=F32)


def _grouped_experts(xs, block_expert, w1, w3, w2):
    rows = xs.shape[0]
    n_blocks = rows // MOE_BLOCK
    grid_spec = pltpu.PrefetchScalarGridSpec(
        num_scalar_prefetch=1,
        grid=(n_blocks,),
        in_specs=[pl.BlockSpec((MOE_BLOCK, D_MODEL), lambda i, be: (i, 0)),
                  pl.BlockSpec((1, D_MODEL, D_EXPERT), lambda i, be: (be[i], 0, 0)),
                  pl.BlockSpec((1, D_MODEL, D_EXPERT), lambda i, be: (be[i], 0, 0)),
                  pl.BlockSpec((1, D_EXPERT, D_MODEL), lambda i, be: (be[i], 0, 0))],
        out_specs=pl.BlockSpec((MOE_BLOCK, D_MODEL), lambda i, be: (i, 0)),
        scratch_shapes=[pltpu.VMEM((D_MODEL, D_EXPERT), BF16), pltpu.VMEM((D_MODEL, D_EXPERT), BF16),
                        pltpu.VMEM((D_EXPERT, D_MODEL), BF16)])
    return pl.pallas_call(
        _experts_kernel,
        grid_spec=grid_spec,
        out_shape=jax.ShapeDtypeStruct((rows, D_MODEL), F32),
        compiler_params=_cparams("arbitrary"),
    )(block_expert, xs, w1, w3, w2)


def _ffn_post_kernel(x_ref, r_ref, s_ref, gate_ref, g_ref, b_ref, o_ref):
    r = DEEPNORM_ALPHA * x_ref[...] + gate_ref[0] * (r_ref[...] + s_ref[...])
    o_ref[...] = _layer_norm_rows(r, g_ref[...], b_ref[...])


def _ffn_post(x, routed, shared, mod3, ln_g, ln_b):
    n = x.shape[0]
    tm = FFN_TM
    row = lambda i: _mod_row_of_tile(i, tm)
    tile = pl.BlockSpec((tm, D_MODEL), lambda i: (i, 0))
    vec = pl.BlockSpec((1, D_MODEL), lambda i: (0, 0))
    return pl.pallas_call(
        _ffn_post_kernel,
        grid=(n // tm,),
        in_specs=[tile, tile, tile, pl.BlockSpec((1, 1, D_MODEL), lambda i: (row(i), 0, 5)), vec, vec],
        out_specs=tile,
        out_shape=jax.ShapeDtypeStruct((n, D_MODEL), F32),
        compiler_params=_cparams("parallel"),
    )(x, routed, shared, mod3, ln_g.reshape(1, D_MODEL), ln_b.reshape(1, D_MODEL))


def _route(logits, router_bias):
    n = logits.shape[0]
    scores = jax.nn.sigmoid(logits)
    sel = scores + router_bias.astype(F32)
    per_group = N_EXPERTS // N_GROUPS
    grp_score = jnp.sum(lax.top_k(sel.reshape(n, N_GROUPS, per_group), 2)[0], axis=-1)
    _, top_grp = lax.top_k(grp_score, TOPK_GROUPS)
    grp_mask = jnp.sum(jax.nn.one_hot(top_grp, N_GROUPS, dtype=F32), axis=1) > 0
    exp_mask = jnp.repeat(grp_mask, per_group, axis=1)
    _, idx = lax.top_k(jnp.where(exp_mask, sel, -jnp.inf), TOP_K)
    w = jnp.take_along_axis(scores, idx, axis=1)
    w = w / jnp.sum(w, axis=-1, keepdims=True) * ROUTED_SCALE
    return idx, w


def _moe_routed(hb, idx, w, w1, w3, w2):
    n = hb.shape[0]
    a = n * TOP_K
    e_flat = idx.reshape(-1)
    order = jnp.argsort(e_flat)
    e_sorted = e_flat[order]
    counts = jnp.bincount(e_flat, length=N_EXPERTS)
    padded = (counts + MOE_BLOCK - 1) // MOE_BLOCK * MOE_BLOCK
    pad_end = jnp.cumsum(padded)
    pad_start = pad_end - padded
    start = jnp.cumsum(counts) - counts
    dest_sorted = pad_start[e_sorted] + jnp.arange(a) - start[e_sorted]
    n_blocks = a // MOE_BLOCK + N_EXPERTS
    rows = n_blocks * MOE_BLOCK
    tok_sorted = (order // TOP_K).astype(jnp.int32)
    row_tok = jnp.full((rows,), n, jnp.int32).at[dest_sorted].set(tok_sorted)
    dest = jnp.zeros((a,), jnp.int32).at[order].set(dest_sorted.astype(jnp.int32)).reshape(n, TOP_K)
    block_expert = jnp.minimum(
        jnp.searchsorted(pad_end, jnp.arange(n_blocks) * MOE_BLOCK, side='right'), N_EXPERTS - 1).astype(jnp.int32)
    x_pad = jnp.concatenate([hb, jnp.zeros((1, D_MODEL), hb.dtype)], axis=0)
    xs = x_pad[row_tok]
    ys = _grouped_experts(xs, block_expert, w1, w3, w2)
    return jnp.einsum('nkd,nk->nd', ys[dest], w)


def _rms_heads(x, g):
    y = x * lax.rsqrt(jnp.mean(x * x, axis=-1, keepdims=True) + EPS)
    return y * g


def _rope_tables():
    rows = DEC_SEQ // GRID_W
    row = jnp.repeat(jnp.arange(rows), GRID_W).astype(F32)
    colp = jnp.tile(jnp.arange(GRID_W), rows).astype(F32)
    nf = HD_A // 4
    inv_freq = ROPE_THETA ** (-jnp.arange(nf, dtype=F32) / nf)
    ar = row[:, None] * inv_freq[None, :]
    ac = colp[:, None] * inv_freq[None, :]
    cos = jnp.concatenate([jnp.cos(ar), jnp.cos(ar), jnp.cos(ac), jnp.cos(ac)], axis=-1)
    sin = jnp.concatenate([-jnp.sin(ar), jnp.sin(ar), -jnp.sin(ac), jnp.sin(ac)], axis=-1)
    return cos, sin


def _rope(x, cos, sin):
    a, b, c, d = jnp.split(x, 4, axis=-1)
    sw = jnp.concatenate([b, a, d, c], axis=-1)
    return x * cos[None, :, None, :] + sw * sin[None, :, None, :]


def _expand_q(q):
    n = q.shape[0]
    z = jnp.zeros_like(q)
    g0 = jnp.concatenate([q, z], axis=-1)
    g1 = jnp.concatenate([z, q], axis=-1)
    sel = (jnp.arange(N_HEADS_A) // GROUP_A)[None, :, None]
    return jnp.where(sel == 0, g0, g1).reshape(n, N_HEADS_A * LANES)


_ATT_HEAD_ORDER = [g * GROUP_A + j for j in range(GROUP_A) for g in range(N_KV_A)]


def _even_layer(x, mod3, j, w_in, w_out, q_norm, k_norm, log_decay, gn_g,
                cache_k, cache_v, state_ret):
    (p,) = _modulated_proj(x, mod3, 0, w_in.astype(BF16), [EVEN_IN], [F32])
    q = _rms_heads(p[:, :A_Q].reshape(N_TOK, N_HEADS_A, HD_A), q_norm)
    k = _rms_heads(p[:, A_Q:A_Q + A_KV].reshape(N_TOK, N_KV_A, HD_A), k_norm)
    v = p[:, A_Q + A_KV:A_Q + 2 * A_KV].reshape(N_TOK, N_KV_A, HD_A)
    k_p = k[:N_P].reshape(BATCH, SEQ, N_KV_A, HD_A)
    v_p = v[:N_P].reshape(BATCH, SEQ, N_KV_A, HD_A)
    scale = HD_A ** -0.5
    qp = _expand_q(q[:N_P] * scale).astype(BF16)
    ktp = jnp.swapaxes(k_p.reshape(BATCH, SEQ, A_KV), 1, 2).astype(BF16)
    o_p = _attention(qp, ktp, v_p.reshape(BATCH, SEQ, A_KV).astype(BF16), BATCH, SEQ)
    cos, sin = _rope_tables()
    qs = _rope(q[N_P:].reshape(DEC_BATCH, DEC_SEQ, N_HEADS_A, HD_A), cos, sin)
    ks = _rope(k[N_P:].reshape(DEC_BATCH, DEC_SEQ, N_KV_A, HD_A), cos, sin)
    k_all = jnp.concatenate([ks, cache_k], axis=1).reshape(DEC_BATCH, DEC_SEQ + PAST_LEN, A_KV)
    v_all = jnp.concatenate([v[N_P:].reshape(DEC_BATCH, DEC_SEQ, N_KV_A, HD_A), cache_v], axis=1)
    v_all = v_all.reshape(DEC_BATCH, DEC_SEQ + PAST_LEN, A_KV)
    qs = _expand_q(qs.reshape(N_S, N_HEADS_A, HD_A) * scale).astype(BF16)
    o_s = _attention(qs, jnp.swapaxes(k_all, 1, 2).astype(BF16), v_all.astype(BF16), DEC_BATCH, DEC_SEQ)
    o_attn = jnp.concatenate([o_p, o_s], axis=0)
    lg_rows = jnp.broadcast_to(log_decay.reshape(2 * H_B, 1), (2 * H_B, LANES))
    zeros_s = jnp.zeros((BATCH, 2, H_B, DK_B, DV_B), F32)
    r_p, s_p = _retention(p, 0, BATCH, SEQ, lg_rows, zeros_s, gn_g)
    r_s, _ = _retention(p, N_P // DEC_SEQ, DEC_BATCH, DEC_SEQ, lg_rows, state_ret, gn_g)
    o_ret = jnp.concatenate([r_p, r_s], axis=0)
    perm = np.concatenate([np.arange(h * HD_A, (h + 1) * HD_A) for h in _ATT_HEAD_ORDER])
    w_o = jnp.concatenate([w_out[:A_Q][perm], w_out[A_Q:]], axis=0).astype(BF16)
    return [o_attn, o_ret], w_o, k_p, v_p, s_p


def _conv_silu(x, w, batch, t):
    xb = x.reshape(batch, t, x.shape[-1])
    xp = jnp.pad(xb, ((0, 0), ((CONV_K - 1) // 2, CONV_K // 2), (0, 0)))
    y = sum(xp[:, i:i + t] * w[i][None, None, :] for i in range(CONV_K))
    return jax.nn.silu(y).reshape(batch * t, x.shape[-1])


def _l2_heads(x):
    xh = x.reshape(x.shape[0], H_C, DK_C)
    return (xh * lax.rsqrt(jnp.sum(xh * xh, axis=-1, keepdims=True) + EPS)).reshape(x.shape)


def _chunk_cumsum(g, batch, t):
    gc = g.reshape(batch, t // GC, GC, 2, H_C)
    f = jnp.cumsum(gc[:, :, :, 0], axis=2)
    b = jnp.cumsum(gc[:, :, ::-1, 1], axis=2)[:, :, ::-1]
    return jnp.stack([f, b], axis=3)


def _odd_layer(x, mod3, j, w_in, conv_w, a_log, dt_bias, norm_g, w_out, state_gdn):
    w_main = w_in[:, :2 * C_QK + 2 * C_V].astype(BF16)
    w_ab = jnp.pad(w_in[:, 2 * C_QK + 2 * C_V:], ((0, 0), (0, LANES - 4 * H_C))).astype(BF16)
    w_cat = jnp.concatenate([w_main, w_ab], axis=1)
    pm, pab = _modulated_proj(x, mod3, 0, w_cat, [2 * C_QK + 2 * C_V, LANES], [F32, F32])
    qkv = pm[:, :2 * C_QK + C_V]
    z = pm[:, 2 * C_QK + C_V:]
    qkv = jnp.concatenate([_conv_silu(qkv[:N_P], conv_w, BATCH, SEQ),
                           _conv_silu(qkv[N_P:], conv_w, DEC_BATCH, DEC_SEQ)], axis=0)
    q = _l2_heads(qkv[:, :C_QK])
    k = _l2_heads(qkv[:, C_QK:2 * C_QK])
    v = qkv[:, 2 * C_QK:]
    ab = pab[:, :4 * H_C].reshape(N_TOK, 2, 2, H_C)
    beta = jax.nn.sigmoid(ab[:, 0])
    g = -jnp.exp(a_log.astype(F32)) * jax.nn.softplus(ab[:, 1] + dt_bias.astype(F32))

    gcs_p = _chunk_cumsum(g[:N_P], BATCH, SEQ)
    gcs_s = _chunk_cumsum(g[N_P:], DEC_BATCH, DEC_SEQ)
    gcol = jnp.concatenate([gcs_p.reshape(N_P, 2 * H_C), gcs_s.reshape(N_S, 2 * H_C)], axis=0)
    gbeta = jnp.concatenate([beta.reshape(N_TOK, 2 * H_C), gcol], axis=1)
    zeros_s = jnp.zeros((BATCH, 2, H_C, DK_C, DV_C), F32)
    o_p, s_p = _gdn(q, k, v, z, gbeta, gcs_p.transpose(0, 3, 4, 1, 2), 0, BATCH, SEQ, zeros_s, norm_g)
    o_s, _ = _gdn(q, k, v, z, gbeta, gcs_s.transpose(0, 3, 4, 1, 2), N_P // DEC_SEQ, DEC_BATCH, DEC_SEQ,
                  state_gdn, norm_g)
    return [jnp.concatenate([o_p, o_s], axis=0)], w_out.astype(BF16), s_p


def kernel(x_prompt, x_sample, cache_attn_k, cache_attn_v, state_ret, state_gdn, c, c_ctx, mod_w, mod_b, ln_g, ln_b, even_w_in, even_w_out, attn_q_norm, attn_k_norm, ret_log_decay, ret_norm_g, odd_w_in, gdn_conv_w, gdn_a_log, gdn_dt_bias, gdn_norm_g, odd_w_out, router_w, router_bias, expert_w1, expert_w3, expert_w2, shared_w1, shared_w3, shared_w2):
    x = jnp.concatenate([x_prompt.reshape(N_P, D_MODEL), x_sample.reshape(N_S, D_MODEL)], axis=0)
    cvec = jnp.concatenate([c_ctx[None, :], c, jnp.zeros((MOD_ROWS - N_MOD, D_MODEL), F32)], axis=0)
    mod_all = _mod_vectors(cvec, mod_w, mod_b)
    new_k, new_v, new_ret, new_gdn = [], [], [], []
    for l in range(DEPTH):
        j = l // 2
        mod3 = mod_all[l].reshape(MOD_ROWS, 1, 6 * D_MODEL)
        if l % 2 == 0:
            a_list, w_o, k_p, v_p, s_p = _even_layer(
                x, mod3, j, even_w_in[j], even_w_out[j], attn_q_norm[j], attn_k_norm[j],
                ret_log_decay[j], ret_norm_g[j], cache_attn_k[:, j], cache_attn_v[:, j], state_ret[:, j])
            new_k.append(k_p)
            new_v.append(v_p)
            new_ret.append(s_p)
        else:
            a_list, w_o, s_p = _odd_layer(
                x, mod3, j, odd_w_in[j], gdn_conv_w[j], gdn_a_log[j], gdn_dt_bias[j], gdn_norm_g[j],
                odd_w_out[j], state_gdn[:, j])
            new_gdn.append(s_p)
        x = _outproj_ln(a_list, w_o, x, mod3, 2, ln_g[l, 0], ln_b[l, 0])
        rw = jnp.pad(router_w[l], ((0, 0), (0, LANES - N_EXPERTS)))
        ws13 = jnp.concatenate([shared_w1[l], shared_w3[l]], axis=1).astype(BF16)
        hb, logits, shared = _ffn_pre(x, mod3, rw, ws13, shared_w2[l].astype(BF16))
        idx, w = _route(logits[:, :N_EXPERTS], router_bias[l])
        routed = _moe_routed(hb, idx, w, expert_w1[l], expert_w3[l], expert_w2[l])
        x = _ffn_post(x, routed, shared, mod3, ln_g[l, 1], ln_b[l, 1])
    return (x[:N_P].reshape(BATCH, SEQ, D_MODEL), x[N_P:].reshape(DEC_BATCH, DEC_SEQ, D_MODEL),
            jnp.stack(new_k, axis=1), jnp.stack(new_v, axis=1),
            jnp.stack(new_ret, axis=1), jnp.stack(new_gdn, axis=1))
```

```python
import functools
import math

import jax
import jax.numpy as jnp
import numpy as np
from jax import lax
from jax.experimental import pallas as pl
from jax.experimental.pallas import tpu as pltpu

D_MODEL = 1024
BATCH = 16
SEQ = 256
DEPTH = 4
DEC_BATCH = 4
DEC_SEQ = 4096
PAST_LEN = 512
GRID_W = 64
N_HEADS_A = 8
N_KV_A = 2
HD_A = 64
ROPE_THETA = 10000.0
H_B = 4
DK_B = 64
DV_B = 128
RET_CHUNK = 128
H_C = 8
DK_C = 128
DV_C = 128
CONV_K = 5
GDN_CHUNK = 64
N_EXPERTS = 64
TOP_K = 8
N_GROUPS = 8
TOPK_GROUPS = 4
D_EXPERT = 256
D_SHARED = 256
ROUTED_SCALE = 2.5
MOE_BLOCK = 128
A_Q = N_HEADS_A * HD_A
A_KV = N_KV_A * HD_A
B_QK = H_B * DK_B
B_V = H_B * DV_B
EVEN_IN = A_Q + 2 * A_KV + 2 * B_QK + 2 * B_V
C_QK = H_C * DK_C
C_V = H_C * DV_C
DEEPNORM_ALPHA = (2 * DEPTH) ** 0.25
EPS = 1e-6

N_P = BATCH * SEQ
N_S = DEC_BATCH * DEC_SEQ
N_TOK = N_P + N_S
N_MOD = 1 + DEC_BATCH
MOD_ROWS = 8

LANES = 128
VMEM_LIMIT = 56 * 1024 * 1024

F32 = jnp.float32
BF16 = jnp.bfloat16


def _cparams(*sem):
    return pltpu.CompilerParams(dimension_semantics=sem, vmem_limit_bytes=VMEM_LIMIT)


def _bdot(a, b):
    return jnp.dot(a.astype(BF16), b.astype(BF16), preferred_element_type=F32)


def _bdot_t(a, b):
    return lax.dot_general(a.astype(BF16), b.astype(BF16), (((0,), (0,)), ((), ())),
                           preferred_element_type=F32)


def _bdot_nt(a, b):
    return lax.dot_general(a.astype(BF16), b.astype(BF16), (((1,), (1,)), ((), ())),
                           preferred_element_type=F32)


def _split3(a):
    hi = a.astype(BF16)
    r = a - hi.astype(F32)
    mid = r.astype(BF16)
    lo = (r - mid.astype(F32)).astype(BF16)
    return hi, mid, lo


def _dot_hp(a, b):
    a0, a1, a2 = _split3(a)
    b0, b1, b2 = _split3(b)
    d = lambda x, y: jnp.dot(x, y, preferred_element_type=F32)
    small = d(a0, b2) + d(a2, b0) + d(a1, b1)
    return (d(a0, b1) + d(a1, b0)) + small + d(a0, b0)


def _silu(x):
    return x * (1.0 / (1.0 + jnp.exp(-x)))


def _mod_row_of_tile(i, tile):
    tiles_p = N_P // tile
    tiles_per_b = DEC_SEQ // tile
    return jnp.where(i < tiles_p, 0, 1 + (i - tiles_p) // tiles_per_b)


MOD_TN = 1536


def _mod_kernel(c_ref, w_ref, b_ref, o_ref):
    a = _silu(c_ref[...])
    o_ref[0] = _bdot(a, w_ref[0]) + b_ref[0]


def _mod_vectors(cvec, mod_w, mod_b):
    n6 = 6 * D_MODEL
    return pl.pallas_call(
        _mod_kernel,
        grid=(DEPTH, n6 // MOD_TN),
        in_specs=[pl.BlockSpec((MOD_ROWS, D_MODEL), lambda l, j: (0, 0)),
                  pl.BlockSpec((1, D_MODEL, MOD_TN), lambda l, j: (l, 0, j)),
                  pl.BlockSpec((1, 1, MOD_TN), lambda l, j: (l, 0, j))],
        out_specs=pl.BlockSpec((1, MOD_ROWS, MOD_TN), lambda l, j: (l, 0, j)),
        out_shape=jax.ShapeDtypeStruct((DEPTH, MOD_ROWS, n6), F32),
        compiler_params=_cparams("parallel", "parallel"),
    )(cvec, mod_w, mod_b.reshape(DEPTH, 1, n6))


PROJ_TM = 256


def _proj_kernel(x_ref, shift_ref, scale_ref, w_ref, *o_refs, widths):
    h = (x_ref[...] * (1.0 + scale_ref[0]) + shift_ref[0]).astype(BF16)
    off = 0
    for o_ref, wd in zip(o_refs, widths):
        o_ref[...] = jnp.dot(h, w_ref[:, off:off + wd], preferred_element_type=F32).astype(o_ref.dtype)
        off += wd


def _modulated_proj(x, mod3, shift_blk, w_bf16, widths, dtypes):
    n = x.shape[0]
    tm = PROJ_TM
    row = lambda i: _mod_row_of_tile(i, tm)
    return pl.pallas_call(
        functools.partial(_proj_kernel, widths=tuple(widths)),
        grid=(n // tm,),
        in_specs=[pl.BlockSpec((tm, D_MODEL), lambda i: (i, 0)),
                  pl.BlockSpec((1, 1, D_MODEL), lambda i: (row(i), 0, shift_blk)),
                  pl.BlockSpec((1, 1, D_MODEL), lambda i: (row(i), 0, shift_blk + 1)),
                  pl.BlockSpec((D_MODEL, sum(widths)), lambda i: (0, 0))],
        out_specs=[pl.BlockSpec((tm, wd), lambda i: (i, 0)) for wd in widths],
        out_shape=[jax.ShapeDtypeStruct((n, wd), dt) for wd, dt in zip(widths, dtypes)],
        compiler_params=_cparams("parallel"),
    )(x, mod3, mod3, w_bf16)


OUT_TM = 512


def _layer_norm_rows(r, g, b):
    mu = jnp.mean(r, axis=-1, keepdims=True)
    d = r - mu
    var = jnp.mean(d * d, axis=-1, keepdims=True)
    return d * lax.rsqrt(var + EPS) * g + b


def _outproj_kernel(*refs, n_a):
    a_refs = refs[:n_a]
    w_ref, x_ref, gate_ref, g_ref, b_ref, o_ref = refs[n_a:]
    off = 0
    acc = None
    for a_ref in a_refs:
        wd = a_ref.shape[1]
        part = jnp.dot(a_ref[...].astype(BF16), w_ref[off:off + wd, :], preferred_element_type=F32)
        acc = part if acc is None else acc + part
        off += wd
    r = DEEPNORM_ALPHA * x_ref[...] + gate_ref[0] * acc
    o_ref[...] = _layer_norm_rows(r, g_ref[...], b_ref[...])


def _outproj_ln(a_list, w_bf16, x, mod3, gate_blk, ln_g, ln_b):
    n = x.shape[0]
    tm = OUT_TM
    row = lambda i: _mod_row_of_tile(i, tm)
    kdim = w_bf16.shape[0]
    return pl.pallas_call(
        functools.partial(_outproj_kernel, n_a=len(a_list)),
        grid=(n // tm,),
        in_specs=[pl.BlockSpec((tm, a.shape[1]), lambda i: (i, 0)) for a in a_list] + [
            pl.BlockSpec((kdim, D_MODEL), lambda i: (0, 0)),
            pl.BlockSpec((tm, D_MODEL), lambda i: (i, 0)),
            pl.BlockSpec((1, 1, D_MODEL), lambda i: (row(i), 0, gate_blk)),
            pl.BlockSpec((1, D_MODEL), lambda i: (0, 0)),
            pl.BlockSpec((1, D_MODEL), lambda i: (0, 0))],
        out_specs=pl.BlockSpec((tm, D_MODEL), lambda i: (i, 0)),
        out_shape=jax.ShapeDtypeStruct((n, D_MODEL), F32),
        compiler_params=_cparams("parallel"),
    )(*a_list, w_bf16, x, mod3, ln_g.reshape(1, D_MODEL), ln_b.reshape(1, D_MODEL))


ATT_TQ = 256
GROUP_A = N_HEADS_A // N_KV_A


def _attn_kernel(q_ref, kt_ref, v_ref, o_ref):
    kt = kt_ref[0]
    v = v_ref[0]
    lane = lax.broadcasted_iota(jnp.int32, (1, LANES), 1)
    for j in range(GROUP_A):
        outs = []
        for g in range(N_KV_A):
            h = g * GROUP_A + j
            s = jnp.dot(q_ref[:, h * LANES:(h + 1) * LANES], kt, preferred_element_type=F32)
            m = jnp.max(s, axis=-1, keepdims=True)
            p = jnp.exp(s - m)
            l = jnp.sum(p, axis=-1, keepdims=True)
            pv = jnp.dot(p.astype(BF16), v, preferred_element_type=F32)
            outs.append(pv * (1.0 / l))
        o_ref[:, j * LANES:(j + 1) * LANES] = jnp.where(lane < HD_A, outs[0], outs[1]).astype(o_ref.dtype)


def _attention(q_exp, kt, v, batch, t):
    tk = v.shape[1]
    tq = min(ATT_TQ, t)
    nq = t // tq
    return pl.pallas_call(
        _attn_kernel,
        grid=(batch, nq),
        in_specs=[pl.BlockSpec((tq, N_HEADS_A * LANES), lambda b, i: (b * nq + i, 0)),
                  pl.BlockSpec((1, LANES, tk), lambda b, i: (b, 0, 0)),
                  pl.BlockSpec((1, tk, LANES), lambda b, i: (b, 0, 0))],
        out_specs=pl.BlockSpec((tq, A_Q), lambda b, i: (b * nq + i, 0)),
        out_shape=jax.ShapeDtypeStruct((batch * t, A_Q), BF16),
        compiler_params=_cparams("parallel", "parallel"),
    )(q_exp, kt, v)


RC = RET_CHUNK


def _ret_kernel(q_ref, k_ref, v_ref, g_ref, lg_ref, s0_ref, gn_ref, o_ref, s_out_ref,
                ob_ref, st_ref, *, t):
    nc = t // RC
    hp = pl.program_id(1)
    ii = lax.broadcasted_iota(jnp.int32, (RC, RC), 0).astype(F32)
    jj = lax.broadcasted_iota(jnp.int32, (RC, RC), 1).astype(F32)
    col_i = lax.broadcasted_iota(jnp.int32, (RC, 1), 0).astype(F32)
    lane = lax.broadcasted_iota(jnp.int32, (1, LANES), 1)
    masks = [(lane >= hh * DK_B) & (lane < (hh + 1) * DK_B) for hh in range(2)]

    consts = []
    for d in range(2):
        for hh in range(2):
            lg = lg_ref[pl.ds(d * H_B + hp * 2 + hh, 1), :][:, :1]
            if d == 0:
                diff = ii - jj
                qe, ke = col_i + 1.0, (RC - 1.0) - col_i
            else:
                diff = jj - ii
                qe, ke = RC - col_i, col_i
            intra = jnp.where(diff >= 0, jnp.exp(jnp.maximum(diff, 0.0) * lg), 0.0)
            consts.append((intra, jnp.exp(qe * lg), jnp.exp(ke * lg), jnp.exp(RC * lg)))
            s0 = s0_ref[0, d, hh]
            z = jnp.zeros((DK_B, DV_B), F32)
            st_ref[d * 2 + hh] = jnp.concatenate([s0, z] if hh == 0 else [z, s0], axis=0)

    def chunk(c, d):
        r0 = pl.multiple_of(c * RC, RC)
        qc = q_ref[pl.ds(r0, RC), :]
        kc = k_ref[pl.ds(r0, RC), :] * (DK_B ** -0.5)
        outs = []
        for hh in range(2):
            intra, q_dec, k_dec, c_dec = consts[d * 2 + hh]
            vc = v_ref[pl.ds(r0, RC), hh * DV_B:(hh + 1) * DV_B]
            qh = jnp.where(masks[hh], qc, 0.0)
            kh = jnp.where(masks[hh], kc, 0.0)
            s = st_ref[d * 2 + hh]
            scores = _bdot_nt(qh, kh) * intra
            o = _bdot(scores, vc) + _bdot(qh * q_dec, s)
            st_ref[d * 2 + hh] = s * c_dec + _bdot_t(kh * k_dec, vc)
            outs.append(o)
        return r0, jnp.concatenate(outs, axis=1)

    def body(c, carry):
        r0, of = chunk(c, 0)
        o_ref[pl.ds(r0, RC), :] = of
        r1, ob = chunk(nc - 1 - c, 1)
        ob_ref[pl.ds(r1, RC), :] = ob
        return carry

    lax.fori_loop(0, nc, body, 0)

    for d in range(2):
        for hh in range(2):
            s_out_ref[0, d, hh] = st_ref[d * 2 + hh][hh * DK_B:(hh + 1) * DK_B, :]

    def finish(c, carry):
        r0 = pl.multiple_of(c * RC, RC)
        o = o_ref[pl.ds(r0, RC), :] + ob_ref[pl.ds(r0, RC), :]
        gate = _silu(g_ref[pl.ds(r0, RC), :])
        ys = []
        for hh in range(2):
            oh = o[:, hh * DV_B:(hh + 1) * DV_B]
            mu = jnp.mean(oh, axis=-1, keepdims=True)
            dlt = oh - mu
            var = jnp.mean(dlt * dlt, axis=-1, keepdims=True)
            ys.append(dlt * lax.rsqrt(var + EPS))
        o_ref[pl.ds(r0, RC), :] = jnp.concatenate(ys, axis=1) * gn_ref[...] * gate
        return carry

    lax.fori_loop(0, nc, finish, 0)


def _retention(p, row_blk0, batch, t, lg_rows, s0, gn_g):
    qb, kb, vb, gb = (A_Q + 2 * A_KV) // LANES, (A_Q + 2 * A_KV + B_QK) // LANES, \
        (A_Q + 2 * A_KV + 2 * B_QK) // (2 * DV_B), (A_Q + 2 * A_KV + 2 * B_QK + B_V) // (2 * DV_B)
    return pl.pallas_call(
        functools.partial(_ret_kernel, t=t),
        grid=(batch, H_B // 2),
        in_specs=[pl.BlockSpec((t, LANES), lambda b, h: (row_blk0 + b, qb + h)),
                  pl.BlockSpec((t, LANES), lambda b, h: (row_blk0 + b, kb + h)),
                  pl.BlockSpec((t, 2 * DV_B), lambda b, h: (row_blk0 + b, vb + h)),
                  pl.BlockSpec((t, 2 * DV_B), lambda b, h: (row_blk0 + b, gb + h)),
                  pl.BlockSpec((2 * H_B, LANES), lambda b, h: (0, 0)),
                  pl.BlockSpec((1, 2, 2, DK_B, DV_B), lambda b, h: (b, 0, h, 0, 0)),
                  pl.BlockSpec((1, 2 * DV_B), lambda b, h: (0, h))],
        out_specs=[pl.BlockSpec((t, 2 * DV_B), lambda b, h: (b, h)),
                   pl.BlockSpec((1, 2, 2, DK_B, DV_B), lambda b, h: (b, 0, h, 0, 0))],
        out_shape=[jax.ShapeDtypeStruct((batch * t, B_V), F32),
                   jax.ShapeDtypeStruct((batch, 2, H_B, DK_B, DV_B), F32)],
        scratch_shapes=[pltpu.VMEM((t, 2 * DV_B), F32), pltpu.VMEM((4, LANES, DV_B), F32)],
        compiler_params=_cparams("parallel", "parallel"),
    )(p, p, p, p, lg_rows, s0, gn_g.reshape(1, B_V))


GC = GDN_CHUNK


def _unit_tri_inverse(a):
    ii = lax.broadcasted_iota(jnp.int32, (GC, GC), 0)
    jj = lax.broadcasted_iota(jnp.int32, (GC, GC), 1)

    def same_block(shift):
        return (ii >> shift) == (jj >> shift)

    eye = (ii == jj).astype(F32)
    m = jnp.where(same_block(3), -a, 0.0)
    inv = eye + m
    m2 = _bdot(m, m)
    inv = inv + _bdot(inv, m2)
    m4 = _bdot(m2, m2)
    inv = inv + _bdot(inv, m4)
    for shift in range(3, int(math.log2(GC))):
        l = jnp.where(same_block(shift + 1) & jnp.logical_not(same_block(shift)), a, 0.0)
        inv = inv - _bdot(_bdot(inv, l), inv)
    return inv


def _gdn_kernel(q_ref, k_ref, v_ref, z_ref, gb_ref, gr_ref, s0_ref, ng_ref, o_ref, s_out_ref,
                ob_ref, st_ref, w_s, u_s, a_s, qg_s, kg_s, *, t):
    nc = t // GC
    h = pl.program_id(1)
    ii = lax.broadcasted_iota(jnp.int32, (GC, GC), 0)
    jj = lax.broadcasted_iota(jnp.int32, (GC, GC), 1)
    incl = [ii >= jj, jj >= ii]
    strict = [ii > jj, jj > ii]
    lane32 = lax.broadcasted_iota(jnp.int32, (1, 4 * H_C), 1)
    for d in range(2):
        st_ref[d] = s0_ref[0, d, 0]

    def col(x, idx):
        return jnp.sum(jnp.where(lane32 == idx, x, 0.0), axis=1, keepdims=True)

    def gate_last(c, d):
        grow = gr_ref[0, d, 0, pl.ds(c, 1), :]
        return grow, (grow[:, GC - 1:GC] if d == 0 else grow[:, 0:1])

    def prep(c, carry):
        rows = pl.ds(pl.multiple_of(c * GC, GC), GC)
        qc = q_ref[rows, :] * (DK_C ** -0.5)
        kc = k_ref[rows, :]
        vc = v_ref[rows, :]
        gb = gb_ref[rows, :]
        kk = _bdot_nt(kc, kc)
        qk = _bdot_nt(qc, kc)
        for d in range(2):
            beta = col(gb, d * H_C + h)
            gcol = col(gb, 2 * H_C + d * H_C + h)
            grow, glast = gate_last(c, d)
            decay = jnp.exp(jnp.where(incl[d], gcol - grow, -jnp.inf))
            eg = jnp.exp(gcol)
            kb = kc * beta
            a = jnp.where(strict[d], kk * beta * decay, 0.0)
            tinv = _unit_tri_inverse(a)
            wu = _bdot(tinv, jnp.concatenate([kb * eg, vc * beta], axis=1))
            w_s[d, rows, :] = wu[:, :DK_C].astype(BF16)
            u_s[d, rows, :] = wu[:, DK_C:]
            a_s[d, rows, :] = (qk * decay).astype(BF16)
            qg_s[d, rows, :] = (qc * eg).astype(BF16)
            kg_s[d, rows, :] = (kc * jnp.exp(glast - gcol)).astype(BF16)
        return carry

    lax.fori_loop(0, nc, prep, 0, unroll=2)

    def step(c, carry):
        for d in range(2):
            cc = c if d == 0 else nc - 1 - c
            rows = pl.ds(pl.multiple_of(cc * GC, GC), GC)
            _, glast = gate_last(cc, d)
            s = st_ref[d]
            sb = s.astype(BF16)
            v_new = u_s[d, rows, :] - jnp.dot(w_s[d, rows, :], sb, preferred_element_type=F32)
            vnb = v_new.astype(BF16)
            o = jnp.dot(qg_s[d, rows, :], sb, preferred_element_type=F32) + \
                jnp.dot(a_s[d, rows, :], vnb, preferred_element_type=F32)
            st_ref[d] = s * jnp.exp(glast) + _bdot_t(kg_s[d, rows, :], vnb)
            if d == 0:
                o_ref[rows, :] = o
            else:
                ob_ref[rows, :] = o
        return carry

    lax.fori_loop(0, nc, step, 0)

    for d in range(2):
        s_out_ref[0, d, 0] = st_ref[d]

    def finish(c, carry):
        r0 = pl.multiple_of(c * GC, GC)
        o = o_ref[pl.ds(r0, GC), :] + ob_ref[pl.ds(r0, GC), :]
        y = o * lax.rsqrt(jnp.mean(o * o, axis=-1, keepdims=True) + EPS) * ng_ref[...]
        o_ref[pl.ds(r0, GC), :] = y * _silu(z_ref[pl.ds(r0, GC), :])
        return carry

    lax.fori_loop(0, nc, finish, 0)


def _gdn(q, k, v, z, gbeta, grow, row_blk0, batch, t, s0, norm_g):
    nc = t // GC
    tok = lambda b, h: (row_blk0 + b, h)
    return pl.pallas_call(
        functools.partial(_gdn_kernel, t=t),
        grid=(batch, H_C),
        in_specs=[pl.BlockSpec((t, DK_C), tok), pl.BlockSpec((t, DK_C), tok),
                  pl.BlockSpec((t, DV_C), tok), pl.BlockSpec((t, DV_C), tok),
                  pl.BlockSpec((t, 4 * H_C), lambda b, h: (row_blk0 + b, 0)),
                  pl.BlockSpec((1, 2, 1, nc, GC), lambda b, h: (b, 0, h, 0, 0)),
                  pl.BlockSpec((1, 2, 1, DK_C, DV_C), lambda b, h: (b, 0, h, 0, 0)),
                  pl.BlockSpec((1, DV_C), lambda b, h: (0, 0))],
        out_specs=[pl.BlockSpec((t, DV_C), lambda b, h: (b, h)),
                   pl.BlockSpec((1, 2, 1, DK_C, DV_C), lambda b, h: (b, 0, h, 0, 0))],
        out_shape=[jax.ShapeDtypeStruct((batch * t, C_V), F32),
                   jax.ShapeDtypeStruct((batch, 2, H_C, DK_C, DV_C), F32)],
        scratch_shapes=[pltpu.VMEM((t, DV_C), F32), pltpu.VMEM((2, DK_C, DV_C), F32),
                        pltpu.VMEM((2, t, DK_C), BF16), pltpu.VMEM((2, t, DV_C), F32),
                        pltpu.VMEM((2, t, GC), BF16), pltpu.VMEM((2, t, DK_C), BF16),
                        pltpu.VMEM((2, t, DK_C), BF16)],
        compiler_params=_cparams("parallel", "parallel"),
    )(q, k, v, z, gbeta, grow, s0, norm_g.reshape(1, DV_C))


FFN_TM = 512


def _ffn_pre_kernel(x_ref, shift_ref, scale_ref, rw_ref, w13_ref, w2_ref, h_ref, logit_ref, sh_ref):
    h = x_ref[...] * (1.0 + scale_ref[0]) + shift_ref[0]
    hb = h.astype(BF16)
    h_ref[...] = hb
    logit_ref[...] = _dot_hp(h, rw_ref[...])
    up = jnp.dot(hb, w13_ref[...], preferred_element_type=F32)
    hid = _silu(up[:, :D_SHARED]) * up[:, D_SHARED:]
    sh_ref[...] = jnp.dot(hid.astype(BF16), w2_ref[...], preferred_element_type=F32)


def _ffn_pre(x, mod3, router_w_pad, ws13, ws2):
    n = x.shape[0]
    tm = FFN_TM
    row = lambda i: _mod_row_of_tile(i, tm)
    return pl.pallas_call(
        _ffn_pre_kernel,
        grid=(n // tm,),
        in_specs=[pl.BlockSpec((tm, D_MODEL), lambda i: (i, 0)),
                  pl.BlockSpec((1, 1, D_MODEL), lambda i: (row(i), 0, 3)),
                  pl.BlockSpec((1, 1, D_MODEL), lambda i: (row(i), 0, 4)),
                  pl.BlockSpec((D_MODEL, LANES), lambda i: (0, 0)),
                  pl.BlockSpec((D_MODEL, 2 * D_SHARED), lambda i: (0, 0)),
                  pl.BlockSpec((D_SHARED, D_MODEL), lambda i: (0, 0))],
        out_specs=[pl.BlockSpec((tm, D_MODEL), lambda i: (i, 0)),
                   pl.BlockSpec((tm, LANES), lambda i: (i, 0)),
                   pl.BlockSpec((tm, D_MODEL), lambda i: (i, 0))],
        out_shape=[jax.ShapeDtypeStruct((n, D_MODEL), BF16),
                   jax.ShapeDtypeStruct((n, LANES), F32),
                   jax.ShapeDtypeStruct((n, D_MODEL), F32)],
        compiler_params=_cparams("parallel"),
    )(x, mod3, mod3, router_w_pad, ws13, ws2)


ROUTE_T = 512
PER_GROUP = N_EXPERTS // N_GROUPS
NEG_INF = float("-inf")


def _first_max(x, iota, n):
    m = jnp.max(x, axis=0, keepdims=True)
    first = jnp.min(jnp.where(x == m, iota, n), axis=0, keepdims=True)
    return m, iota == first


def _route_kernel(lg_ref, bias_ref, tri_ref, idx_ref, w_ref, rank_ref, cnt_ref, carry_ref):
    @pl.when(pl.program_id(0) == 0)
    def _():
        carry_ref[...] = jnp.zeros_like(carry_ref)

    t = lg_ref.shape[0]
    logits = jnp.transpose(lg_ref[...])[:N_EXPERTS]
    scores = 1.0 / (1.0 + jnp.exp(-logits))
    sel = scores + bias_ref[...]
    sub_g = lax.broadcasted_iota(jnp.int32, (PER_GROUP, t), 0)
    sub_e = lax.broadcasted_iota(jnp.int32, (N_EXPERTS, t), 0)
    grp_rows = []
    for g in range(N_GROUPS):
        x = sel[g * PER_GROUP:(g + 1) * PER_GROUP]
        m1, hit = _first_max(x, sub_g, PER_GROUP)
        m2 = jnp.max(jnp.where(hit, NEG_INF, x), axis=0, keepdims=True)
        grp_rows.append(m1 + m2)
    cur = jnp.concatenate(grp_rows, axis=0)
    sub_grp = lax.broadcasted_iota(jnp.int32, (N_GROUPS, t), 0)
    grp_on = jnp.zeros((N_GROUPS, t), F32)
    for _ in range(TOPK_GROUPS):
        _, hit = _first_max(cur, sub_grp, N_GROUPS)
        grp_on = jnp.where(hit, 1.0, grp_on)
        cur = jnp.where(hit, NEG_INF, cur)
    exp_on = jnp.concatenate([jnp.broadcast_to(grp_on[g:g + 1], (PER_GROUP, t)) for g in range(N_GROUPS)], axis=0)
    cur = jnp.where(exp_on > 0.0, sel, NEG_INF)
    hits, idx_rows = [], []
    for _ in range(TOP_K):
        _, hit = _first_max(cur, sub_e, N_EXPERTS)
        hits.append(hit)
        idx_rows.append(jnp.sum(jnp.where(hit, sub_e, 0), axis=0, keepdims=True))
        cur = jnp.where(hit, NEG_INF, cur)
    w_rows = [jnp.sum(jnp.where(hit, scores, 0.0), axis=0, keepdims=True) for hit in hits]
    total = functools.reduce(lambda a, b: a + b, w_rows)
    w_ref[...] = jnp.concatenate([w / total * ROUTED_SCALE for w in w_rows], axis=0)
    idx_ref[...] = jnp.concatenate(idx_rows, axis=0)
    chosen = functools.reduce(lambda a, b: a + b, [jnp.where(hit, 1.0, 0.0) for hit in hits])
    incl = jnp.dot(chosen.astype(BF16), tri_ref[...], preferred_element_type=F32)
    before = incl - chosen + carry_ref[:, :1]
    rank_ref[...] = jnp.concatenate(
        [jnp.sum(jnp.where(hit, before, 0.0), axis=0, keepdims=True) for hit in hits], axis=0).astype(jnp.int32)
    carry_ref[...] = carry_ref[...] + incl[:, t - 1:t]
    cnt_ref[...] = carry_ref[...]


def _route(logits, router_bias):
    n = logits.shape[0]
    t = ROUTE_T
    tri = (jnp.arange(t)[:, None] <= jnp.arange(t)[None, :]).astype(BF16)
    bias_b = jnp.broadcast_to(router_bias.astype(F32)[:, None], (N_EXPERTS, t))
    slot = pl.BlockSpec((TOP_K, t), lambda i: (0, i))
    idx, w, rank, cnt = pl.pallas_call(
        _route_kernel,
        grid=(n // t,),
        in_specs=[pl.BlockSpec((t, LANES), lambda i: (i, 0)),
                  pl.BlockSpec((N_EXPERTS, t), lambda i: (0, 0)),
                  pl.BlockSpec((t, t), lambda i: (0, 0))],
        out_specs=[slot, slot, slot, pl.BlockSpec((N_EXPERTS, LANES), lambda i: (0, 0))],
        out_shape=[jax.ShapeDtypeStruct((TOP_K, n), jnp.int32), jax.ShapeDtypeStruct((TOP_K, n), F32),
                   jax.ShapeDtypeStruct((TOP_K, n), jnp.int32), jax.ShapeDtypeStruct((N_EXPERTS, LANES), F32)],
        scratch_shapes=[pltpu.VMEM((N_EXPERTS, LANES), F32)],
        compiler_params=_cparams("arbitrary"),
    )(logits, bias_b, tri)
    return idx, w, rank, cnt[:, 0].astype(jnp.int32)


def _experts_kernel(be_ref, x_ref, w1_ref, w3_ref, w2_ref, o_ref, w1b, w3b, w2b):
    i = pl.program_id(0)
    changed = jnp.logical_or(i == 0, be_ref[i] != be_ref[jnp.maximum(i - 1, 0)])

    @pl.when(changed)
    def _():
        w1b[...] = w1_ref[0].astype(BF16)
        w3b[...] = w3_ref[0].astype(BF16)
        w2b[...] = w2_ref[0].astype(BF16)

    xb = x_ref[...]
    hid = _silu(jnp.dot(xb, w1b[...], preferred_element_type=F32)) * \
        jnp.dot(xb, w3b[...], preferred_element_type=F32)
    o_ref[...] = jnp.dot(hid.astype(BF16), w2b[...], preferred_element_type=F32)


def _grouped_experts(xs, block_expert, w1, w3, w2):
    rows = xs.shape[0]
    n_blocks = rows // MOE_BLOCK
    grid_spec = pltpu.PrefetchScalarGridSpec(
        num_scalar_prefetch=1,
        grid=(n_blocks,),
        in_specs=[pl.BlockSpec((MOE_BLOCK, D_MODEL), lambda i, be: (i, 0)),
                  pl.BlockSpec((1, D_MODEL, D_EXPERT), lambda i, be: (be[i], 0, 0)),
                  pl.BlockSpec((1, D_MODEL, D_EXPERT), lambda i, be: (be[i], 0, 0)),
                  pl.BlockSpec((1, D_EXPERT, D_MODEL), lambda i, be: (be[i], 0, 0))],
        out_specs=pl.BlockSpec((MOE_BLOCK, D_MODEL), lambda i, be: (i, 0)),
        scratch_shapes=[pltpu.VMEM((D_MODEL, D_EXPERT), BF16), pltpu.VMEM((D_MODEL, D_EXPERT), BF16),
                        pltpu.VMEM((D_EXPERT, D_MODEL), BF16)])
    return pl.pallas_call(
        _experts_kernel,
        grid_spec=grid_spec,
        out_shape=jax.ShapeDtypeStruct((rows, D_MODEL), F32),
        compiler_params=_cparams("arbitrary"),
    )(block_expert, xs, w1, w3, w2)


def _ffn_post_kernel(x_ref, r_ref, s_ref, gate_ref, g_ref, b_ref, o_ref):
    r = DEEPNORM_ALPHA * x_ref[...] + gate_ref[0] * (r_ref[...] + s_ref[...])
    o_ref[...] = _layer_norm_rows(r, g_ref[...], b_ref[...])


def _ffn_post(x, routed, shared, mod3, ln_g, ln_b):
    n = x.shape[0]
    tm = FFN_TM
    row = lambda i: _mod_row_of_tile(i, tm)
    tile = pl.BlockSpec((tm, D_MODEL), lambda i: (i, 0))
    vec = pl.BlockSpec((1, D_MODEL), lambda i: (0, 0))
    return pl.pallas_call(
        _ffn_post_kernel,
        grid=(n // tm,),
        in_specs=[tile, tile, tile, pl.BlockSpec((1, 1, D_MODEL), lambda i: (row(i), 0, 5)), vec, vec],
        out_specs=tile,
        out_shape=jax.ShapeDtypeStruct((n, D_MODEL), F32),
        compiler_params=_cparams("parallel"),
    )(x, routed, shared, mod3, ln_g.reshape(1, D_MODEL), ln_b.reshape(1, D_MODEL))


def _moe_routed(hb, idx, w, rank, counts, w1, w3, w2):
    n = hb.shape[0]
    padded = (counts + MOE_BLOCK - 1) // MOE_BLOCK * MOE_BLOCK
    pad_end = jnp.cumsum(padded)
    pad_start = pad_end - padded
    dest = pad_start[idx] + rank
    n_blocks = n * TOP_K // MOE_BLOCK + N_EXPERTS
    rows = n_blocks * MOE_BLOCK
    row_tok = (jnp.arange(rows, dtype=jnp.int32) % n).at[dest.reshape(-1)].set(
        jnp.tile(jnp.arange(n, dtype=jnp.int32), TOP_K))
    block_expert = jnp.minimum(
        jnp.searchsorted(pad_end, jnp.arange(n_blocks) * MOE_BLOCK, side='right'), N_EXPERTS - 1).astype(jnp.int32)
    xs = hb[row_tok]
    ys = _grouped_experts(xs, block_expert, w1, w3, w2)
    return jnp.einsum('knd,kn->nd', ys[dest], w)


def _rms_heads(x, g):
    y = x * lax.rsqrt(jnp.mean(x * x, axis=-1, keepdims=True) + EPS)
    return y * g


def _rope_tables():
    rows = DEC_SEQ // GRID_W
    row = jnp.repeat(jnp.arange(rows), GRID_W).astype(F32)
    colp = jnp.tile(jnp.arange(GRID_W), rows).astype(F32)
    nf = HD_A // 4
    inv_freq = ROPE_THETA ** (-jnp.arange(nf, dtype=F32) / nf)
    ar = row[:, None] * inv_freq[None, :]
    ac = colp[:, None] * inv_freq[None, :]
    cos = jnp.concatenate([jnp.cos(ar), jnp.cos(ar), jnp.cos(ac), jnp.cos(ac)], axis=-1)
    sin = jnp.concatenate([-jnp.sin(ar), jnp.sin(ar), -jnp.sin(ac), jnp.sin(ac)], axis=-1)
    return cos, sin


def _rope(x, cos, sin):
    a, b, c, d = jnp.split(x, 4, axis=-1)
    sw = jnp.concatenate([b, a, d, c], axis=-1)
    return x * cos[None, :, None, :] + sw * sin[None, :, None, :]


def _expand_q(q):
    n = q.shape[0]
    z = jnp.zeros_like(q)
    g0 = jnp.concatenate([q, z], axis=-1)
    g1 = jnp.concatenate([z, q], axis=-1)
    sel = (jnp.arange(N_HEADS_A) // GROUP_A)[None, :, None]
    return jnp.where(sel == 0, g0, g1).reshape(n, N_HEADS_A * LANES)


_ATT_HEAD_ORDER = [g * GROUP_A + j for j in range(GROUP_A) for g in range(N_KV_A)]


def _even_layer(x, mod3, j, w_in, w_out, q_norm, k_norm, log_decay, gn_g,
                cache_k, cache_v, state_ret):
    (p,) = _modulated_proj(x, mod3, 0, w_in.astype(BF16), [EVEN_IN], [F32])
    q = _rms_heads(p[:, :A_Q].reshape(N_TOK, N_HEADS_A, HD_A), q_norm)
    k = _rms_heads(p[:, A_Q:A_Q + A_KV].reshape(N_TOK, N_KV_A, HD_A), k_norm)
    v = p[:, A_Q + A_KV:A_Q + 2 * A_KV].reshape(N_TOK, N_KV_A, HD_A)
    k_p = k[:N_P].reshape(BATCH, SEQ, N_KV_A, HD_A)
    v_p = v[:N_P].reshape(BATCH, SEQ, N_KV_A, HD_A)
    scale = HD_A ** -0.5
    qp = _expand_q(q[:N_P] * scale).astype(BF16)
    ktp = jnp.swapaxes(k_p.reshape(BATCH, SEQ, A_KV), 1, 2).astype(BF16)
    o_p = _attention(qp, ktp, v_p.reshape(BATCH, SEQ, A_KV).astype(BF16), BATCH, SEQ)
    cos, sin = _rope_tables()
    qs = _rope(q[N_P:].reshape(DEC_BATCH, DEC_SEQ, N_HEADS_A, HD_A), cos, sin)
    ks = _rope(k[N_P:].reshape(DEC_BATCH, DEC_SEQ, N_KV_A, HD_A), cos, sin)
    k_all = jnp.concatenate([ks, cache_k], axis=1).reshape(DEC_BATCH, DEC_SEQ + PAST_LEN, A_KV)
    v_all = jnp.concatenate([v[N_P:].reshape(DEC_BATCH, DEC_SEQ, N_KV_A, HD_A), cache_v], axis=1)
    v_all = v_all.reshape(DEC_BATCH, DEC_SEQ + PAST_LEN, A_KV)
    qs = _expand_q(qs.reshape(N_S, N_HEADS_A, HD_A) * scale).astype(BF16)
    o_s = _attention(qs, jnp.swapaxes(k_all, 1, 2).astype(BF16), v_all.astype(BF16), DEC_BATCH, DEC_SEQ)
    o_attn = jnp.concatenate([o_p, o_s], axis=0)
    lg_rows = jnp.broadcast_to(log_decay.reshape(2 * H_B, 1), (2 * H_B, LANES))
    zeros_s = jnp.zeros((BATCH, 2, H_B, DK_B, DV_B), F32)
    r_p, s_p = _retention(p, 0, BATCH, SEQ, lg_rows, zeros_s, gn_g)
    r_s, _ = _retention(p, N_P // DEC_SEQ, DEC_BATCH, DEC_SEQ, lg_rows, state_ret, gn_g)
    o_ret = jnp.concatenate([r_p, r_s], axis=0)
    perm = np.concatenate([np.arange(h * HD_A, (h + 1) * HD_A) for h in _ATT_HEAD_ORDER])
    w_o = jnp.concatenate([w_out[:A_Q][perm], w_out[A_Q:]], axis=0).astype(BF16)
    return [o_attn, o_ret], w_o, k_p, v_p, s_p


def _conv_silu(x, w, batch, t):
    xb = x.reshape(batch, t, x.shape[-1])
    xp = jnp.pad(xb, ((0, 0), ((CONV_K - 1) // 2, CONV_K // 2), (0, 0)))
    y = sum(xp[:, i:i + t] * w[i][None, None, :] for i in range(CONV_K))
    return jax.nn.silu(y).reshape(batch * t, x.shape[-1])


def _l2_heads(x):
    xh = x.reshape(x.shape[0], H_C, DK_C)
    return (xh * lax.rsqrt(jnp.sum(xh * xh, axis=-1, keepdims=True) + EPS)).reshape(x.shape)


def _chunk_cumsum(g, batch, t):
    gc = g.reshape(batch, t // GC, GC, 2, H_C)
    f = jnp.cumsum(gc[:, :, :, 0], axis=2)
    b = jnp.cumsum(gc[:, :, ::-1, 1], axis=2)[:, :, ::-1]
    return jnp.stack([f, b], axis=3)


def _odd_layer(x, mod3, j, w_in, conv_w, a_log, dt_bias, norm_g, w_out, state_gdn):
    w_main = w_in[:, :2 * C_QK + 2 * C_V].astype(BF16)
    w_ab = jnp.pad(w_in[:, 2 * C_QK + 2 * C_V:], ((0, 0), (0, LANES - 4 * H_C))).astype(BF16)
    w_cat = jnp.concatenate([w_main, w_ab], axis=1)
    pm, pab = _modulated_proj(x, mod3, 0, w_cat, [2 * C_QK + 2 * C_V, LANES], [F32, F32])
    qkv = pm[:, :2 * C_QK + C_V]
    z = pm[:, 2 * C_QK + C_V:]
    qkv = jnp.concatenate([_conv_silu(qkv[:N_P], conv_w, BATCH, SEQ),
                           _conv_silu(qkv[N_P:], conv_w, DEC_BATCH, DEC_SEQ)], axis=0)
    q = _l2_heads(qkv[:, :C_QK])
    k = _l2_heads(qkv[:, C_QK:2 * C_QK])
    v = qkv[:, 2 * C_QK:]
    ab = pab[:, :4 * H_C].reshape(N_TOK, 2, 2, H_C)
    beta = jax.nn.sigmoid(ab[:, 0])
    g = -jnp.exp(a_log.astype(F32)) * jax.nn.softplus(ab[:, 1] + dt_bias.astype(F32))

    gcs_p = _chunk_cumsum(g[:N_P], BATCH, SEQ)
    gcs_s = _chunk_cumsum(g[N_P:], DEC_BATCH, DEC_SEQ)
    gcol = jnp.concatenate([gcs_p.reshape(N_P, 2 * H_C), gcs_s.reshape(N_S, 2 * H_C)], axis=0)
    gbeta = jnp.concatenate([beta.reshape(N_TOK, 2 * H_C), gcol], axis=1)
    zeros_s = jnp.zeros((BATCH, 2, H_C, DK_C, DV_C), F32)
    o_p, s_p = _gdn(q, k, v, z, gbeta, gcs_p.transpose(0, 3, 4, 1, 2), 0, BATCH, SEQ, zeros_s, norm_g)
    o_s, _ = _gdn(q, k, v, z, gbeta, gcs_s.transpose(0, 3, 4, 1, 2), N_P // DEC_SEQ, DEC_BATCH, DEC_SEQ,
                  state_gdn, norm_g)
    return [jnp.concatenate([o_p, o_s], axis=0)], w_out.astype(BF16), s_p


def kernel(x_prompt, x_sample, cache_attn_k, cache_attn_v, state_ret, state_gdn, c, c_ctx, mod_w, mod_b, ln_g, ln_b, even_w_in, even_w_out, attn_q_norm, attn_k_norm, ret_log_decay, ret_norm_g, odd_w_in, gdn_conv_w, gdn_a_log, gdn_dt_bias, gdn_norm_g, odd_w_out, router_w, router_bias, expert_w1, expert_w3, expert_w2, shared_w1, shared_w3, shared_w2):
    x = jnp.concatenate([x_prompt.reshape(N_P, D_MODEL), x_sample.reshape(N_S, D_MODEL)], axis=0)
    cvec = jnp.concatenate([c_ctx[None, :], c, jnp.zeros((MOD_ROWS - N_MOD, D_MODEL), F32)], axis=0)
    mod_all = _mod_vectors(cvec, mod_w, mod_b)
    new_k, new_v, new_ret, new_gdn = [], [], [], []
    for l in range(DEPTH):
        j = l // 2
        mod3 = mod_all[l].reshape(MOD_ROWS, 1, 6 * D_MODEL)
        if l % 2 == 0:
            a_list, w_o, k_p, v_p, s_p = _even_layer(
                x, mod3, j, even_w_in[j], even_w_out[j], attn_q_norm[j], attn_k_norm[j],
                ret_log_decay[j], ret_norm_g[j], cache_attn_k[:, j], cache_attn_v[:, j], state_ret[:, j])
            new_k.append(k_p)
            new_v.append(v_p)
            new_ret.append(s_p)
        else:
            a_list, w_o, s_p = _odd_layer(
                x, mod3, j, odd_w_in[j], gdn_conv_w[j], gdn_a_log[j], gdn_dt_bias[j], gdn_norm_g[j],
                odd_w_out[j], state_gdn[:, j])
            new_gdn.append(s_p)
        x = _outproj_ln(a_list, w_o, x, mod3, 2, ln_g[l, 0], ln_b[l, 0])
        rw = jnp.pad(router_w[l], ((0, 0), (0, LANES - N_EXPERTS)))
        ws13 = jnp.concatenate([shared_w1[l], shared_w3[l]], axis=1).astype(BF16)
        hb, logits, shared = _ffn_pre(x, mod3, rw, ws13, shared_w2[l].astype(BF16))
        idx, w, rank, counts = _route(logits, router_bias[l])
        routed = _moe_routed(hb, idx, w, rank, counts, expert_w1[l], expert_w3[l], expert_w2[l])
        x = _ffn_post(x, routed, shared, mod3, ln_g[l, 1], ln_b[l, 1])
    return (x[:N_P].reshape(BATCH, SEQ, D_MODEL), x[N_P:].reshape(DEC_BATCH, DEC_SEQ, D_MODEL),
            jnp.stack(new_k, axis=1), jnp.stack(new_v, axis=1),
            jnp.stack(new_ret, axis=1), jnp.stack(new_gdn, axis=1))
```

```python
import functools
import math

import jax
import jax.numpy as jnp
import numpy as np
from jax import lax
from jax.experimental import pallas as pl
from jax.experimental.pallas import tpu as pltpu
from jax.experimental.pallas import tpu_sc as plsc

D_MODEL = 1024
BATCH = 16
SEQ = 256
DEPTH = 4
DEC_BATCH = 4
DEC_SEQ = 4096
PAST_LEN = 512
GRID_W = 64
N_HEADS_A = 8
N_KV_A = 2
HD_A = 64
ROPE_THETA = 10000.0
H_B = 4
DK_B = 64
DV_B = 128
RET_CHUNK = 128
H_C = 8
DK_C = 128
DV_C = 128
CONV_K = 5
GDN_CHUNK = 64
N_EXPERTS = 64
TOP_K = 8
N_GROUPS = 8
TOPK_GROUPS = 4
D_EXPERT = 256
D_SHARED = 256
ROUTED_SCALE = 2.5
MOE_BLOCK = 256
A_Q = N_HEADS_A * HD_A
A_KV = N_KV_A * HD_A
B_QK = H_B * DK_B
B_V = H_B * DV_B
EVEN_IN = A_Q + 2 * A_KV + 2 * B_QK + 2 * B_V
C_QK = H_C * DK_C
C_V = H_C * DV_C
DEEPNORM_ALPHA = (2 * DEPTH) ** 0.25
EPS = 1e-6

N_P = BATCH * SEQ
N_S = DEC_BATCH * DEC_SEQ
N_TOK = N_P + N_S
N_MOD = 1 + DEC_BATCH
MOD_ROWS = 8

LANES = 128
VMEM_LIMIT = 56 * 1024 * 1024

F32 = jnp.float32
BF16 = jnp.bfloat16


def _cparams(*sem):
    return pltpu.CompilerParams(dimension_semantics=sem, vmem_limit_bytes=VMEM_LIMIT)


def _bdot(a, b):
    return jnp.dot(a.astype(BF16), b.astype(BF16), preferred_element_type=F32)


def _bdot_t(a, b):
    return lax.dot_general(a.astype(BF16), b.astype(BF16), (((0,), (0,)), ((), ())),
                           preferred_element_type=F32)


def _bdot_nt(a, b):
    return lax.dot_general(a.astype(BF16), b.astype(BF16), (((1,), (1,)), ((), ())),
                           preferred_element_type=F32)


def _split3(a):
    hi = a.astype(BF16)
    r = a - hi.astype(F32)
    mid = r.astype(BF16)
    lo = (r - mid.astype(F32)).astype(BF16)
    return hi, mid, lo


def _dot_hp(a, b):
    a0, a1, a2 = _split3(a)
    b0, b1, b2 = _split3(b)
    d = lambda x, y: jnp.dot(x, y, preferred_element_type=F32)
    small = d(a0, b2) + d(a2, b0) + d(a1, b1)
    return (d(a0, b1) + d(a1, b0)) + small + d(a0, b0)


def _silu(x):
    return x * (1.0 / (1.0 + jnp.exp(-x)))


def _mod_row_of_tile(i, tile):
    tiles_p = N_P // tile
    tiles_per_b = DEC_SEQ // tile
    return jnp.where(i < tiles_p, 0, 1 + (i - tiles_p) // tiles_per_b)


MOD_TN = 1536


def _mod_kernel(c_ref, w_ref, b_ref, o_ref):
    a = _silu(c_ref[...])
    o_ref[0] = _bdot(a, w_ref[0]) + b_ref[0]


def _mod_vectors(cvec, mod_w, mod_b):
    n6 = 6 * D_MODEL
    return pl.pallas_call(
        _mod_kernel,
        grid=(DEPTH, n6 // MOD_TN),
        in_specs=[pl.BlockSpec((MOD_ROWS, D_MODEL), lambda l, j: (0, 0)),
                  pl.BlockSpec((1, D_MODEL, MOD_TN), lambda l, j: (l, 0, j)),
                  pl.BlockSpec((1, 1, MOD_TN), lambda l, j: (l, 0, j))],
        out_specs=pl.BlockSpec((1, MOD_ROWS, MOD_TN), lambda l, j: (l, 0, j)),
        out_shape=jax.ShapeDtypeStruct((DEPTH, MOD_ROWS, n6), F32),
        compiler_params=_cparams("parallel", "parallel"),
    )(cvec, mod_w, mod_b.reshape(DEPTH, 1, n6))


PROJ_TM = 256


def _proj_kernel(x_ref, shift_ref, scale_ref, w_ref, *o_refs, widths):
    h = (x_ref[...] * (1.0 + scale_ref[0]) + shift_ref[0]).astype(BF16)
    off = 0
    for o_ref, wd in zip(o_refs, widths):
        o_ref[...] = jnp.dot(h, w_ref[:, off:off + wd], preferred_element_type=F32).astype(o_ref.dtype)
        off += wd


def _modulated_proj(x, mod3, shift_blk, w_bf16, widths, dtypes):
    n = x.shape[0]
    tm = PROJ_TM
    row = lambda i: _mod_row_of_tile(i, tm)
    return pl.pallas_call(
        functools.partial(_proj_kernel, widths=tuple(widths)),
        grid=(n // tm,),
        in_specs=[pl.BlockSpec((tm, D_MODEL), lambda i: (i, 0)),
                  pl.BlockSpec((1, 1, D_MODEL), lambda i: (row(i), 0, shift_blk)),
                  pl.BlockSpec((1, 1, D_MODEL), lambda i: (row(i), 0, shift_blk + 1)),
                  pl.BlockSpec((D_MODEL, sum(widths)), lambda i: (0, 0))],
        out_specs=[pl.BlockSpec((tm, wd), lambda i: (i, 0)) for wd in widths],
        out_shape=[jax.ShapeDtypeStruct((n, wd), dt) for wd, dt in zip(widths, dtypes)],
        compiler_params=_cparams("parallel"),
    )(x, mod3, mod3, w_bf16)


OUT_TM = 512


def _layer_norm_rows(r, g, b):
    mu = jnp.mean(r, axis=-1, keepdims=True)
    d = r - mu
    var = jnp.mean(d * d, axis=-1, keepdims=True)
    return d * lax.rsqrt(var + EPS) * g + b


def _outproj_kernel(*refs, n_a):
    a_refs = refs[:n_a]
    w_ref, x_ref, gate_ref, g_ref, b_ref, o_ref = refs[n_a:]
    off = 0
    acc = None
    for a_ref in a_refs:
        wd = a_ref.shape[1]
        part = jnp.dot(a_ref[...].astype(BF16), w_ref[off:off + wd, :], preferred_element_type=F32)
        acc = part if acc is None else acc + part
        off += wd
    r = DEEPNORM_ALPHA * x_ref[...] + gate_ref[0] * acc
    o_ref[...] = _layer_norm_rows(r, g_ref[...], b_ref[...])


def _outproj_ln(a_list, w_bf16, x, mod3, gate_blk, ln_g, ln_b):
    n = x.shape[0]
    tm = OUT_TM
    row = lambda i: _mod_row_of_tile(i, tm)
    kdim = w_bf16.shape[0]
    return pl.pallas_call(
        functools.partial(_outproj_kernel, n_a=len(a_list)),
        grid=(n // tm,),
        in_specs=[pl.BlockSpec((tm, a.shape[1]), lambda i: (i, 0)) for a in a_list] + [
            pl.BlockSpec((kdim, D_MODEL), lambda i: (0, 0)),
            pl.BlockSpec((tm, D_MODEL), lambda i: (i, 0)),
            pl.BlockSpec((1, 1, D_MODEL), lambda i: (row(i), 0, gate_blk)),
            pl.BlockSpec((1, D_MODEL), lambda i: (0, 0)),
            pl.BlockSpec((1, D_MODEL), lambda i: (0, 0))],
        out_specs=pl.BlockSpec((tm, D_MODEL), lambda i: (i, 0)),
        out_shape=jax.ShapeDtypeStruct((n, D_MODEL), F32),
        compiler_params=_cparams("parallel"),
    )(*a_list, w_bf16, x, mod3, ln_g.reshape(1, D_MODEL), ln_b.reshape(1, D_MODEL))


ATT_TQ = 256
GROUP_A = N_HEADS_A // N_KV_A


def _attn_kernel(q_ref, kt_ref, v_ref, o_ref):
    kt = kt_ref[0]
    v = v_ref[0]
    lane = lax.broadcasted_iota(jnp.int32, (1, LANES), 1)
    for j in range(GROUP_A):
        outs = []
        for g in range(N_KV_A):
            h = g * GROUP_A + j
            s = jnp.dot(q_ref[:, h * LANES:(h + 1) * LANES], kt, preferred_element_type=F32)
            m = jnp.max(s, axis=-1, keepdims=True)
            p = jnp.exp(s - m)
            l = jnp.sum(p, axis=-1, keepdims=True)
            pv = jnp.dot(p.astype(BF16), v, preferred_element_type=F32)
            outs.append(pv * (1.0 / l))
        o_ref[:, j * LANES:(j + 1) * LANES] = jnp.where(lane < HD_A, outs[0], outs[1]).astype(o_ref.dtype)


def _attention(q_exp, kt, v, batch, t):
    tk = v.shape[1]
    tq = min(ATT_TQ, t)
    nq = t // tq
    return pl.pallas_call(
        _attn_kernel,
        grid=(batch, nq),
        in_specs=[pl.BlockSpec((tq, N_HEADS_A * LANES), lambda b, i: (b * nq + i, 0)),
                  pl.BlockSpec((1, LANES, tk), lambda b, i: (b, 0, 0)),
                  pl.BlockSpec((1, tk, LANES), lambda b, i: (b, 0, 0))],
        out_specs=pl.BlockSpec((tq, A_Q), lambda b, i: (b * nq + i, 0)),
        out_shape=jax.ShapeDtypeStruct((batch * t, A_Q), BF16),
        compiler_params=_cparams("parallel", "parallel"),
    )(q_exp, kt, v)


RC = RET_CHUNK


def _ret_kernel(q_ref, k_ref, v_ref, g_ref, lg_ref, s0_ref, gn_ref, o_ref, s_out_ref,
                ob_ref, st_ref, *, t):
    nc = t // RC
    hp = pl.program_id(1)
    ii = lax.broadcasted_iota(jnp.int32, (RC, RC), 0).astype(F32)
    jj = lax.broadcasted_iota(jnp.int32, (RC, RC), 1).astype(F32)
    col_i = lax.broadcasted_iota(jnp.int32, (RC, 1), 0).astype(F32)
    lane = lax.broadcasted_iota(jnp.int32, (1, LANES), 1)
    masks = [(lane >= hh * DK_B) & (lane < (hh + 1) * DK_B) for hh in range(2)]

    consts = []
    for d in range(2):
        for hh in range(2):
            lg = lg_ref[pl.ds(d * H_B + hp * 2 + hh, 1), :][:, :1]
            if d == 0:
                diff = ii - jj
                qe, ke = col_i + 1.0, (RC - 1.0) - col_i
            else:
                diff = jj - ii
                qe, ke = RC - col_i, col_i
            intra = jnp.where(diff >= 0, jnp.exp(jnp.maximum(diff, 0.0) * lg), 0.0)
            consts.append((intra, jnp.exp(qe * lg), jnp.exp(ke * lg), jnp.exp(RC * lg)))
            s0 = s0_ref[0, d, hh]
            z = jnp.zeros((DK_B, DV_B), F32)
            st_ref[d * 2 + hh] = jnp.concatenate([s0, z] if hh == 0 else [z, s0], axis=0)

    def chunk(c, d):
        r0 = pl.multiple_of(c * RC, RC)
        qc = q_ref[pl.ds(r0, RC), :]
        kc = k_ref[pl.ds(r0, RC), :] * (DK_B ** -0.5)
        outs = []
        for hh in range(2):
            intra, q_dec, k_dec, c_dec = consts[d * 2 + hh]
            vc = v_ref[pl.ds(r0, RC), hh * DV_B:(hh + 1) * DV_B]
            qh = jnp.where(masks[hh], qc, 0.0)
            kh = jnp.where(masks[hh], kc, 0.0)
            s = st_ref[d * 2 + hh]
            scores = _bdot_nt(qh, kh) * intra
            o = _bdot(scores, vc) + _bdot(qh * q_dec, s)
            st_ref[d * 2 + hh] = s * c_dec + _bdot_t(kh * k_dec, vc)
            outs.append(o)
        return r0, jnp.concatenate(outs, axis=1)

    def body(c, carry):
        r0, of = chunk(c, 0)
        o_ref[pl.ds(r0, RC), :] = of
        r1, ob = chunk(nc - 1 - c, 1)
        ob_ref[pl.ds(r1, RC), :] = ob
        return carry

    lax.fori_loop(0, nc, body, 0)

    for d in range(2):
        for hh in range(2):
            s_out_ref[0, d, hh] = st_ref[d * 2 + hh][hh * DK_B:(hh + 1) * DK_B, :]

    def finish(c, carry):
        r0 = pl.multiple_of(c * RC, RC)
        o = o_ref[pl.ds(r0, RC), :] + ob_ref[pl.ds(r0, RC), :]
        gate = _silu(g_ref[pl.ds(r0, RC), :])
        ys = []
        for hh in range(2):
            oh = o[:, hh * DV_B:(hh + 1) * DV_B]
            mu = jnp.mean(oh, axis=-1, keepdims=True)
            dlt = oh - mu
            var = jnp.mean(dlt * dlt, axis=-1, keepdims=True)
            ys.append(dlt * lax.rsqrt(var + EPS))
        o_ref[pl.ds(r0, RC), :] = jnp.concatenate(ys, axis=1) * gn_ref[...] * gate
        return carry

    lax.fori_loop(0, nc, finish, 0)


def _retention(p, row_blk0, batch, t, lg_rows, s0, gn_g):
    qb, kb, vb, gb = (A_Q + 2 * A_KV) // LANES, (A_Q + 2 * A_KV + B_QK) // LANES, \
        (A_Q + 2 * A_KV + 2 * B_QK) // (2 * DV_B), (A_Q + 2 * A_KV + 2 * B_QK + B_V) // (2 * DV_B)
    return pl.pallas_call(
        functools.partial(_ret_kernel, t=t),
        grid=(batch, H_B // 2),
        in_specs=[pl.BlockSpec((t, LANES), lambda b, h: (row_blk0 + b, qb + h)),
                  pl.BlockSpec((t, LANES), lambda b, h: (row_blk0 + b, kb + h)),
                  pl.BlockSpec((t, 2 * DV_B), lambda b, h: (row_blk0 + b, vb + h)),
                  pl.BlockSpec((t, 2 * DV_B), lambda b, h: (row_blk0 + b, gb + h)),
                  pl.BlockSpec((2 * H_B, LANES), lambda b, h: (0, 0)),
                  pl.BlockSpec((1, 2, 2, DK_B, DV_B), lambda b, h: (b, 0, h, 0, 0)),
                  pl.BlockSpec((1, 2 * DV_B), lambda b, h: (0, h))],
        out_specs=[pl.BlockSpec((t, 2 * DV_B), lambda b, h: (b, h)),
                   pl.BlockSpec((1, 2, 2, DK_B, DV_B), lambda b, h: (b, 0, h, 0, 0))],
        out_shape=[jax.ShapeDtypeStruct((batch * t, B_V), F32),
                   jax.ShapeDtypeStruct((batch, 2, H_B, DK_B, DV_B), F32)],
        scratch_shapes=[pltpu.VMEM((t, 2 * DV_B), F32), pltpu.VMEM((4, LANES, DV_B), F32)],
        compiler_params=_cparams("parallel", "parallel"),
    )(p, p, p, p, lg_rows, s0, gn_g.reshape(1, B_V))


GC = 256


def _unit_tri_inverse(a):
    ii = lax.broadcasted_iota(jnp.int32, (GC, GC), 0)
    jj = lax.broadcasted_iota(jnp.int32, (GC, GC), 1)

    block_dist = ii ^ jj

    eye = (ii == jj).astype(F32)
    m = jnp.where((block_dist >> 3) == 0, -a, 0.0)
    inv = eye + m
    m2 = _bdot(m, m)
    inv = inv + _bdot(inv, m2)
    m4 = _bdot(m2, m2)
    inv = inv + _bdot(inv, m4)
    for shift in range(3, int(math.log2(GC))):
        l = jnp.where((block_dist >> shift) == 1, a, 0.0)
        inv = inv - _bdot(_bdot(inv, l), inv)
    return inv


def _gdn_kernel(q_ref, k_ref, v_ref, z_ref, gb_ref, gr_ref, s0_ref, ng_ref, o_ref, s_out_ref,
                ob_ref, st_ref, wq_s, u_s, a_s, kg_s, *, t):
    nc = t // GC
    h = pl.program_id(1)
    ii = lax.broadcasted_iota(jnp.int32, (GC, GC), 0)
    jj = lax.broadcasted_iota(jnp.int32, (GC, GC), 1)
    incl = [ii >= jj, jj >= ii]
    strict = [ii > jj, jj > ii]
    lane32 = lax.broadcasted_iota(jnp.int32, (1, 4 * H_C), 1)
    for d in range(2):
        st_ref[d] = s0_ref[0, d, 0]

    def col(x, idx):
        return jnp.sum(jnp.where(lane32 == idx, x, 0.0), axis=1, keepdims=True)

    def gate_last(c, d):
        grow = gr_ref[0, d, 0, pl.ds(c, 1), :]
        return grow, (grow[:, GC - 1:GC] if d == 0 else grow[:, 0:1])

    def prep(c, carry):
        rows = pl.ds(pl.multiple_of(c * GC, GC), GC)
        qc = q_ref[rows, :] * (DK_C ** -0.5)
        kc = k_ref[rows, :]
        vc = v_ref[rows, :]
        gb = gb_ref[rows, :]
        kk = _bdot_nt(kc, kc)
        qk = _bdot_nt(qc, kc)
        for d in range(2):
            beta = col(gb, d * H_C + h)
            gcol = col(gb, 2 * H_C + d * H_C + h)
            grow, glast = gate_last(c, d)
            decay = jnp.exp(jnp.where(incl[d], gcol - grow, -jnp.inf))
            eg = jnp.exp(gcol)
            kb = kc * beta
            a = jnp.where(strict[d], kk * beta * decay, 0.0)
            tinv = _unit_tri_inverse(a)
            wu = _bdot(tinv, jnp.concatenate([kb * eg, vc * beta], axis=1))
            r2 = pl.multiple_of(c * 2 * GC, 2 * GC)
            wq_s[d, pl.ds(r2, GC), :] = wu[:, :DK_C].astype(BF16)
            wq_s[d, pl.ds(r2 + GC, GC), :] = (qc * eg).astype(BF16)
            u_s[d, rows, :] = wu[:, DK_C:]
            a_s[d, rows, :] = (qk * decay).astype(BF16)
            kg_s[d, rows, :] = (kc * jnp.exp(glast - gcol)).astype(BF16)
        return carry

    lax.fori_loop(0, nc, prep, 0)

    def step(c, carry):
        for d in range(2):
            cc = c if d == 0 else nc - 1 - c
            rows = pl.ds(pl.multiple_of(cc * GC, GC), GC)
            _, glast = gate_last(cc, d)
            s = st_ref[d]
            sb = s.astype(BF16)
            wq = wq_s[d, pl.ds(pl.multiple_of(cc * 2 * GC, 2 * GC), 2 * GC), :]
            ws = jnp.dot(wq, sb, preferred_element_type=F32)
            v_new = u_s[d, rows, :] - ws[:GC]
            vnb = v_new.astype(BF16)
            o = ws[GC:] + jnp.dot(a_s[d, rows, :], vnb, preferred_element_type=F32)
            st_ref[d] = s * jnp.exp(glast) + _bdot_t(kg_s[d, rows, :], vnb)
            if d == 0:
                o_ref[rows, :] = o
            else:
                ob_ref[rows, :] = o
        return carry

    lax.fori_loop(0, nc, step, 0)

    for d in range(2):
        s_out_ref[0, d, 0] = st_ref[d]

    def finish(c, carry):
        r0 = pl.multiple_of(c * GC, GC)
        o = o_ref[pl.ds(r0, GC), :] + ob_ref[pl.ds(r0, GC), :]
        y = o * lax.rsqrt(jnp.mean(o * o, axis=-1, keepdims=True) + EPS) * ng_ref[...]
        o_ref[pl.ds(r0, GC), :] = y * _silu(z_ref[pl.ds(r0, GC), :])
        return carry

    lax.fori_loop(0, nc, finish, 0)


def _gdn(q, k, v, z, gbeta, grow, row_blk0, batch, t, s0, norm_g):
    nc = t // GC
    tok = lambda b, h: (row_blk0 + b, h)
    return pl.pallas_call(
        functools.partial(_gdn_kernel, t=t),
        grid=(batch, H_C),
        in_specs=[pl.BlockSpec((t, DK_C), tok), pl.BlockSpec((t, DK_C), tok),
                  pl.BlockSpec((t, DV_C), tok), pl.BlockSpec((t, DV_C), tok),
                  pl.BlockSpec((t, 4 * H_C), lambda b, h: (row_blk0 + b, 0)),
                  pl.BlockSpec((1, 2, 1, nc, GC), lambda b, h: (b, 0, h, 0, 0)),
                  pl.BlockSpec((1, 2, 1, DK_C, DV_C), lambda b, h: (b, 0, h, 0, 0)),
                  pl.BlockSpec((1, DV_C), lambda b, h: (0, 0))],
        out_specs=[pl.BlockSpec((t, DV_C), lambda b, h: (b, h)),
                   pl.BlockSpec((1, 2, 1, DK_C, DV_C), lambda b, h: (b, 0, h, 0, 0))],
        out_shape=[jax.ShapeDtypeStruct((batch * t, C_V), F32),
                   jax.ShapeDtypeStruct((batch, 2, H_C, DK_C, DV_C), F32)],
        scratch_shapes=[pltpu.VMEM((t, DV_C), F32), pltpu.VMEM((2, DK_C, DV_C), F32),
                        pltpu.VMEM((2, 2 * t, DK_C), BF16), pltpu.VMEM((2, t, DV_C), F32),
                        pltpu.VMEM((2, t, GC), BF16), pltpu.VMEM((2, t, DK_C), BF16)],
        compiler_params=_cparams("parallel", "parallel"),
    )(q, k, v, z, gbeta, grow, s0, norm_g.reshape(1, DV_C))


FFN_TM = 512


def _ffn_pre_kernel(x_ref, shift_ref, scale_ref, rw_ref, w13_ref, w2_ref, h_ref, logit_ref, sh_ref):
    h = x_ref[...] * (1.0 + scale_ref[0]) + shift_ref[0]
    hb = h.astype(BF16)
    h_ref[...] = h
    logit_ref[...] = _dot_hp(h, rw_ref[...])
    up = jnp.dot(hb, w13_ref[...], preferred_element_type=F32)
    hid = _silu(up[:, :D_SHARED]) * up[:, D_SHARED:]
    sh_ref[...] = jnp.dot(hid.astype(BF16), w2_ref[...], preferred_element_type=F32)


def _ffn_pre(x, mod3, router_w_pad, ws13, ws2):
    n = x.shape[0]
    tm = FFN_TM
    row = lambda i: _mod_row_of_tile(i, tm)
    return pl.pallas_call(
        _ffn_pre_kernel,
        grid=(n // tm,),
        in_specs=[pl.BlockSpec((tm, D_MODEL), lambda i: (i, 0)),
                  pl.BlockSpec((1, 1, D_MODEL), lambda i: (row(i), 0, 3)),
                  pl.BlockSpec((1, 1, D_MODEL), lambda i: (row(i), 0, 4)),
                  pl.BlockSpec((D_MODEL, LANES), lambda i: (0, 0)),
                  pl.BlockSpec((D_MODEL, 2 * D_SHARED), lambda i: (0, 0)),
                  pl.BlockSpec((D_SHARED, D_MODEL), lambda i: (0, 0))],
        out_specs=[pl.BlockSpec((tm, D_MODEL), lambda i: (i, 0)),
                   pl.BlockSpec((tm, LANES), lambda i: (i, 0)),
                   pl.BlockSpec((tm, D_MODEL), lambda i: (i, 0))],
        out_shape=[jax.ShapeDtypeStruct((n, D_MODEL), F32),
                   jax.ShapeDtypeStruct((n, LANES), F32),
                   jax.ShapeDtypeStruct((n, D_MODEL), F32)],
        compiler_params=_cparams("parallel"),
    )(x, mod3, mod3, router_w_pad, ws13, ws2)


ROUTE_T = 512
PER_GROUP = N_EXPERTS // N_GROUPS
NEG_INF = float("-inf")


def _first_max(x, iota, n):
    m = jnp.max(x, axis=0, keepdims=True)
    first = jnp.min(jnp.where(x == m, iota, n), axis=0, keepdims=True)
    return m, iota == first


def _route_kernel(lg_ref, bias_ref, tri_ref, idx_ref, w_ref, rank_ref, cnt_ref, carry_ref):
    @pl.when(pl.program_id(0) == 0)
    def _():
        carry_ref[...] = jnp.zeros_like(carry_ref)

    t = lg_ref.shape[0]
    logits = jnp.transpose(lg_ref[...])[:N_EXPERTS]
    scores = 1.0 / (1.0 + jnp.exp(-logits))
    sel = scores + bias_ref[...]
    sub_g = lax.broadcasted_iota(jnp.int32, (PER_GROUP, t), 0)
    sub_e = lax.broadcasted_iota(jnp.int32, (N_EXPERTS, t), 0)
    grp_rows = []
    for g in range(N_GROUPS):
        x = sel[g * PER_GROUP:(g + 1) * PER_GROUP]
        m1, hit = _first_max(x, sub_g, PER_GROUP)
        m2 = jnp.max(jnp.where(hit, NEG_INF, x), axis=0, keepdims=True)
        grp_rows.append(m1 + m2)
    cur = jnp.concatenate(grp_rows, axis=0)
    sub_grp = lax.broadcasted_iota(jnp.int32, (N_GROUPS, t), 0)
    grp_on = jnp.zeros((N_GROUPS, t), F32)
    for _ in range(TOPK_GROUPS):
        _, hit = _first_max(cur, sub_grp, N_GROUPS)
        grp_on = jnp.where(hit, 1.0, grp_on)
        cur = jnp.where(hit, NEG_INF, cur)
    exp_on = jnp.concatenate([jnp.broadcast_to(grp_on[g:g + 1], (PER_GROUP, t)) for g in range(N_GROUPS)], axis=0)
    cur = jnp.where(exp_on > 0.0, sel, NEG_INF)
    hits, idx_rows = [], []
    for _ in range(TOP_K):
        _, hit = _first_max(cur, sub_e, N_EXPERTS)
        hits.append(hit)
        idx_rows.append(jnp.sum(jnp.where(hit, sub_e, 0), axis=0, keepdims=True))
        cur = jnp.where(hit, NEG_INF, cur)
    w_rows = [jnp.sum(jnp.where(hit, scores, 0.0), axis=0, keepdims=True) for hit in hits]
    total = functools.reduce(lambda a, b: a + b, w_rows)
    w_ref[...] = jnp.concatenate([w / total * ROUTED_SCALE for w in w_rows], axis=0)
    idx_ref[...] = jnp.concatenate(idx_rows, axis=0)
    chosen = functools.reduce(lambda a, b: a + b, [jnp.where(hit, 1.0, 0.0) for hit in hits])
    incl = jnp.dot(chosen.astype(BF16), tri_ref[...], preferred_element_type=F32)
    before = incl - chosen + carry_ref[:, :1]
    rank_ref[...] = jnp.concatenate(
        [jnp.sum(jnp.where(hit, before, 0.0), axis=0, keepdims=True) for hit in hits], axis=0).astype(jnp.int32)
    carry_ref[...] = carry_ref[...] + incl[:, t - 1:t]
    cnt_ref[...] = carry_ref[...]


def _route(logits, router_bias):
    n = logits.shape[0]
    t = ROUTE_T
    tri = (jnp.arange(t)[:, None] <= jnp.arange(t)[None, :]).astype(BF16)
    bias_b = jnp.broadcast_to(router_bias.astype(F32)[:, None], (N_EXPERTS, t))
    slot = pl.BlockSpec((TOP_K, t), lambda i: (0, i))
    idx, w, rank, cnt = pl.pallas_call(
        _route_kernel,
        grid=(n // t,),
        in_specs=[pl.BlockSpec((t, LANES), lambda i: (i, 0)),
                  pl.BlockSpec((N_EXPERTS, t), lambda i: (0, 0)),
                  pl.BlockSpec((t, t), lambda i: (0, 0))],
        out_specs=[slot, slot, slot, pl.BlockSpec((N_EXPERTS, LANES), lambda i: (0, 0))],
        out_shape=[jax.ShapeDtypeStruct((TOP_K, n), jnp.int32), jax.ShapeDtypeStruct((TOP_K, n), F32),
                   jax.ShapeDtypeStruct((TOP_K, n), jnp.int32), jax.ShapeDtypeStruct((N_EXPERTS, LANES), F32)],
        scratch_shapes=[pltpu.VMEM((N_EXPERTS, LANES), F32)],
        compiler_params=_cparams("arbitrary"),
    )(logits, bias_b, tri)
    return idx, w, rank, cnt[:, 0].astype(jnp.int32)


def _experts_kernel(be_ref, x_ref, w1_ref, w3_ref, w2_ref, o_ref, w1b, w3b, w2b):
    i = pl.program_id(0)
    changed = jnp.logical_or(i == 0, be_ref[i] != be_ref[jnp.maximum(i - 1, 0)])

    @pl.when(changed)
    def _():
        w1b[...] = w1_ref[0].astype(BF16)
        w3b[...] = w3_ref[0].astype(BF16)
        w2b[...] = w2_ref[0].astype(BF16)

    xb = x_ref[...].astype(BF16)
    hid = _silu(jnp.dot(xb, w1b[...], preferred_element_type=F32)) * \
        jnp.dot(xb, w3b[...], preferred_element_type=F32)
    o_ref[...] = jnp.dot(hid.astype(BF16), w2b[...], preferred_element_type=F32)


def _grouped_experts(xs, block_expert, w1, w3, w2):
    rows = xs.shape[0]
    n_blocks = rows // MOE_BLOCK
    grid_spec = pltpu.PrefetchScalarGridSpec(
        num_scalar_prefetch=1,
        grid=(n_blocks,),
        in_specs=[pl.BlockSpec((MOE_BLOCK, D_MODEL), lambda i, be: (i, 0)),
                  pl.BlockSpec((1, D_MODEL, D_EXPERT), lambda i, be: (be[i], 0, 0)),
                  pl.BlockSpec((1, D_MODEL, D_EXPERT), lambda i, be: (be[i], 0, 0)),
                  pl.BlockSpec((1, D_EXPERT, D_MODEL), lambda i, be: (be[i], 0, 0))],
        out_specs=pl.BlockSpec((MOE_BLOCK, D_MODEL), lambda i, be: (i, 0)),
        scratch_shapes=[pltpu.VMEM((D_MODEL, D_EXPERT), BF16), pltpu.VMEM((D_MODEL, D_EXPERT), BF16),
                        pltpu.VMEM((D_EXPERT, D_MODEL), BF16)])
    return pl.pallas_call(
        _experts_kernel,
        grid_spec=grid_spec,
        out_shape=jax.ShapeDtypeStruct((rows, D_MODEL), F32),
        compiler_params=_cparams("arbitrary"),
    )(block_expert, xs, w1, w3, w2)


SC_CORES = 2
SC_SUBCORES = 16
SC_WORKERS = SC_CORES * SC_SUBCORES
SC_ROWS = 32


def _sc_gather(table, idx):
    b = idx.shape[0]
    d = table.shape[1]
    per_worker = b // SC_WORKERS
    n_chunks = per_worker // SC_ROWS
    assert per_worker * SC_WORKERS == b and n_chunks * SC_ROWS == per_worker
    mesh = plsc.VectorSubcoreMesh(core_axis_name="c", subcore_axis_name="s",
                                  num_cores=SC_CORES, num_subcores=SC_SUBCORES)

    @functools.partial(
        pl.kernel, mesh=mesh,
        out_type=jax.ShapeDtypeStruct((b, d), table.dtype),
        scratch_types=[pltpu.VMEM((n_chunks, SC_ROWS), jnp.int32),
                       pltpu.VMEM((SC_ROWS, d), table.dtype),
                       pltpu.SemaphoreType.DMA])
    def gather(table_hbm, idx_hbm, out_hbm, idx_v, rows_v, sem):
        wid = lax.axis_index("s") * SC_CORES + lax.axis_index("c")
        base = wid * per_worker
        pltpu.sync_copy(idx_hbm.at[wid], idx_v)

        @pl.loop(0, n_chunks)
        def _(j):
            pltpu.async_copy(table_hbm.at[idx_v.at[j]], rows_v, sem).wait()
            pltpu.sync_copy(rows_v, out_hbm.at[pl.ds(base + j * SC_ROWS, SC_ROWS)])

    return gather(table, idx.reshape(SC_WORKERS, n_chunks, SC_ROWS))


POST_TM = 256


def _ffn_post_kernel(x_ref, yg_ref, w_ref, s_ref, gate_ref, g_ref, b_ref, o_ref):
    w = w_ref[...]
    routed = yg_ref[0] * w[:, 0:1]
    for k in range(1, TOP_K):
        routed = routed + yg_ref[k] * w[:, k:k + 1]
    r = DEEPNORM_ALPHA * x_ref[...] + gate_ref[0] * (routed + s_ref[...])
    o_ref[...] = _layer_norm_rows(r, g_ref[...], b_ref[...])


def _ffn_post(x, yg, w_tok, shared, mod3, ln_g, ln_b):
    n = x.shape[0]
    tm = POST_TM
    row = lambda i: _mod_row_of_tile(i, tm)
    tile = pl.BlockSpec((tm, D_MODEL), lambda i: (i, 0))
    vec = pl.BlockSpec((1, D_MODEL), lambda i: (0, 0))
    return pl.pallas_call(
        _ffn_post_kernel,
        grid=(n // tm,),
        in_specs=[tile, pl.BlockSpec((TOP_K, tm, D_MODEL), lambda i: (0, i, 0)),
                  pl.BlockSpec((tm, TOP_K), lambda i: (i, 0)), tile,
                  pl.BlockSpec((1, 1, D_MODEL), lambda i: (row(i), 0, 5)), vec, vec],
        out_specs=tile,
        out_shape=jax.ShapeDtypeStruct((n, D_MODEL), F32),
        compiler_params=_cparams("parallel"),
    )(x, yg, w_tok, shared, mod3, ln_g.reshape(1, D_MODEL), ln_b.reshape(1, D_MODEL))


def _moe_routed(h, idx, rank, counts, w1, w3, w2):
    n = h.shape[0]
    padded = (counts + MOE_BLOCK - 1) // MOE_BLOCK * MOE_BLOCK
    pad_end = jnp.cumsum(padded)
    pad_start = pad_end - padded
    dest = pad_start[idx] + rank
    n_blocks = n * TOP_K // MOE_BLOCK + N_EXPERTS
    rows = n_blocks * MOE_BLOCK
    row_tok = (jnp.arange(rows, dtype=jnp.int32) % n).at[dest.reshape(-1)].set(
        jnp.tile(jnp.arange(n, dtype=jnp.int32), TOP_K))
    block_start = jnp.arange(n_blocks, dtype=jnp.int32) * MOE_BLOCK
    block_expert = jnp.minimum(jnp.sum(pad_end[None, :] <= block_start[:, None], axis=1), N_EXPERTS - 1).astype(jnp.int32)
    xs = _sc_gather(h, row_tok)
    ys = _grouped_experts(xs, block_expert, w1, w3, w2)
    return _sc_gather(ys, dest.reshape(-1)).reshape(TOP_K, n, D_MODEL)


def _rms_heads(x, g):
    y = x * lax.rsqrt(jnp.mean(x * x, axis=-1, keepdims=True) + EPS)
    return y * g


def _rope_tables():
    rows = DEC_SEQ // GRID_W
    row = jnp.repeat(jnp.arange(rows), GRID_W).astype(F32)
    colp = jnp.tile(jnp.arange(GRID_W), rows).astype(F32)
    nf = HD_A // 4
    inv_freq = ROPE_THETA ** (-jnp.arange(nf, dtype=F32) / nf)
    ar = row[:, None] * inv_freq[None, :]
    ac = colp[:, None] * inv_freq[None, :]
    cos = jnp.concatenate([jnp.cos(ar), jnp.cos(ar), jnp.cos(ac), jnp.cos(ac)], axis=-1)
    sin = jnp.concatenate([-jnp.sin(ar), jnp.sin(ar), -jnp.sin(ac), jnp.sin(ac)], axis=-1)
    return cos, sin


def _rope(x, cos, sin):
    a, b, c, d = jnp.split(x, 4, axis=-1)
    sw = jnp.concatenate([b, a, d, c], axis=-1)
    return x * cos[None, :, None, :] + sw * sin[None, :, None, :]


def _expand_q(q):
    n = q.shape[0]
    z = jnp.zeros_like(q)
    g0 = jnp.concatenate([q, z], axis=-1)
    g1 = jnp.concatenate([z, q], axis=-1)
    sel = (jnp.arange(N_HEADS_A) // GROUP_A)[None, :, None]
    return jnp.where(sel == 0, g0, g1).reshape(n, N_HEADS_A * LANES)


_ATT_HEAD_ORDER = [g * GROUP_A + j for j in range(GROUP_A) for g in range(N_KV_A)]


def _even_layer(x, mod3, j, w_in, w_out, q_norm, k_norm, log_decay, gn_g,
                cache_k, cache_v, state_ret):
    (p,) = _modulated_proj(x, mod3, 0, w_in.astype(BF16), [EVEN_IN], [F32])
    q = _rms_heads(p[:, :A_Q].reshape(N_TOK, N_HEADS_A, HD_A), q_norm)
    k = _rms_heads(p[:, A_Q:A_Q + A_KV].reshape(N_TOK, N_KV_A, HD_A), k_norm)
    v = p[:, A_Q + A_KV:A_Q + 2 * A_KV].reshape(N_TOK, N_KV_A, HD_A)
    k_p = k[:N_P].reshape(BATCH, SEQ, N_KV_A, HD_A)
    v_p = v[:N_P].reshape(BATCH, SEQ, N_KV_A, HD_A)
    scale = HD_A ** -0.5
    qp = _expand_q(q[:N_P] * scale).astype(BF16)
    ktp = jnp.swapaxes(k_p.reshape(BATCH, SEQ, A_KV), 1, 2).astype(BF16)
    o_p = _attention(qp, ktp, v_p.reshape(BATCH, SEQ, A_KV).astype(BF16), BATCH, SEQ)
    cos, sin = _rope_tables()
    qs = _rope(q[N_P:].reshape(DEC_BATCH, DEC_SEQ, N_HEADS_A, HD_A), cos, sin)
    ks = _rope(k[N_P:].reshape(DEC_BATCH, DEC_SEQ, N_KV_A, HD_A), cos, sin)
    k_all = jnp.concatenate([ks, cache_k], axis=1).reshape(DEC_BATCH, DEC_SEQ + PAST_LEN, A_KV)
    v_all = jnp.concatenate([v[N_P:].reshape(DEC_BATCH, DEC_SEQ, N_KV_A, HD_A), cache_v], axis=1)
    v_all = v_all.reshape(DEC_BATCH, DEC_SEQ + PAST_LEN, A_KV)
    qs = _expand_q(qs.reshape(N_S, N_HEADS_A, HD_A) * scale).astype(BF16)
    o_s = _attention(qs, jnp.swapaxes(k_all, 1, 2).astype(BF16), v_all.astype(BF16), DEC_BATCH, DEC_SEQ)
    o_attn = jnp.concatenate([o_p, o_s], axis=0)
    lg_rows = jnp.broadcast_to(log_decay.reshape(2 * H_B, 1), (2 * H_B, LANES))
    zeros_s = jnp.zeros((BATCH, 2, H_B, DK_B, DV_B), F32)
    r_p, s_p = _retention(p, 0, BATCH, SEQ, lg_rows, zeros_s, gn_g)
    r_s, _ = _retention(p, N_P // DEC_SEQ, DEC_BATCH, DEC_SEQ, lg_rows, state_ret, gn_g)
    o_ret = jnp.concatenate([r_p, r_s], axis=0)
    perm = np.concatenate([np.arange(h * HD_A, (h + 1) * HD_A) for h in _ATT_HEAD_ORDER])
    w_o = jnp.concatenate([w_out[:A_Q][perm], w_out[A_Q:]], axis=0).astype(BF16)
    return [o_attn, o_ret], w_o, k_p, v_p, s_p


def _conv_silu(x, w, batch, t):
    xb = x.reshape(batch, t, x.shape[-1])
    xp = jnp.pad(xb, ((0, 0), ((CONV_K - 1) // 2, CONV_K // 2), (0, 0)))
    y = sum(xp[:, i:i + t] * w[i][None, None, :] for i in range(CONV_K))
    return jax.nn.silu(y).reshape(batch * t, x.shape[-1])


def _l2_heads(x):
    xh = x.reshape(x.shape[0], H_C, DK_C)
    return (xh * lax.rsqrt(jnp.sum(xh * xh, axis=-1, keepdims=True) + EPS)).reshape(x.shape)


def _chunk_cumsum(g, batch, t):
    gc = g.reshape(batch, t // GC, GC, 2, H_C)
    f = jnp.cumsum(gc[:, :, :, 0], axis=2)
    b = jnp.cumsum(gc[:, :, ::-1, 1], axis=2)[:, :, ::-1]
    return jnp.stack([f, b], axis=3)


def _odd_layer(x, mod3, j, w_in, conv_w, a_log, dt_bias, norm_g, w_out, state_gdn):
    w_main = w_in[:, :2 * C_QK + 2 * C_V].astype(BF16)
    w_ab = jnp.pad(w_in[:, 2 * C_QK + 2 * C_V:], ((0, 0), (0, LANES - 4 * H_C))).astype(BF16)
    w_cat = jnp.concatenate([w_main, w_ab], axis=1)
    pm, pab = _modulated_proj(x, mod3, 0, w_cat, [2 * C_QK + 2 * C_V, LANES], [F32, F32])
    qkv = pm[:, :2 * C_QK + C_V]
    z = pm[:, 2 * C_QK + C_V:]
    qkv = jnp.concatenate([_conv_silu(qkv[:N_P], conv_w, BATCH, SEQ),
                           _conv_silu(qkv[N_P:], conv_w, DEC_BATCH, DEC_SEQ)], axis=0)
    q = _l2_heads(qkv[:, :C_QK])
    k = _l2_heads(qkv[:, C_QK:2 * C_QK])
    v = qkv[:, 2 * C_QK:]
    ab = pab[:, :4 * H_C].reshape(N_TOK, 2, 2, H_C)
    beta = jax.nn.sigmoid(ab[:, 0])
    g = -jnp.exp(a_log.astype(F32)) * jax.nn.softplus(ab[:, 1] + dt_bias.astype(F32))

    gcs_p = _chunk_cumsum(g[:N_P], BATCH, SEQ)
    gcs_s = _chunk_cumsum(g[N_P:], DEC_BATCH, DEC_SEQ)
    gcol = jnp.concatenate([gcs_p.reshape(N_P, 2 * H_C), gcs_s.reshape(N_S, 2 * H_C)], axis=0)
    gbeta = jnp.concatenate([beta.reshape(N_TOK, 2 * H_C), gcol], axis=1)
    zeros_s = jnp.zeros((BATCH, 2, H_C, DK_C, DV_C), F32)
    o_p, s_p = _gdn(q, k, v, z, gbeta, gcs_p.transpose(0, 3, 4, 1, 2), 0, BATCH, SEQ, zeros_s, norm_g)
    o_s, _ = _gdn(q, k, v, z, gbeta, gcs_s.transpose(0, 3, 4, 1, 2), N_P // DEC_SEQ, DEC_BATCH, DEC_SEQ,
                  state_gdn, norm_g)
    return [jnp.concatenate([o_p, o_s], axis=0)], w_out.astype(BF16), s_p


def kernel(x_prompt, x_sample, cache_attn_k, cache_attn_v, state_ret, state_gdn, c, c_ctx, mod_w, mod_b, ln_g, ln_b, even_w_in, even_w_out, attn_q_norm, attn_k_norm, ret_log_decay, ret_norm_g, odd_w_in, gdn_conv_w, gdn_a_log, gdn_dt_bias, gdn_norm_g, odd_w_out, router_w, router_bias, expert_w1, expert_w3, expert_w2, shared_w1, shared_w3, shared_w2):
    x = jnp.concatenate([x_prompt.reshape(N_P, D_MODEL), x_sample.reshape(N_S, D_MODEL)], axis=0)
    cvec = jnp.concatenate([c_ctx[None, :], c, jnp.zeros((MOD_ROWS - N_MOD, D_MODEL), F32)], axis=0)
    mod_all = _mod_vectors(cvec, mod_w, mod_b)
    new_k, new_v, new_ret, new_gdn = [], [], [], []
    for l in range(DEPTH):
        j = l // 2
        mod3 = mod_all[l].reshape(MOD_ROWS, 1, 6 * D_MODEL)
        if l % 2 == 0:
            a_list, w_o, k_p, v_p, s_p = _even_layer(
                x, mod3, j, even_w_in[j], even_w_out[j], attn_q_norm[j], attn_k_norm[j],
                ret_log_decay[j], ret_norm_g[j], cache_attn_k[:, j], cache_attn_v[:, j], state_ret[:, j])
            new_k.append(k_p)
            new_v.append(v_p)
            new_ret.append(s_p)
        else:
            a_list, w_o, s_p = _odd_layer(
                x, mod3, j, odd_w_in[j], gdn_conv_w[j], gdn_a_log[j], gdn_dt_bias[j], gdn_norm_g[j],
                odd_w_out[j], state_gdn[:, j])
            new_gdn.append(s_p)
        x = _outproj_ln(a_list, w_o, x, mod3, 2, ln_g[l, 0], ln_b[l, 0])
        rw = jnp.pad(router_w[l], ((0, 0), (0, LANES - N_EXPERTS)))
        ws13 = jnp.concatenate([shared_w1[l], shared_w3[l]], axis=1).astype(BF16)
        h, logits, shared = _ffn_pre(x, mod3, rw, ws13, shared_w2[l].astype(BF16))
        idx, w, rank, counts = _route(logits, router_bias[l])
        yg = _moe_routed(h, idx, rank, counts, expert_w1[l], expert_w3[l], expert_w2[l])
        x = _ffn_post(x, yg, w.T, shared, mod3, ln_g[l, 1], ln_b[l, 1])
    return (x[:N_P].reshape(BATCH, SEQ, D_MODEL), x[N_P:].reshape(DEC_BATCH, DEC_SEQ, D_MODEL),
            jnp.stack(new_k, axis=1), jnp.stack(new_v, axis=1),
            jnp.stack(new_ret, axis=1), jnp.stack(new_gdn, axis=1))
```

```python
import functools
import math

import jax
import jax.numpy as jnp
import numpy as np
from jax import lax
from jax.experimental import pallas as pl
from jax.experimental.pallas import tpu as pltpu
from jax.experimental.pallas import tpu_sc as plsc

D_MODEL = 1024
BATCH = 16
SEQ = 256
DEPTH = 4
DEC_BATCH = 4
DEC_SEQ = 4096
PAST_LEN = 512
GRID_W = 64
N_HEADS_A = 8
N_KV_A = 2
HD_A = 64
ROPE_THETA = 10000.0
H_B = 4
DK_B = 64
DV_B = 128
RET_CHUNK = 128
H_C = 8
DK_C = 128
DV_C = 128
CONV_K = 5
GDN_CHUNK = 64
N_EXPERTS = 64
TOP_K = 8
N_GROUPS = 8
TOPK_GROUPS = 4
D_EXPERT = 256
D_SHARED = 256
ROUTED_SCALE = 2.5
MOE_BLOCK = 256
A_Q = N_HEADS_A * HD_A
A_KV = N_KV_A * HD_A
B_QK = H_B * DK_B
B_V = H_B * DV_B
EVEN_IN = A_Q + 2 * A_KV + 2 * B_QK + 2 * B_V
C_QK = H_C * DK_C
C_V = H_C * DV_C
DEEPNORM_ALPHA = (2 * DEPTH) ** 0.25
EPS = 1e-6

N_P = BATCH * SEQ
N_S = DEC_BATCH * DEC_SEQ
N_TOK = N_P + N_S
N_MOD = 1 + DEC_BATCH
MOD_ROWS = 8

LANES = 128
VMEM_LIMIT = 56 * 1024 * 1024

F32 = jnp.float32
BF16 = jnp.bfloat16


def _cparams(*sem):
    return pltpu.CompilerParams(dimension_semantics=sem, vmem_limit_bytes=VMEM_LIMIT)


def _bdot(a, b):
    return jnp.dot(a.astype(BF16), b.astype(BF16), preferred_element_type=F32)


def _bdot_t(a, b):
    return lax.dot_general(a.astype(BF16), b.astype(BF16), (((0,), (0,)), ((), ())),
                           preferred_element_type=F32)


def _bdot_nt(a, b):
    return lax.dot_general(a.astype(BF16), b.astype(BF16), (((1,), (1,)), ((), ())),
                           preferred_element_type=F32)


def _split3(a):
    hi = a.astype(BF16)
    r = a - hi.astype(F32)
    mid = r.astype(BF16)
    lo = (r - mid.astype(F32)).astype(BF16)
    return hi, mid, lo


def _dot_hp(a, b):
    a0, a1, a2 = _split3(a)
    b0, b1, b2 = _split3(b)
    d = lambda x, y: jnp.dot(x, y, preferred_element_type=F32)
    small = d(a0, b2) + d(a2, b0) + d(a1, b1)
    return (d(a0, b1) + d(a1, b0)) + small + d(a0, b0)


def _silu(x):
    return x * (1.0 / (1.0 + jnp.exp(-x)))


HI16 = 0xFFFF0000
D_PACK = D_MODEL // 2


def _pack_bf16_pairs(x):
    c = x.shape[1] // 2
    lo = pltpu.bitcast(x[:, :c].astype(BF16).astype(F32), jnp.uint32) >> 16
    hi = pltpu.bitcast(x[:, c:].astype(BF16).astype(F32), jnp.uint32) & jnp.uint32(HI16)
    return pltpu.bitcast(lo | hi, jnp.int32)


def _unpack_bf16_pairs(p):
    u = pltpu.bitcast(p, jnp.uint32)
    return jnp.concatenate([pltpu.bitcast(u << 16, F32), pltpu.bitcast(u & jnp.uint32(HI16), F32)], axis=1)


def _mod_row_of_tile(i, tile):
    tiles_p = N_P // tile
    tiles_per_b = DEC_SEQ // tile
    return jnp.where(i < tiles_p, 0, 1 + (i - tiles_p) // tiles_per_b)


MOD_TN = 1536


def _mod_kernel(c_ref, w_ref, b_ref, o_ref):
    a = _silu(c_ref[...])
    o_ref[0] = _bdot(a, w_ref[0]) + b_ref[0]


def _mod_vectors(cvec, mod_w, mod_b):
    n6 = 6 * D_MODEL
    return pl.pallas_call(
        _mod_kernel,
        grid=(DEPTH, n6 // MOD_TN),
        in_specs=[pl.BlockSpec((MOD_ROWS, D_MODEL), lambda l, j: (0, 0)),
                  pl.BlockSpec((1, D_MODEL, MOD_TN), lambda l, j: (l, 0, j)),
                  pl.BlockSpec((1, 1, MOD_TN), lambda l, j: (l, 0, j))],
        out_specs=pl.BlockSpec((1, MOD_ROWS, MOD_TN), lambda l, j: (l, 0, j)),
        out_shape=jax.ShapeDtypeStruct((DEPTH, MOD_ROWS, n6), F32),
        compiler_params=_cparams("parallel", "parallel"),
    )(cvec, mod_w, mod_b.reshape(DEPTH, 1, n6))


PROJ_TM = 256


def _proj_kernel(x_ref, shift_ref, scale_ref, w_ref, *o_refs, widths):
    h = (x_ref[...] * (1.0 + scale_ref[0]) + shift_ref[0]).astype(BF16)
    off = 0
    for o_ref, wd in zip(o_refs, widths):
        o_ref[...] = jnp.dot(h, w_ref[:, off:off + wd], preferred_element_type=F32).astype(o_ref.dtype)
        off += wd


def _modulated_proj(x, mod3, shift_blk, w_bf16, widths, dtypes):
    n = x.shape[0]
    tm = PROJ_TM
    row = lambda i: _mod_row_of_tile(i, tm)
    return pl.pallas_call(
        functools.partial(_proj_kernel, widths=tuple(widths)),
        grid=(n // tm,),
        in_specs=[pl.BlockSpec((tm, D_MODEL), lambda i: (i, 0)),
                  pl.BlockSpec((1, 1, D_MODEL), lambda i: (row(i), 0, shift_blk)),
                  pl.BlockSpec((1, 1, D_MODEL), lambda i: (row(i), 0, shift_blk + 1)),
                  pl.BlockSpec((D_MODEL, sum(widths)), lambda i: (0, 0))],
        out_specs=[pl.BlockSpec((tm, wd), lambda i: (i, 0)) for wd in widths],
        out_shape=[jax.ShapeDtypeStruct((n, wd), dt) for wd, dt in zip(widths, dtypes)],
        compiler_params=_cparams("parallel"),
    )(x, mod3, mod3, w_bf16)


OUT_TM = 512


def _layer_norm_rows(r, g, b):
    mu = jnp.mean(r, axis=-1, keepdims=True)
    d = r - mu
    var = jnp.mean(d * d, axis=-1, keepdims=True)
    return d * lax.rsqrt(var + EPS) * g + b


def _outproj_kernel(*refs, n_a):
    a_refs = refs[:n_a]
    w_ref, x_ref, gate_ref, g_ref, b_ref, o_ref = refs[n_a:]
    off = 0
    acc = None
    for a_ref in a_refs:
        wd = a_ref.shape[1]
        part = jnp.dot(a_ref[...].astype(BF16), w_ref[off:off + wd, :], preferred_element_type=F32)
        acc = part if acc is None else acc + part
        off += wd
    r = DEEPNORM_ALPHA * x_ref[...] + gate_ref[0] * acc
    o_ref[...] = _layer_norm_rows(r, g_ref[...], b_ref[...])


def _outproj_ln(a_list, w_bf16, x, mod3, gate_blk, ln_g, ln_b):
    n = x.shape[0]
    tm = OUT_TM
    row = lambda i: _mod_row_of_tile(i, tm)
    kdim = w_bf16.shape[0]
    return pl.pallas_call(
        functools.partial(_outproj_kernel, n_a=len(a_list)),
        grid=(n // tm,),
        in_specs=[pl.BlockSpec((tm, a.shape[1]), lambda i: (i, 0)) for a in a_list] + [
            pl.BlockSpec((kdim, D_MODEL), lambda i: (0, 0)),
            pl.BlockSpec((tm, D_MODEL), lambda i: (i, 0)),
            pl.BlockSpec((1, 1, D_MODEL), lambda i: (row(i), 0, gate_blk)),
            pl.BlockSpec((1, D_MODEL), lambda i: (0, 0)),
            pl.BlockSpec((1, D_MODEL), lambda i: (0, 0))],
        out_specs=pl.BlockSpec((tm, D_MODEL), lambda i: (i, 0)),
        out_shape=jax.ShapeDtypeStruct((n, D_MODEL), F32),
        compiler_params=_cparams("parallel"),
    )(*a_list, w_bf16, x, mod3, ln_g.reshape(1, D_MODEL), ln_b.reshape(1, D_MODEL))


ATT_TQ = 256
GROUP_A = N_HEADS_A // N_KV_A


def _attn_kernel(q_ref, kt_ref, v_ref, o_ref):
    kt = kt_ref[0]
    v = v_ref[0]
    lane = lax.broadcasted_iota(jnp.int32, (1, LANES), 1)
    for j in range(GROUP_A):
        outs = []
        for g in range(N_KV_A):
            h = g * GROUP_A + j
            s = jnp.dot(q_ref[:, h * LANES:(h + 1) * LANES], kt, preferred_element_type=F32)
            m = jnp.max(s, axis=-1, keepdims=True)
            p = jnp.exp(s - m)
            l = jnp.sum(p, axis=-1, keepdims=True)
            pv = jnp.dot(p.astype(BF16), v, preferred_element_type=F32)
            outs.append(pv * (1.0 / l))
        o_ref[:, j * LANES:(j + 1) * LANES] = jnp.where(lane < HD_A, outs[0], outs[1]).astype(o_ref.dtype)


def _attention(q_exp, kt, v, batch, t):
    tk = v.shape[1]
    tq = min(ATT_TQ, t)
    nq = t // tq
    return pl.pallas_call(
        _attn_kernel,
        grid=(batch, nq),
        in_specs=[pl.BlockSpec((tq, N_HEADS_A * LANES), lambda b, i: (b * nq + i, 0)),
                  pl.BlockSpec((1, LANES, tk), lambda b, i: (b, 0, 0)),
                  pl.BlockSpec((1, tk, LANES), lambda b, i: (b, 0, 0))],
        out_specs=pl.BlockSpec((tq, A_Q), lambda b, i: (b * nq + i, 0)),
        out_shape=jax.ShapeDtypeStruct((batch * t, A_Q), BF16),
        compiler_params=_cparams("parallel", "parallel"),
    )(q_exp, kt, v)


RC = RET_CHUNK


def _ret_kernel(q_ref, k_ref, v_ref, g_ref, lg_ref, s0_ref, gn_ref, o_ref, s_out_ref,
                ob_ref, st_ref, *, t):
    nc = t // RC
    hp = pl.program_id(1)
    ii = lax.broadcasted_iota(jnp.int32, (RC, RC), 0).astype(F32)
    jj = lax.broadcasted_iota(jnp.int32, (RC, RC), 1).astype(F32)
    col_i = lax.broadcasted_iota(jnp.int32, (RC, 1), 0).astype(F32)
    lane = lax.broadcasted_iota(jnp.int32, (1, LANES), 1)
    masks = [(lane >= hh * DK_B) & (lane < (hh + 1) * DK_B) for hh in range(2)]

    consts = []
    for d in range(2):
        for hh in range(2):
            lg = lg_ref[pl.ds(d * H_B + hp * 2 + hh, 1), :][:, :1]
            if d == 0:
                diff = ii - jj
                qe, ke = col_i + 1.0, (RC - 1.0) - col_i
            else:
                diff = jj - ii
                qe, ke = RC - col_i, col_i
            intra = jnp.where(diff >= 0, jnp.exp(jnp.maximum(diff, 0.0) * lg), 0.0)
            consts.append((intra, jnp.exp(qe * lg), jnp.exp(ke * lg), jnp.exp(RC * lg)))
            s0 = s0_ref[0, d, hh]
            z = jnp.zeros((DK_B, DV_B), F32)
            st_ref[d * 2 + hh] = jnp.concatenate([s0, z] if hh == 0 else [z, s0], axis=0)

    def chunk(c, d):
        r0 = pl.multiple_of(c * RC, RC)
        qc = q_ref[pl.ds(r0, RC), :]
        kc = k_ref[pl.ds(r0, RC), :] * (DK_B ** -0.5)
        outs = []
        for hh in range(2):
            intra, q_dec, k_dec, c_dec = consts[d * 2 + hh]
            vc = v_ref[pl.ds(r0, RC), hh * DV_B:(hh + 1) * DV_B]
            qh = jnp.where(masks[hh], qc, 0.0)
            kh = jnp.where(masks[hh], kc, 0.0)
            s = st_ref[d * 2 + hh]
            scores = _bdot_nt(qh, kh) * intra
            o = _bdot(scores, vc) + _bdot(qh * q_dec, s)
            st_ref[d * 2 + hh] = s * c_dec + _bdot_t(kh * k_dec, vc)
            outs.append(o)
        return r0, jnp.concatenate(outs, axis=1)

    def body(c, carry):
        r0, of = chunk(c, 0)
        o_ref[pl.ds(r0, RC), :] = of
        r1, ob = chunk(nc - 1 - c, 1)
        ob_ref[pl.ds(r1, RC), :] = ob
        return carry

    lax.fori_loop(0, nc, body, 0)

    for d in range(2):
        for hh in range(2):
            s_out_ref[0, d, hh] = st_ref[d * 2 + hh][hh * DK_B:(hh + 1) * DK_B, :]

    def finish(c, carry):
        r0 = pl.multiple_of(c * RC, RC)
        o = o_ref[pl.ds(r0, RC), :] + ob_ref[pl.ds(r0, RC), :]
        gate = _silu(g_ref[pl.ds(r0, RC), :])
        ys = []
        for hh in range(2):
            oh = o[:, hh * DV_B:(hh + 1) * DV_B]
            mu = jnp.mean(oh, axis=-1, keepdims=True)
            dlt = oh - mu
            var = jnp.mean(dlt * dlt, axis=-1, keepdims=True)
            ys.append(dlt * lax.rsqrt(var + EPS))
        o_ref[pl.ds(r0, RC), :] = jnp.concatenate(ys, axis=1) * gn_ref[...] * gate
        return carry

    lax.fori_loop(0, nc, finish, 0)


def _retention(p, row_blk0, batch, t, lg_rows, s0, gn_g):
    qb, kb, vb, gb = (A_Q + 2 * A_KV) // LANES, (A_Q + 2 * A_KV + B_QK) // LANES, \
        (A_Q + 2 * A_KV + 2 * B_QK) // (2 * DV_B), (A_Q + 2 * A_KV + 2 * B_QK + B_V) // (2 * DV_B)
    return pl.pallas_call(
        functools.partial(_ret_kernel, t=t),
        grid=(batch, H_B // 2),
        in_specs=[pl.BlockSpec((t, LANES), lambda b, h: (row_blk0 + b, qb + h)),
                  pl.BlockSpec((t, LANES), lambda b, h: (row_blk0 + b, kb + h)),
                  pl.BlockSpec((t, 2 * DV_B), lambda b, h: (row_blk0 + b, vb + h)),
                  pl.BlockSpec((t, 2 * DV_B), lambda b, h: (row_blk0 + b, gb + h)),
                  pl.BlockSpec((2 * H_B, LANES), lambda b, h: (0, 0)),
                  pl.BlockSpec((1, 2, 2, DK_B, DV_B), lambda b, h: (b, 0, h, 0, 0)),
                  pl.BlockSpec((1, 2 * DV_B), lambda b, h: (0, h))],
        out_specs=[pl.BlockSpec((t, 2 * DV_B), lambda b, h: (b, h)),
                   pl.BlockSpec((1, 2, 2, DK_B, DV_B), lambda b, h: (b, 0, h, 0, 0))],
        out_shape=[jax.ShapeDtypeStruct((batch * t, B_V), F32),
                   jax.ShapeDtypeStruct((batch, 2, H_B, DK_B, DV_B), F32)],
        scratch_shapes=[pltpu.VMEM((t, 2 * DV_B), F32), pltpu.VMEM((4, LANES, DV_B), F32)],
        compiler_params=_cparams("parallel", "parallel"),
    )(p, p, p, p, lg_rows, s0, gn_g.reshape(1, B_V))


GC = 256


def _unit_tri_inverse(a):
    ii = lax.broadcasted_iota(jnp.int32, (GC, GC), 0)
    jj = lax.broadcasted_iota(jnp.int32, (GC, GC), 1)

    block_dist = ii ^ jj

    eye = (ii == jj).astype(F32)
    m = jnp.where((block_dist >> 3) == 0, -a, 0.0)
    inv = eye + m
    m2 = _bdot(m, m)
    inv = inv + _bdot(inv, m2)
    m4 = _bdot(m2, m2)
    inv = inv + _bdot(inv, m4)
    for shift in range(3, int(math.log2(GC))):
        l = jnp.where((block_dist >> shift) == 1, a, 0.0)
        inv = inv - _bdot(_bdot(inv, l), inv)
    return inv


def _gdn_kernel(q_ref, k_ref, v_ref, z_ref, gb_ref, gr_ref, s0_ref, ng_ref, o_ref, s_out_ref,
                ob_ref, st_ref, wq_s, u_s, a_s, kg_s, *, t):
    nc = t // GC
    h = pl.program_id(1)
    ii = lax.broadcasted_iota(jnp.int32, (GC, GC), 0)
    jj = lax.broadcasted_iota(jnp.int32, (GC, GC), 1)
    incl = [ii >= jj, jj >= ii]
    strict = [ii > jj, jj > ii]
    lane32 = lax.broadcasted_iota(jnp.int32, (1, 4 * H_C), 1)
    for d in range(2):
        st_ref[d] = s0_ref[0, d, 0]

    def col(x, idx):
        return jnp.sum(jnp.where(lane32 == idx, x, 0.0), axis=1, keepdims=True)

    def gate_last(c, d):
        grow = gr_ref[0, d, 0, pl.ds(c, 1), :]
        return grow, (grow[:, GC - 1:GC] if d == 0 else grow[:, 0:1])

    def prep(c, carry):
        rows = pl.ds(pl.multiple_of(c * GC, GC), GC)
        qc = q_ref[rows, :] * (DK_C ** -0.5)
        kc = k_ref[rows, :]
        vc = v_ref[rows, :]
        gb = gb_ref[rows, :]
        kk = _bdot_nt(kc, kc)
        qk = _bdot_nt(qc, kc)
        for d in range(2):
            beta = col(gb, d * H_C + h)
            gcol = col(gb, 2 * H_C + d * H_C + h)
            grow, glast = gate_last(c, d)
            decay = jnp.exp(jnp.where(incl[d], gcol - grow, -jnp.inf))
            eg = jnp.exp(gcol)
            kb = kc * beta
            a = jnp.where(strict[d], kk * beta * decay, 0.0)
            tinv = _unit_tri_inverse(a)
            wu = _bdot(tinv, jnp.concatenate([kb * eg, vc * beta], axis=1))
            r2 = pl.multiple_of(c * 2 * GC, 2 * GC)
            wq_s[d, pl.ds(r2, GC), :] = wu[:, :DK_C].astype(BF16)
            wq_s[d, pl.ds(r2 + GC, GC), :] = (qc * eg).astype(BF16)
            u_s[d, rows, :] = wu[:, DK_C:]
            a_s[d, rows, :] = (qk * decay).astype(BF16)
            kg_s[d, rows, :] = (kc * jnp.exp(glast - gcol)).astype(BF16)
        return carry

    lax.fori_loop(0, nc, prep, 0)

    def step(c, carry):
        for d in range(2):
            cc = c if d == 0 else nc - 1 - c
            rows = pl.ds(pl.multiple_of(cc * GC, GC), GC)
            _, glast = gate_last(cc, d)
            s = st_ref[d]
            sb = s.astype(BF16)
            wq = wq_s[d, pl.ds(pl.multiple_of(cc * 2 * GC, 2 * GC), 2 * GC), :]
            ws = jnp.dot(wq, sb, preferred_element_type=F32)
            v_new = u_s[d, rows, :] - ws[:GC]
            vnb = v_new.astype(BF16)
            o = ws[GC:] + jnp.dot(a_s[d, rows, :], vnb, preferred_element_type=F32)
            st_ref[d] = s * jnp.exp(glast) + _bdot_t(kg_s[d, rows, :], vnb)
            if d == 0:
                o_ref[rows, :] = o
            else:
                ob_ref[rows, :] = o
        return carry

    lax.fori_loop(0, nc, step, 0)

    for d in range(2):
        s_out_ref[0, d, 0] = st_ref[d]

    def finish(c, carry):
        r0 = pl.multiple_of(c * GC, GC)
        o = o_ref[pl.ds(r0, GC), :] + ob_ref[pl.ds(r0, GC), :]
        y = o * lax.rsqrt(jnp.mean(o * o, axis=-1, keepdims=True) + EPS) * ng_ref[...]
        o_ref[pl.ds(r0, GC), :] = y * _silu(z_ref[pl.ds(r0, GC), :])
        return carry

    lax.fori_loop(0, nc, finish, 0)


def _gdn(q, k, v, z, gbeta, grow, row_blk0, batch, t, s0, norm_g):
    nc = t // GC
    tok = lambda b, h: (row_blk0 + b, h)
    return pl.pallas_call(
        functools.partial(_gdn_kernel, t=t),
        grid=(batch, H_C),
        in_specs=[pl.BlockSpec((t, DK_C), tok), pl.BlockSpec((t, DK_C), tok),
                  pl.BlockSpec((t, DV_C), tok), pl.BlockSpec((t, DV_C), tok),
                  pl.BlockSpec((t, 4 * H_C), lambda b, h: (row_blk0 + b, 0)),
                  pl.BlockSpec((1, 2, 1, nc, GC), lambda b, h: (b, 0, h, 0, 0)),
                  pl.BlockSpec((1, 2, 1, DK_C, DV_C), lambda b, h: (b, 0, h, 0, 0)),
                  pl.BlockSpec((1, DV_C), lambda b, h: (0, 0))],
        out_specs=[pl.BlockSpec((t, DV_C), lambda b, h: (b, h)),
                   pl.BlockSpec((1, 2, 1, DK_C, DV_C), lambda b, h: (b, 0, h, 0, 0))],
        out_shape=[jax.ShapeDtypeStruct((batch * t, C_V), F32),
                   jax.ShapeDtypeStruct((batch, 2, H_C, DK_C, DV_C), F32)],
        scratch_shapes=[pltpu.VMEM((t, DV_C), F32), pltpu.VMEM((2, DK_C, DV_C), F32),
                        pltpu.VMEM((2, 2 * t, DK_C), BF16), pltpu.VMEM((2, t, DV_C), F32),
                        pltpu.VMEM((2, t, GC), BF16), pltpu.VMEM((2, t, DK_C), BF16)],
        compiler_params=_cparams("parallel", "parallel"),
    )(q, k, v, z, gbeta, grow, s0, norm_g.reshape(1, DV_C))


FFN_TM = 512


def _ffn_pre_kernel(x_ref, shift_ref, scale_ref, rw_ref, w13_ref, w2_ref, h_ref, logit_ref, sh_ref):
    h = x_ref[...] * (1.0 + scale_ref[0]) + shift_ref[0]
    hb = h.astype(BF16)
    h_ref[...] = _pack_bf16_pairs(h)
    logit_ref[...] = _dot_hp(h, rw_ref[...])
    up = jnp.dot(hb, w13_ref[...], preferred_element_type=F32)
    hid = _silu(up[:, :D_SHARED]) * up[:, D_SHARED:]
    sh_ref[...] = jnp.dot(hid.astype(BF16), w2_ref[...], preferred_element_type=F32)


def _ffn_pre(x, mod3, router_w_pad, ws13, ws2):
    n = x.shape[0]
    tm = FFN_TM
    row = lambda i: _mod_row_of_tile(i, tm)
    return pl.pallas_call(
        _ffn_pre_kernel,
        grid=(n // tm,),
        in_specs=[pl.BlockSpec((tm, D_MODEL), lambda i: (i, 0)),
                  pl.BlockSpec((1, 1, D_MODEL), lambda i: (row(i), 0, 3)),
                  pl.BlockSpec((1, 1, D_MODEL), lambda i: (row(i), 0, 4)),
                  pl.BlockSpec((D_MODEL, LANES), lambda i: (0, 0)),
                  pl.BlockSpec((D_MODEL, 2 * D_SHARED), lambda i: (0, 0)),
                  pl.BlockSpec((D_SHARED, D_MODEL), lambda i: (0, 0))],
        out_specs=[pl.BlockSpec((tm, D_PACK), lambda i: (i, 0)),
                   pl.BlockSpec((tm, LANES), lambda i: (i, 0)),
                   pl.BlockSpec((tm, D_MODEL), lambda i: (i, 0))],
        out_shape=[jax.ShapeDtypeStruct((n, D_PACK), jnp.int32),
                   jax.ShapeDtypeStruct((n, LANES), F32),
                   jax.ShapeDtypeStruct((n, D_MODEL), F32)],
        compiler_params=_cparams("parallel"),
    )(x, mod3, mod3, router_w_pad, ws13, ws2)


ROUTE_T = 512
PER_GROUP = N_EXPERTS // N_GROUPS
NEG_INF = float("-inf")


def _first_max(x, iota, n):
    m = jnp.max(x, axis=0, keepdims=True)
    first = jnp.min(jnp.where(x == m, iota, n), axis=0, keepdims=True)
    return m, iota == first


def _route_kernel(lg_ref, bias_ref, tri_ref, idx_ref, w_ref, rank_ref, cnt_ref, carry_ref):
    @pl.when(pl.program_id(0) == 0)
    def _():
        carry_ref[...] = jnp.zeros_like(carry_ref)

    t = lg_ref.shape[0]
    logits = jnp.transpose(lg_ref[...])[:N_EXPERTS]
    scores = 1.0 / (1.0 + jnp.exp(-logits))
    sel = scores + bias_ref[...]
    sub_g = lax.broadcasted_iota(jnp.int32, (PER_GROUP, t), 0)
    sub_e = lax.broadcasted_iota(jnp.int32, (N_EXPERTS, t), 0)
    grp_rows = []
    for g in range(N_GROUPS):
        x = sel[g * PER_GROUP:(g + 1) * PER_GROUP]
        m1, hit = _first_max(x, sub_g, PER_GROUP)
        m2 = jnp.max(jnp.where(hit, NEG_INF, x), axis=0, keepdims=True)
        grp_rows.append(m1 + m2)
    cur = jnp.concatenate(grp_rows, axis=0)
    sub_grp = lax.broadcasted_iota(jnp.int32, (N_GROUPS, t), 0)
    grp_on = jnp.zeros((N_GROUPS, t), F32)
    for _ in range(TOPK_GROUPS):
        _, hit = _first_max(cur, sub_grp, N_GROUPS)
        grp_on = jnp.where(hit, 1.0, grp_on)
        cur = jnp.where(hit, NEG_INF, cur)
    exp_on = jnp.concatenate([jnp.broadcast_to(grp_on[g:g + 1], (PER_GROUP, t)) for g in range(N_GROUPS)], axis=0)
    cur = jnp.where(exp_on > 0.0, sel, NEG_INF)
    hits, idx_rows = [], []
    for _ in range(TOP_K):
        _, hit = _first_max(cur, sub_e, N_EXPERTS)
        hits.append(hit)
        idx_rows.append(jnp.sum(jnp.where(hit, sub_e, 0), axis=0, keepdims=True))
        cur = jnp.where(hit, NEG_INF, cur)
    w_rows = [jnp.sum(jnp.where(hit, scores, 0.0), axis=0, keepdims=True) for hit in hits]
    total = functools.reduce(lambda a, b: a + b, w_rows)
    w_ref[...] = jnp.concatenate([w / total * ROUTED_SCALE for w in w_rows], axis=0)
    idx_ref[...] = jnp.concatenate(idx_rows, axis=0)
    chosen = functools.reduce(lambda a, b: a + b, [jnp.where(hit, 1.0, 0.0) for hit in hits])
    incl = jnp.dot(chosen.astype(BF16), tri_ref[...], preferred_element_type=F32)
    before = incl - chosen + carry_ref[:, :1]
    rank_ref[...] = jnp.concatenate(
        [jnp.sum(jnp.where(hit, before, 0.0), axis=0, keepdims=True) for hit in hits], axis=0).astype(jnp.int32)
    carry_ref[...] = carry_ref[...] + incl[:, t - 1:t]
    cnt_ref[...] = carry_ref[...]


def _route(logits, router_bias):
    n = logits.shape[0]
    t = ROUTE_T
    tri = (jnp.arange(t)[:, None] <= jnp.arange(t)[None, :]).astype(BF16)
    bias_b = jnp.broadcast_to(router_bias.astype(F32)[:, None], (N_EXPERTS, t))
    slot = pl.BlockSpec((TOP_K, t), lambda i: (0, i))
    idx, w, rank, cnt = pl.pallas_call(
        _route_kernel,
        grid=(n // t,),
        in_specs=[pl.BlockSpec((t, LANES), lambda i: (i, 0)),
                  pl.BlockSpec((N_EXPERTS, t), lambda i: (0, 0)),
                  pl.BlockSpec((t, t), lambda i: (0, 0))],
        out_specs=[slot, slot, slot, pl.BlockSpec((N_EXPERTS, LANES), lambda i: (0, 0))],
        out_shape=[jax.ShapeDtypeStruct((TOP_K, n), jnp.int32), jax.ShapeDtypeStruct((TOP_K, n), F32),
                   jax.ShapeDtypeStruct((TOP_K, n), jnp.int32), jax.ShapeDtypeStruct((N_EXPERTS, LANES), F32)],
        scratch_shapes=[pltpu.VMEM((N_EXPERTS, LANES), F32)],
        compiler_params=_cparams("arbitrary"),
    )(logits, bias_b, tri)
    return idx, w, rank, cnt[:, 0].astype(jnp.int32)


def _experts_kernel(be_ref, bv_ref, x_ref, w1_ref, w3_ref, w2_ref, o_ref, w1b, w3b, w2b):
    i = pl.program_id(0)
    changed = jnp.logical_or(i == 0, be_ref[i] != be_ref[jnp.maximum(i - 1, 0)])

    @pl.when(changed)
    def _():
        w1b[...] = w1_ref[0].astype(BF16)
        w3b[...] = w3_ref[0].astype(BF16)
        w2b[...] = w2_ref[0].astype(BF16)

    live = lax.broadcasted_iota(jnp.int32, (MOE_BLOCK, 1), 0) < bv_ref[i]
    xb = _unpack_bf16_pairs(jnp.where(live, x_ref[...], 0)).astype(BF16)
    hid = _silu(jnp.dot(xb, w1b[...], preferred_element_type=F32)) * \
        jnp.dot(xb, w3b[...], preferred_element_type=F32)
    o_ref[...] = _pack_bf16_pairs(jnp.dot(hid.astype(BF16), w2b[...], preferred_element_type=F32))


def _grouped_experts(xs, block_expert, block_valid, w1, w3, w2):
    rows = xs.shape[0]
    n_blocks = rows // MOE_BLOCK
    grid_spec = pltpu.PrefetchScalarGridSpec(
        num_scalar_prefetch=2,
        grid=(n_blocks,),
        in_specs=[pl.BlockSpec((MOE_BLOCK, D_PACK), lambda i, be, bv: (i, 0)),
                  pl.BlockSpec((1, D_MODEL, D_EXPERT), lambda i, be, bv: (be[i], 0, 0)),
                  pl.BlockSpec((1, D_MODEL, D_EXPERT), lambda i, be, bv: (be[i], 0, 0)),
                  pl.BlockSpec((1, D_EXPERT, D_MODEL), lambda i, be, bv: (be[i], 0, 0))],
        out_specs=pl.BlockSpec((MOE_BLOCK, D_PACK), lambda i, be, bv: (i, 0)),
        scratch_shapes=[pltpu.VMEM((D_MODEL, D_EXPERT), BF16), pltpu.VMEM((D_MODEL, D_EXPERT), BF16),
                        pltpu.VMEM((D_EXPERT, D_MODEL), BF16)])
    return pl.pallas_call(
        _experts_kernel,
        grid_spec=grid_spec,
        out_shape=jax.ShapeDtypeStruct((rows, D_PACK), jnp.int32),
        compiler_params=_cparams("arbitrary"),
    )(block_expert, block_valid, xs, w1, w3, w2)


SC_CORES = 2
SC_SUBCORES = 16
SC_WORKERS = SC_CORES * SC_SUBCORES
SC_ROWS = 64


def _sc_mesh():
    return plsc.VectorSubcoreMesh(core_axis_name="c", subcore_axis_name="s",
                                  num_cores=SC_CORES, num_subcores=SC_SUBCORES)


def _sc_dispatch(table, dest, rows):
    n, d = table.shape
    kk = dest.shape[0]
    per_worker = n // SC_WORKERS
    n_chunks = per_worker // SC_ROWS
    assert per_worker * SC_WORKERS == n and n_chunks * SC_ROWS == per_worker
    idx = dest.reshape(kk, SC_WORKERS, n_chunks, SC_ROWS).transpose(1, 2, 0, 3).reshape(
        SC_WORKERS, n_chunks * kk, SC_ROWS)

    @functools.partial(
        pl.kernel, mesh=_sc_mesh(),
        out_type=jax.ShapeDtypeStruct((rows, d), table.dtype),
        scratch_types=[pltpu.VMEM((n_chunks * kk, SC_ROWS), jnp.int32),
                       pltpu.VMEM((SC_ROWS, d), table.dtype),
                       pltpu.SemaphoreType.DMA])
    def dispatch(table_hbm, idx_hbm, out_hbm, idx_v, rows_v, sem):
        wid = lax.axis_index("s") * SC_CORES + lax.axis_index("c")
        base = wid * per_worker
        pltpu.sync_copy(idx_hbm.at[wid], idx_v)

        @pl.loop(0, n_chunks)
        def _(j):
            pltpu.sync_copy(table_hbm.at[pl.ds(base + j * SC_ROWS, SC_ROWS)], rows_v)
            copies = [pltpu.async_copy(rows_v, out_hbm.at[idx_v.at[j * kk + k]], sem) for k in range(kk)]
            for cp in copies:
                cp.wait()

    return dispatch(table, idx)


def _sc_gather(table, idx):
    b = idx.shape[0]
    d = table.shape[1]
    per_worker = b // SC_WORKERS
    n_chunks = per_worker // SC_ROWS
    assert per_worker * SC_WORKERS == b and n_chunks * SC_ROWS == per_worker

    @functools.partial(
        pl.kernel, mesh=_sc_mesh(),
        out_type=jax.ShapeDtypeStruct((b, d), table.dtype),
        scratch_types=[pltpu.VMEM((n_chunks, SC_ROWS), jnp.int32),
                       pltpu.VMEM((SC_ROWS, d), table.dtype),
                       pltpu.SemaphoreType.DMA])
    def gather(table_hbm, idx_hbm, out_hbm, idx_v, rows_v, sem):
        wid = lax.axis_index("s") * SC_CORES + lax.axis_index("c")
        base = wid * per_worker
        pltpu.sync_copy(idx_hbm.at[wid], idx_v)

        @pl.loop(0, n_chunks)
        def _(j):
            pltpu.async_copy(table_hbm.at[idx_v.at[j]], rows_v, sem).wait()
            pltpu.sync_copy(rows_v, out_hbm.at[pl.ds(base + j * SC_ROWS, SC_ROWS)])

    return gather(table, idx.reshape(SC_WORKERS, n_chunks, SC_ROWS))


POST_TM = 256


def _ffn_post_kernel(x_ref, yg_ref, w_ref, s_ref, gate_ref, g_ref, b_ref, o_ref):
    w = w_ref[...]
    routed = _unpack_bf16_pairs(yg_ref[0]) * w[:, 0:1]
    for k in range(1, TOP_K):
        routed = routed + _unpack_bf16_pairs(yg_ref[k]) * w[:, k:k + 1]
    r = DEEPNORM_ALPHA * x_ref[...] + gate_ref[0] * (routed + s_ref[...])
    o_ref[...] = _layer_norm_rows(r, g_ref[...], b_ref[...])


def _ffn_post(x, yg, w_tok, shared, mod3, ln_g, ln_b):
    n = x.shape[0]
    tm = POST_TM
    row = lambda i: _mod_row_of_tile(i, tm)
    tile = pl.BlockSpec((tm, D_MODEL), lambda i: (i, 0))
    vec = pl.BlockSpec((1, D_MODEL), lambda i: (0, 0))
    return pl.pallas_call(
        _ffn_post_kernel,
        grid=(n // tm,),
        in_specs=[tile, pl.BlockSpec((TOP_K, tm, D_PACK), lambda i: (0, i, 0)),
                  pl.BlockSpec((tm, TOP_K), lambda i: (i, 0)), tile,
                  pl.BlockSpec((1, 1, D_MODEL), lambda i: (row(i), 0, 5)), vec, vec],
        out_specs=tile,
        out_shape=jax.ShapeDtypeStruct((n, D_MODEL), F32),
        compiler_params=_cparams("parallel"),
    )(x, yg, w_tok, shared, mod3, ln_g.reshape(1, D_MODEL), ln_b.reshape(1, D_MODEL))


def _moe_routed(h, idx, rank, counts, w1, w3, w2):
    n = h.shape[0]
    padded = (counts + MOE_BLOCK - 1) // MOE_BLOCK * MOE_BLOCK
    pad_end = jnp.cumsum(padded)
    pad_start = pad_end - padded
    experts = jnp.arange(N_EXPERTS, dtype=jnp.int32)
    dest = jnp.sum(jnp.where(idx[:, :, None] == experts, pad_start, 0), axis=-1) + rank
    n_blocks = n * TOP_K // MOE_BLOCK + N_EXPERTS
    rows = n_blocks * MOE_BLOCK
    block_start = jnp.arange(n_blocks, dtype=jnp.int32) * MOE_BLOCK
    block_expert = jnp.minimum(jnp.sum(pad_end[None, :] <= block_start[:, None], axis=1), N_EXPERTS - 1).astype(jnp.int32)
    of_block = block_expert[:, None] == experts
    used = jnp.sum(jnp.where(of_block, counts, 0), axis=1) - (block_start - jnp.sum(jnp.where(of_block, pad_start, 0), axis=1))
    block_valid = jnp.clip(used, 0, MOE_BLOCK).astype(jnp.int32)
    xs = _sc_dispatch(h, dest, rows)
    ys = _grouped_experts(xs, block_expert, block_valid, w1, w3, w2)
    return _sc_gather(ys, dest.reshape(-1)).reshape(TOP_K, n, D_PACK)


def _rms_heads(x, g):
    y = x * lax.rsqrt(jnp.mean(x * x, axis=-1, keepdims=True) + EPS)
    return y * g


def _rope_tables():
    rows = DEC_SEQ // GRID_W
    row = jnp.repeat(jnp.arange(rows), GRID_W).astype(F32)
    colp = jnp.tile(jnp.arange(GRID_W), rows).astype(F32)
    nf = HD_A // 4
    inv_freq = ROPE_THETA ** (-jnp.arange(nf, dtype=F32) / nf)
    ar = row[:, None] * inv_freq[None, :]
    ac = colp[:, None] * inv_freq[None, :]
    cos = jnp.concatenate([jnp.cos(ar), jnp.cos(ar), jnp.cos(ac), jnp.cos(ac)], axis=-1)
    sin = jnp.concatenate([-jnp.sin(ar), jnp.sin(ar), -jnp.sin(ac), jnp.sin(ac)], axis=-1)
    return cos, sin


def _rope(x, cos, sin):
    a, b, c, d = jnp.split(x, 4, axis=-1)
    sw = jnp.concatenate([b, a, d, c], axis=-1)
    return x * cos[None, :, None, :] + sw * sin[None, :, None, :]


def _expand_q(q):
    n = q.shape[0]
    z = jnp.zeros_like(q)
    g0 = jnp.concatenate([q, z], axis=-1)
    g1 = jnp.concatenate([z, q], axis=-1)
    sel = (jnp.arange(N_HEADS_A) // GROUP_A)[None, :, None]
    return jnp.where(sel == 0, g0, g1).reshape(n, N_HEADS_A * LANES)


_ATT_HEAD_ORDER = [g * GROUP_A + j for j in range(GROUP_A) for g in range(N_KV_A)]


def _even_layer(x, mod3, j, w_in, w_out, q_norm, k_norm, log_decay, gn_g,
                cache_k, cache_v, state_ret):
    (p,) = _modulated_proj(x, mod3, 0, w_in.astype(BF16), [EVEN_IN], [F32])
    q = _rms_heads(p[:, :A_Q].reshape(N_TOK, N_HEADS_A, HD_A), q_norm)
    k = _rms_heads(p[:, A_Q:A_Q + A_KV].reshape(N_TOK, N_KV_A, HD_A), k_norm)
    v = p[:, A_Q + A_KV:A_Q + 2 * A_KV].reshape(N_TOK, N_KV_A, HD_A)
    k_p = k[:N_P].reshape(BATCH, SEQ, N_KV_A, HD_A)
    v_p = v[:N_P].reshape(BATCH, SEQ, N_KV_A, HD_A)
    scale = HD_A ** -0.5
    qp = _expand_q(q[:N_P] * scale).astype(BF16)
    ktp = jnp.swapaxes(k_p.reshape(BATCH, SEQ, A_KV), 1, 2).astype(BF16)
    o_p = _attention(qp, ktp, v_p.reshape(BATCH, SEQ, A_KV).astype(BF16), BATCH, SEQ)
    cos, sin = _rope_tables()
    qs = _rope(q[N_P:].reshape(DEC_BATCH, DEC_SEQ, N_HEADS_A, HD_A), cos, sin)
    ks = _rope(k[N_P:].reshape(DEC_BATCH, DEC_SEQ, N_KV_A, HD_A), cos, sin)
    k_all = jnp.concatenate([ks, cache_k], axis=1).reshape(DEC_BATCH, DEC_SEQ + PAST_LEN, A_KV)
    v_all = jnp.concatenate([v[N_P:].reshape(DEC_BATCH, DEC_SEQ, N_KV_A, HD_A), cache_v], axis=1)
    v_all = v_all.reshape(DEC_BATCH, DEC_SEQ + PAST_LEN, A_KV)
    qs = _expand_q(qs.reshape(N_S, N_HEADS_A, HD_A) * scale).astype(BF16)
    o_s = _attention(qs, jnp.swapaxes(k_all, 1, 2).astype(BF16), v_all.astype(BF16), DEC_BATCH, DEC_SEQ)
    o_attn = jnp.concatenate([o_p, o_s], axis=0)
    lg_rows = jnp.broadcast_to(log_decay.reshape(2 * H_B, 1), (2 * H_B, LANES))
    zeros_s = jnp.zeros((BATCH, 2, H_B, DK_B, DV_B), F32)
    r_p, s_p = _retention(p, 0, BATCH, SEQ, lg_rows, zeros_s, gn_g)
    r_s, _ = _retention(p, N_P // DEC_SEQ, DEC_BATCH, DEC_SEQ, lg_rows, state_ret, gn_g)
    o_ret = jnp.concatenate([r_p, r_s], axis=0)
    perm = np.concatenate([np.arange(h * HD_A, (h + 1) * HD_A) for h in _ATT_HEAD_ORDER])
    w_o = jnp.concatenate([w_out[:A_Q][perm], w_out[A_Q:]], axis=0).astype(BF16)
    return [o_attn, o_ret], w_o, k_p, v_p, s_p


def _conv_silu(x, w, batch, t):
    xb = x.reshape(batch, t, x.shape[-1])
    xp = jnp.pad(xb, ((0, 0), ((CONV_K - 1) // 2, CONV_K // 2), (0, 0)))
    y = sum(xp[:, i:i + t] * w[i][None, None, :] for i in range(CONV_K))
    return jax.nn.silu(y).reshape(batch * t, x.shape[-1])


def _l2_heads(x):
    xh = x.reshape(x.shape[0], H_C, DK_C)
    return (xh * lax.rsqrt(jnp.sum(xh * xh, axis=-1, keepdims=True) + EPS)).reshape(x.shape)


def _chunk_cumsum(g, batch, t):
    gc = g.reshape(batch, t // GC, GC, 2, H_C)
    f = jnp.cumsum(gc[:, :, :, 0], axis=2)
    b = jnp.cumsum(gc[:, :, ::-1, 1], axis=2)[:, :, ::-1]
    return jnp.stack([f, b], axis=3)


def _odd_layer(x, mod3, j, w_in, conv_w, a_log, dt_bias, norm_g, w_out, state_gdn):
    w_main = w_in[:, :2 * C_QK + 2 * C_V].astype(BF16)
    w_ab = jnp.pad(w_in[:, 2 * C_QK + 2 * C_V:], ((0, 0), (0, LANES - 4 * H_C))).astype(BF16)
    w_cat = jnp.concatenate([w_main, w_ab], axis=1)
    pm, pab = _modulated_proj(x, mod3, 0, w_cat, [2 * C_QK + 2 * C_V, LANES], [F32, F32])
    qkv = pm[:, :2 * C_QK + C_V]
    z = pm[:, 2 * C_QK + C_V:]
    qkv = jnp.concatenate([_conv_silu(qkv[:N_P], conv_w, BATCH, SEQ),
                           _conv_silu(qkv[N_P:], conv_w, DEC_BATCH, DEC_SEQ)], axis=0)
    q = _l2_heads(qkv[:, :C_QK])
    k = _l2_heads(qkv[:, C_QK:2 * C_QK])
    v = qkv[:, 2 * C_QK:]
    ab = pab[:, :4 * H_C].reshape(N_TOK, 2, 2, H_C)
    beta = jax.nn.sigmoid(ab[:, 0])
    g = -jnp.exp(a_log.astype(F32)) * jax.nn.softplus(ab[:, 1] + dt_bias.astype(F32))

    gcs_p = _chunk_cumsum(g[:N_P], BATCH, SEQ)
    gcs_s = _chunk_cumsum(g[N_P:], DEC_BATCH, DEC_SEQ)
    gcol = jnp.concatenate([gcs_p.reshape(N_P, 2 * H_C), gcs_s.reshape(N_S, 2 * H_C)], axis=0)
    gbeta = jnp.concatenate([beta.reshape(N_TOK, 2 * H_C), gcol], axis=1)
    zeros_s = jnp.zeros((BATCH, 2, H_C, DK_C, DV_C), F32)
    o_p, s_p = _gdn(q, k, v, z, gbeta, gcs_p.transpose(0, 3, 4, 1, 2), 0, BATCH, SEQ, zeros_s, norm_g)
    o_s, _ = _gdn(q, k, v, z, gbeta, gcs_s.transpose(0, 3, 4, 1, 2), N_P // DEC_SEQ, DEC_BATCH, DEC_SEQ,
                  state_gdn, norm_g)
    return [jnp.concatenate([o_p, o_s], axis=0)], w_out.astype(BF16), s_p


def kernel(x_prompt, x_sample, cache_attn_k, cache_attn_v, state_ret, state_gdn, c, c_ctx, mod_w, mod_b, ln_g, ln_b, even_w_in, even_w_out, attn_q_norm, attn_k_norm, ret_log_decay, ret_norm_g, odd_w_in, gdn_conv_w, gdn_a_log, gdn_dt_bias, gdn_norm_g, odd_w_out, router_w, router_bias, expert_w1, expert_w3, expert_w2, shared_w1, shared_w3, shared_w2):
    x = jnp.concatenate([x_prompt.reshape(N_P, D_MODEL), x_sample.reshape(N_S, D_MODEL)], axis=0)
    cvec = jnp.concatenate([c_ctx[None, :], c, jnp.zeros((MOD_ROWS - N_MOD, D_MODEL), F32)], axis=0)
    mod_all = _mod_vectors(cvec, mod_w, mod_b)
    new_k, new_v, new_ret, new_gdn = [], [], [], []
    for l in range(DEPTH):
        j = l // 2
        mod3 = mod_all[l].reshape(MOD_ROWS, 1, 6 * D_MODEL)
        if l % 2 == 0:
            a_list, w_o, k_p, v_p, s_p = _even_layer(
                x, mod3, j, even_w_in[j], even_w_out[j], attn_q_norm[j], attn_k_norm[j],
                ret_log_decay[j], ret_norm_g[j], cache_attn_k[:, j], cache_attn_v[:, j], state_ret[:, j])
            new_k.append(k_p)
            new_v.append(v_p)
            new_ret.append(s_p)
        else:
            a_list, w_o, s_p = _odd_layer(
                x, mod3, j, odd_w_in[j], gdn_conv_w[j], gdn_a_log[j], gdn_dt_bias[j], gdn_norm_g[j],
                odd_w_out[j], state_gdn[:, j])
            new_gdn.append(s_p)
        x = _outproj_ln(a_list, w_o, x, mod3, 2, ln_g[l, 0], ln_b[l, 0])
        rw = jnp.pad(router_w[l], ((0, 0), (0, LANES - N_EXPERTS)))
        ws13 = jnp.concatenate([shared_w1[l], shared_w3[l]], axis=1).astype(BF16)
        h, logits, shared = _ffn_pre(x, mod3, rw, ws13, shared_w2[l].astype(BF16))
        idx, w, rank, counts = _route(logits, router_bias[l])
        yg = _moe_routed(h, idx, rank, counts, expert_w1[l], expert_w3[l], expert_w2[l])
        x = _ffn_post(x, yg, w.T, shared, mod3, ln_g[l, 1], ln_b[l, 1])
    return (x[:N_P].reshape(BATCH, SEQ, D_MODEL), x[N_P:].reshape(DEC_BATCH, DEC_SEQ, D_MODEL),
            jnp.stack(new_k, axis=1), jnp.stack(new_v, axis=1),
            jnp.stack(new_ret, axis=1), jnp.stack(new_gdn, axis=1))
```

```python
import functools
import math

import jax
import jax.numpy as jnp
from jax import lax
from jax.experimental import pallas as pl
from jax.experimental.pallas import tpu as pltpu
from jax.experimental.pallas import tpu_sc as plsc

D_MODEL = 1024
BATCH = 16
SEQ = 256
DEPTH = 4
DEC_BATCH = 4
DEC_SEQ = 4096
PAST_LEN = 512
GRID_W = 64
N_HEADS_A = 8
N_KV_A = 2
HD_A = 64
ROPE_THETA = 10000.0
H_B = 4
DK_B = 64
DV_B = 128
RET_CHUNK = 128
H_C = 8
DK_C = 128
DV_C = 128
CONV_K = 5
GDN_CHUNK = 64
N_EXPERTS = 64
TOP_K = 8
N_GROUPS = 8
TOPK_GROUPS = 4
D_EXPERT = 256
D_SHARED = 256
ROUTED_SCALE = 2.5
MOE_BLOCK = 256
A_Q = N_HEADS_A * HD_A
A_KV = N_KV_A * HD_A
B_QK = H_B * DK_B
B_V = H_B * DV_B
EVEN_IN = A_Q + 2 * A_KV + 2 * B_QK + 2 * B_V
C_QK = H_C * DK_C
C_V = H_C * DV_C
DEEPNORM_ALPHA = (2 * DEPTH) ** 0.25
EPS = 1e-6

N_P = BATCH * SEQ
N_S = DEC_BATCH * DEC_SEQ
N_TOK = N_P + N_S
N_MOD = 1 + DEC_BATCH
MOD_ROWS = 8

LANES = 128
VMEM_LIMIT = 56 * 1024 * 1024

F32 = jnp.float32
BF16 = jnp.bfloat16


def _cparams(*sem):
    return pltpu.CompilerParams(dimension_semantics=sem, vmem_limit_bytes=VMEM_LIMIT)


def _bdot(a, b):
    return jnp.dot(a.astype(BF16), b.astype(BF16), preferred_element_type=F32)


def _bdot_t(a, b):
    return lax.dot_general(a.astype(BF16), b.astype(BF16), (((0,), (0,)), ((), ())),
                           preferred_element_type=F32)


def _bdot_nt(a, b):
    return lax.dot_general(a.astype(BF16), b.astype(BF16), (((1,), (1,)), ((), ())),
                           preferred_element_type=F32)


def _split3(a):
    hi = a.astype(BF16)
    r = a - hi.astype(F32)
    mid = r.astype(BF16)
    lo = (r - mid.astype(F32)).astype(BF16)
    return hi, mid, lo


def _dot_hp(a, b):
    a0, a1, a2 = _split3(a)
    b0, b1, b2 = _split3(b)
    d = lambda x, y: jnp.dot(x, y, preferred_element_type=F32)
    small = d(a0, b2) + d(a2, b0) + d(a1, b1)
    return (d(a0, b1) + d(a1, b0)) + small + d(a0, b0)


def _silu(x):
    return x * (1.0 / (1.0 + jnp.exp(-x)))


HI16 = 0xFFFF0000
D_PACK = D_MODEL // 2


def _pack_bf16_pairs(x):
    c = x.shape[1] // 2
    lo = pltpu.bitcast(x[:, :c].astype(BF16).astype(F32), jnp.uint32) >> 16
    hi = pltpu.bitcast(x[:, c:].astype(BF16).astype(F32), jnp.uint32) & jnp.uint32(HI16)
    return pltpu.bitcast(lo | hi, jnp.int32)


def _unpack_bf16_pairs(p):
    u = pltpu.bitcast(p, jnp.uint32)
    return jnp.concatenate([pltpu.bitcast(u << 16, F32), pltpu.bitcast(u & jnp.uint32(HI16), F32)], axis=1)


def _mod_row_of_tile(i, tile):
    tiles_p = N_P // tile
    tiles_per_b = DEC_SEQ // tile
    return jnp.where(i < tiles_p, 0, 1 + (i - tiles_p) // tiles_per_b)


MOD_TN = 1536


def _mod_kernel(c_ref, w_ref, b_ref, o_ref):
    a = _silu(c_ref[...])
    o_ref[0] = _bdot(a, w_ref[0]) + b_ref[0]


def _mod_vectors(cvec, mod_w, mod_b):
    n6 = 6 * D_MODEL
    return pl.pallas_call(
        _mod_kernel,
        grid=(DEPTH, n6 // MOD_TN),
        in_specs=[pl.BlockSpec((MOD_ROWS, D_MODEL), lambda l, j: (0, 0)),
                  pl.BlockSpec((1, D_MODEL, MOD_TN), lambda l, j: (l, 0, j)),
                  pl.BlockSpec((1, 1, MOD_TN), lambda l, j: (l, 0, j))],
        out_specs=pl.BlockSpec((1, MOD_ROWS, MOD_TN), lambda l, j: (l, 0, j)),
        out_shape=jax.ShapeDtypeStruct((DEPTH, MOD_ROWS, n6), F32),
        compiler_params=_cparams("parallel", "parallel"),
    )(cvec, mod_w, mod_b.reshape(DEPTH, 1, n6))


PROJ_TM = 256


def _proj_kernel(x_ref, shift_ref, scale_ref, w_ref, *o_refs, widths):
    h = (x_ref[...] * (1.0 + scale_ref[0]) + shift_ref[0]).astype(BF16)
    off = 0
    for o_ref, wd in zip(o_refs, widths):
        o_ref[...] = jnp.dot(h, w_ref[:, off:off + wd], preferred_element_type=F32).astype(o_ref.dtype)
        off += wd


def _modulated_proj(x, mod3, shift_blk, w_bf16, widths, dtypes):
    n = x.shape[0]
    tm = PROJ_TM
    row = lambda i: _mod_row_of_tile(i, tm)
    return pl.pallas_call(
        functools.partial(_proj_kernel, widths=tuple(widths)),
        grid=(n // tm,),
        in_specs=[pl.BlockSpec((tm, D_MODEL), lambda i: (i, 0)),
                  pl.BlockSpec((1, 1, D_MODEL), lambda i: (row(i), 0, shift_blk)),
                  pl.BlockSpec((1, 1, D_MODEL), lambda i: (row(i), 0, shift_blk + 1)),
                  pl.BlockSpec((D_MODEL, sum(widths)), lambda i: (0, 0))],
        out_specs=[pl.BlockSpec((tm, wd), lambda i: (i, 0)) for wd in widths],
        out_shape=[jax.ShapeDtypeStruct((n, wd), dt) for wd, dt in zip(widths, dtypes)],
        compiler_params=_cparams("parallel"),
    )(x, mod3, mod3, w_bf16)


OUT_TM = 512


def _layer_norm_rows(r, g, b):
    mu = jnp.mean(r, axis=-1, keepdims=True)
    d = r - mu
    var = jnp.mean(d * d, axis=-1, keepdims=True)
    return d * lax.rsqrt(var + EPS) * g + b


def _outproj_kernel(*refs, n_a):
    a_refs = refs[:n_a]
    w_ref, x_ref, gate_ref, g_ref, b_ref, o_ref = refs[n_a:]
    off = 0
    acc = None
    for a_ref in a_refs:
        wd = a_ref.shape[1]
        part = jnp.dot(a_ref[...].astype(BF16), w_ref[off:off + wd, :], preferred_element_type=F32)
        acc = part if acc is None else acc + part
        off += wd
    r = DEEPNORM_ALPHA * x_ref[...] + gate_ref[0] * acc
    o_ref[...] = _layer_norm_rows(r, g_ref[...], b_ref[...])


def _outproj_ln(a_list, w_bf16, x, mod3, gate_blk, ln_g, ln_b):
    n = x.shape[0]
    tm = OUT_TM
    row = lambda i: _mod_row_of_tile(i, tm)
    kdim = w_bf16.shape[0]
    return pl.pallas_call(
        functools.partial(_outproj_kernel, n_a=len(a_list)),
        grid=(n // tm,),
        in_specs=[pl.BlockSpec((tm, a.shape[1]), lambda i: (i, 0)) for a in a_list] + [
            pl.BlockSpec((kdim, D_MODEL), lambda i: (0, 0)),
            pl.BlockSpec((tm, D_MODEL), lambda i: (i, 0)),
            pl.BlockSpec((1, 1, D_MODEL), lambda i: (row(i), 0, gate_blk)),
            pl.BlockSpec((1, D_MODEL), lambda i: (0, 0)),
            pl.BlockSpec((1, D_MODEL), lambda i: (0, 0))],
        out_specs=pl.BlockSpec((tm, D_MODEL), lambda i: (i, 0)),
        out_shape=jax.ShapeDtypeStruct((n, D_MODEL), F32),
        compiler_params=_cparams("parallel"),
    )(*a_list, w_bf16, x, mod3, ln_g.reshape(1, D_MODEL), ln_b.reshape(1, D_MODEL))


PREP_TM = 256
ROPE_SEG = HD_A // 4


def _head_rms(x, gain, ones_bd):
    parts = _split3(x * x)
    ss = functools.reduce(lambda a, b: a + b,
                          [jnp.dot(p, ones_bd, preferred_element_type=F32) for p in reversed(parts)])
    return x * lax.rsqrt(ss * (1.0 / HD_A) + EPS) * gain


def _rope_lanes(x, cos, sin):
    lane = lax.broadcasted_iota(jnp.int32, (1, LANES), 1)
    first = (lane % (2 * ROPE_SEG)) < ROPE_SEG
    cols = []
    for b in range(x.shape[1] // LANES):
        xb = x[:, b * LANES:(b + 1) * LANES]
        partner = jnp.where(first, pltpu.roll(xb, LANES - ROPE_SEG, axis=1), pltpu.roll(xb, ROPE_SEG, axis=1))
        cols.append(xb * cos + partner * sin)
    return cols[0] if len(cols) == 1 else jnp.concatenate(cols, axis=1)


def _attn_prep_kernel(q_ref, kv_ref, cos_ref, sin_ref, qg_ref, kg_ref, bd_ref, qo_ref, ko_ref):
    cos = cos_ref[...]
    sin = sin_ref[...]
    q = _rope_lanes(_head_rms(q_ref[...], qg_ref[...], bd_ref[...]), cos, sin)
    qo_ref[...] = (q * (HD_A ** -0.5)).astype(BF16)
    k = _head_rms(kv_ref[:, :A_KV], kg_ref[...], bd_ref[:A_KV, :A_KV])
    ko_ref[...] = _rope_lanes(k, cos, sin)


def _rope_tables():
    rows = DEC_SEQ // GRID_W
    row = jnp.repeat(jnp.arange(rows), GRID_W).astype(F32)
    colp = jnp.tile(jnp.arange(GRID_W), rows).astype(F32)
    inv_freq = ROPE_THETA ** (-jnp.arange(ROPE_SEG, dtype=F32) / ROPE_SEG)
    ar = row[:, None] * inv_freq[None, :]
    ac = colp[:, None] * inv_freq[None, :]
    cos = jnp.concatenate([jnp.cos(ar), jnp.cos(ar), jnp.cos(ac), jnp.cos(ac)], axis=-1)
    sin = jnp.concatenate([-jnp.sin(ar), jnp.sin(ar), -jnp.sin(ac), jnp.sin(ac)], axis=-1)
    cos = jnp.concatenate([jnp.ones((PREP_TM, HD_A), F32), cos], axis=0)
    sin = jnp.concatenate([jnp.zeros((PREP_TM, HD_A), F32), sin], axis=0)
    return jnp.tile(cos, (1, LANES // HD_A)), jnp.tile(sin, (1, LANES // HD_A))


def _attn_prep(p, q_norm, k_norm):
    tm = PREP_TM
    tiles_p = N_P // tm
    tiles_seq = DEC_SEQ // tm
    tab = lambda i: (jnp.where(i < tiles_p, 0, 1 + (i - tiles_p) % tiles_seq), 0)
    cos, sin = _rope_tables()
    head_id = jnp.arange(A_Q) // HD_A
    ones_bd = (head_id[:, None] == head_id[None, :]).astype(BF16)
    return pl.pallas_call(
        _attn_prep_kernel,
        grid=(N_TOK // tm,),
        in_specs=[pl.BlockSpec((tm, A_Q), lambda i: (i, 0)),
                  pl.BlockSpec((tm, 2 * A_KV), lambda i: (i, A_Q // (2 * A_KV))),
                  pl.BlockSpec((tm, LANES), tab), pl.BlockSpec((tm, LANES), tab),
                  pl.BlockSpec((1, A_Q), lambda i: (0, 0)), pl.BlockSpec((1, A_KV), lambda i: (0, 0)),
                  pl.BlockSpec((A_Q, A_Q), lambda i: (0, 0))],
        out_specs=[pl.BlockSpec((tm, A_Q), lambda i: (i, 0)), pl.BlockSpec((tm, A_KV), lambda i: (i, 0))],
        out_shape=[jax.ShapeDtypeStruct((N_TOK, A_Q), BF16), jax.ShapeDtypeStruct((N_TOK, A_KV), F32)],
        compiler_params=_cparams("parallel"),
    )(p, p, cos, sin, jnp.tile(q_norm, N_HEADS_A).reshape(1, A_Q), jnp.tile(k_norm, N_KV_A).reshape(1, A_KV),
      ones_bd)


ATT_TQ = 256
GROUP_A = N_HEADS_A // N_KV_A


def _attn_kernel(q_ref, kt_ref, kts_ref, v_ref, vs_ref, o_ref):
    low = lax.broadcasted_iota(jnp.int32, (1, LANES), 1) < HD_A
    for j in range(N_HEADS_A // 2):
        qb = q_ref[:, j * LANES:(j + 1) * LANES]
        outs = []
        for half in range(2):
            kv_head = (2 * j + half) // GROUP_A
            qh = jnp.where(low if half == 0 else jnp.logical_not(low), qb, jnp.zeros_like(qb))
            kt, v = (kt_ref[0], v_ref[0]) if half == kv_head else (kts_ref[0], vs_ref[0])
            s = jnp.dot(qh, kt, preferred_element_type=F32)
            m = jnp.max(s, axis=-1, keepdims=True)
            p = jnp.exp(s - m)
            l = jnp.sum(p, axis=-1, keepdims=True)
            pv = jnp.dot(p.astype(BF16), v, preferred_element_type=F32)
            outs.append(pv * (1.0 / l))
        o_ref[:, j * LANES:(j + 1) * LANES] = jnp.where(low, outs[0], outs[1]).astype(o_ref.dtype)


def _attention(q, k, v, row_blk0, batch, t):
    tk = v.shape[1]
    tq = min(ATT_TQ, t)
    nq = t // tq
    swap = lambda x: jnp.roll(x, HD_A, axis=-1)
    kt = jnp.swapaxes(k, 1, 2).astype(BF16)
    kts = jnp.swapaxes(swap(k), 1, 2).astype(BF16)
    kspec = pl.BlockSpec((1, LANES, tk), lambda b, i: (b, 0, 0))
    vspec = pl.BlockSpec((1, tk, LANES), lambda b, i: (b, 0, 0))
    return pl.pallas_call(
        _attn_kernel,
        grid=(batch, nq),
        in_specs=[pl.BlockSpec((tq, A_Q), lambda b, i: ((row_blk0 + b) * nq + i, 0)), kspec, kspec, vspec, vspec],
        out_specs=pl.BlockSpec((tq, A_Q), lambda b, i: (b * nq + i, 0)),
        out_shape=jax.ShapeDtypeStruct((batch * t, A_Q), BF16),
        compiler_params=_cparams("parallel", "parallel"),
    )(q, kt, kts, v.astype(BF16), swap(v).astype(BF16))


RC = RET_CHUNK


def _ret_kernel(q_ref, k_ref, v_ref, g_ref, lg_ref, s0_ref, gn_ref, o_ref, s_out_ref,
                ob_ref, st_ref, *, t):
    nc = t // RC
    hp = pl.program_id(1)
    ii = lax.broadcasted_iota(jnp.int32, (RC, RC), 0).astype(F32)
    jj = lax.broadcasted_iota(jnp.int32, (RC, RC), 1).astype(F32)
    col_i = lax.broadcasted_iota(jnp.int32, (RC, 1), 0).astype(F32)
    lane = lax.broadcasted_iota(jnp.int32, (1, LANES), 1)
    masks = [(lane >= hh * DK_B) & (lane < (hh + 1) * DK_B) for hh in range(2)]

    consts = []
    for d in range(2):
        for hh in range(2):
            lg = lg_ref[pl.ds(d * H_B + hp * 2 + hh, 1), :][:, :1]
            if d == 0:
                diff = ii - jj
                qe, ke = col_i + 1.0, (RC - 1.0) - col_i
            else:
                diff = jj - ii
                qe, ke = RC - col_i, col_i
            intra = jnp.where(diff >= 0, jnp.exp(jnp.maximum(diff, 0.0) * lg), 0.0)
            consts.append((intra, jnp.exp(qe * lg), jnp.exp(ke * lg), jnp.exp(RC * lg)))
            s0 = s0_ref[0, d, hh]
            z = jnp.zeros((DK_B, DV_B), F32)
            st_ref[d * 2 + hh] = jnp.concatenate([s0, z] if hh == 0 else [z, s0], axis=0)

    def chunk(c, d):
        r0 = pl.multiple_of(c * RC, RC)
        qc = q_ref[pl.ds(r0, RC), :]
        kc = k_ref[pl.ds(r0, RC), :] * (DK_B ** -0.5)
        outs = []
        for hh in range(2):
            intra, q_dec, k_dec, c_dec = consts[d * 2 + hh]
            vc = v_ref[pl.ds(r0, RC), hh * DV_B:(hh + 1) * DV_B]
            qh = jnp.where(masks[hh], qc, 0.0)
            kh = jnp.where(masks[hh], kc, 0.0)
            s = st_ref[d * 2 + hh]
            scores = _bdot_nt(qh, kh) * intra
            o = _bdot(scores, vc) + _bdot(qh * q_dec, s)
            st_ref[d * 2 + hh] = s * c_dec + _bdot_t(kh * k_dec, vc)
            outs.append(o)
        return r0, jnp.concatenate(outs, axis=1)

    def body(c, carry):
        r0, of = chunk(c, 0)
        o_ref[pl.ds(r0, RC), :] = of
        r1, ob = chunk(nc - 1 - c, 1)
        ob_ref[pl.ds(r1, RC), :] = ob
        return carry

    lax.fori_loop(0, nc, body, 0)

    for d in range(2):
        for hh in range(2):
            s_out_ref[0, d, hh] = st_ref[d * 2 + hh][hh * DK_B:(hh + 1) * DK_B, :]

    def finish(c, carry):
        r0 = pl.multiple_of(c * RC, RC)
        o = o_ref[pl.ds(r0, RC), :] + ob_ref[pl.ds(r0, RC), :]
        gate = _silu(g_ref[pl.ds(r0, RC), :])
        ys = []
        for hh in range(2):
            oh = o[:, hh * DV_B:(hh + 1) * DV_B]
            mu = jnp.mean(oh, axis=-1, keepdims=True)
            dlt = oh - mu
            var = jnp.mean(dlt * dlt, axis=-1, keepdims=True)
            ys.append(dlt * lax.rsqrt(var + EPS))
        o_ref[pl.ds(r0, RC), :] = jnp.concatenate(ys, axis=1) * gn_ref[...] * gate
        return carry

    lax.fori_loop(0, nc, finish, 0)


def _retention(p, row_blk0, batch, t, lg_rows, s0, gn_g):
    qb, kb, vb, gb = (A_Q + 2 * A_KV) // LANES, (A_Q + 2 * A_KV + B_QK) // LANES, \
        (A_Q + 2 * A_KV + 2 * B_QK) // (2 * DV_B), (A_Q + 2 * A_KV + 2 * B_QK + B_V) // (2 * DV_B)
    return pl.pallas_call(
        functools.partial(_ret_kernel, t=t),
        grid=(batch, H_B // 2),
        in_specs=[pl.BlockSpec((t, LANES), lambda b, h: (row_blk0 + b, qb + h)),
                  pl.BlockSpec((t, LANES), lambda b, h: (row_blk0 + b, kb + h)),
                  pl.BlockSpec((t, 2 * DV_B), lambda b, h: (row_blk0 + b, vb + h)),
                  pl.BlockSpec((t, 2 * DV_B), lambda b, h: (row_blk0 + b, gb + h)),
                  pl.BlockSpec((2 * H_B, LANES), lambda b, h: (0, 0)),
                  pl.BlockSpec((1, 2, 2, DK_B, DV_B), lambda b, h: (b, 0, h, 0, 0)),
                  pl.BlockSpec((1, 2 * DV_B), lambda b, h: (0, h))],
        out_specs=[pl.BlockSpec((t, 2 * DV_B), lambda b, h: (b, h)),
                   pl.BlockSpec((1, 2, 2, DK_B, DV_B), lambda b, h: (b, 0, h, 0, 0))],
        out_shape=[jax.ShapeDtypeStruct((batch * t, B_V), F32),
                   jax.ShapeDtypeStruct((batch, 2, H_B, DK_B, DV_B), F32)],
        scratch_shapes=[pltpu.VMEM((t, 2 * DV_B), F32), pltpu.VMEM((4, LANES, DV_B), F32)],
        compiler_params=_cparams("parallel", "parallel"),
    )(p, p, p, p, lg_rows, s0, gn_g.reshape(1, B_V))


GC = 256


def _unit_tri_inverses(mats):
    ii = lax.broadcasted_iota(jnp.int32, (GC, GC), 0)
    jj = lax.broadcasted_iota(jnp.int32, (GC, GC), 1)
    block_dist = ii ^ jj
    eye = (ii == jj).astype(F32)
    ms = [jnp.where((block_dist >> 3) == 0, -a, 0.0) for a in mats]
    invs = [eye + m for m in ms]
    for _ in range(2):
        ms = [_bdot(m, m) for m in ms]
        invs = [inv + _bdot(inv, m) for inv, m in zip(invs, ms)]
    for shift in range(3, int(math.log2(GC))):
        ls = [jnp.where((block_dist >> shift) == 1, a, 0.0) for a in mats]
        ts = [_bdot(inv, l) for inv, l in zip(invs, ls)]
        invs = [inv - _bdot(t, inv) for inv, t in zip(invs, ts)]
    return invs


def _gdn_kernel(q_ref, k_ref, v_ref, z_ref, gb_ref, gr_ref, s0_ref, ng_ref, o_ref, s_out_ref,
                ob_ref, st_ref, wq_s, u_s, a_s, kg_s, *, t):
    nc = t // GC
    h = pl.program_id(1)
    ii = lax.broadcasted_iota(jnp.int32, (GC, GC), 0)
    jj = lax.broadcasted_iota(jnp.int32, (GC, GC), 1)
    incl = [ii >= jj, jj >= ii]
    strict = [ii > jj, jj > ii]
    lane32 = lax.broadcasted_iota(jnp.int32, (1, 4 * H_C), 1)
    for d in range(2):
        st_ref[d] = s0_ref[0, d, 0]

    def col(x, idx):
        return jnp.sum(jnp.where(lane32 == idx, x, 0.0), axis=1, keepdims=True)

    def gate_last(c, d):
        grow = gr_ref[0, d, 0, pl.ds(c, 1), :]
        return grow, (grow[:, GC - 1:GC] if d == 0 else grow[:, 0:1])

    def prep(c, carry):
        rows = pl.ds(pl.multiple_of(c * GC, GC), GC)
        qc = q_ref[rows, :] * (DK_C ** -0.5)
        kc = k_ref[rows, :]
        vc = v_ref[rows, :]
        gb = gb_ref[rows, :]
        kk = _bdot_nt(kc, kc)
        qk = _bdot_nt(qc, kc)
        r2 = pl.multiple_of(c * 2 * GC, 2 * GC)
        mats, rhs = [], []
        for d in range(2):
            beta = col(gb, d * H_C + h)
            gcol = col(gb, 2 * H_C + d * H_C + h)
            grow, glast = gate_last(c, d)
            decay = jnp.exp(jnp.where(incl[d], gcol - grow, -jnp.inf))
            eg = jnp.exp(gcol)
            mats.append(jnp.where(strict[d], kk * beta * decay, 0.0))
            rhs.append(jnp.concatenate([kc * (beta * eg), vc * beta], axis=1).astype(BF16))
            wq_s[d, pl.ds(r2 + GC, GC), :] = (qc * eg).astype(BF16)
            a_s[d, rows, :] = (qk * decay).astype(BF16)
            kg_s[d, rows, :] = (kc * jnp.exp(glast - gcol)).astype(BF16)
        for d, tinv in enumerate(_unit_tri_inverses(mats)):
            wu = jnp.dot(tinv.astype(BF16), rhs[d], preferred_element_type=F32)
            wq_s[d, pl.ds(r2, GC), :] = wu[:, :DK_C].astype(BF16)
            u_s[d, rows, :] = wu[:, DK_C:]
        return carry

    lax.fori_loop(0, nc, prep, 0)

    def step(c, carry):
        ccs = [c, nc - 1 - c]
        rows = [pl.ds(pl.multiple_of(cc * GC, GC), GC) for cc in ccs]
        ss = [st_ref[d] for d in range(2)]
        sbs = [s.astype(BF16) for s in ss]
        wss = [jnp.dot(wq_s[d, pl.ds(pl.multiple_of(ccs[d] * 2 * GC, 2 * GC), 2 * GC), :], sbs[d],
                       preferred_element_type=F32) for d in range(2)]
        vnbs = [(u_s[d, rows[d], :] - wss[d][:GC]).astype(BF16) for d in range(2)]
        os_ = [wss[d][GC:] + jnp.dot(a_s[d, rows[d], :], vnbs[d], preferred_element_type=F32) for d in range(2)]
        kvs = [_bdot_t(kg_s[d, rows[d], :], vnbs[d]) for d in range(2)]
        for d in range(2):
            _, glast = gate_last(ccs[d], d)
            st_ref[d] = ss[d] * jnp.exp(glast) + kvs[d]
        o_ref[rows[0], :] = os_[0]
        ob_ref[rows[1], :] = os_[1]
        return carry

    lax.fori_loop(0, nc, step, 0)

    for d in range(2):
        s_out_ref[0, d, 0] = st_ref[d]

    def finish(c, carry):
        r0 = pl.multiple_of(c * GC, GC)
        o = o_ref[pl.ds(r0, GC), :] + ob_ref[pl.ds(r0, GC), :]
        y = o * lax.rsqrt(jnp.mean(o * o, axis=-1, keepdims=True) + EPS) * ng_ref[...]
        o_ref[pl.ds(r0, GC), :] = y * _silu(z_ref[pl.ds(r0, GC), :])
        return carry

    lax.fori_loop(0, nc, finish, 0)


def _gdn_prep_kernel(x_ref, w_ref, o_ref, *, t):
    x = x_ref[...]
    row = lax.broadcasted_iota(jnp.int32, (t, 1), 0)
    centre = (CONV_K - 1) // 2
    y = x * w_ref[centre:centre + 1, :]
    for i in range(CONV_K):
        s = i - centre
        if s == 0:
            continue
        shifted = pltpu.roll(x, (-s) % t, axis=0)
        inside = jnp.logical_and(row + s >= 0, row + s < t)
        y = y + jnp.where(inside, shifted, 0.0) * w_ref[i:i + 1, :]
    y = _silu(y)
    inv_norm = lax.rsqrt(jnp.sum(y * y, axis=-1, keepdims=True) + EPS)
    o_ref[...] = y * jnp.where(pl.program_id(1) < 2 * H_C, inv_norm, 1.0)


def _gdn_prep(pm, conv_w, row_blk0, batch, t):
    n_col = (2 * C_QK + C_V) // LANES
    return pl.pallas_call(
        functools.partial(_gdn_prep_kernel, t=t),
        grid=(batch, n_col),
        in_specs=[pl.BlockSpec((t, LANES), lambda b, j: (row_blk0 + b, j)),
                  pl.BlockSpec((CONV_K, LANES), lambda b, j: (0, j))],
        out_specs=pl.BlockSpec((t, LANES), lambda b, j: (b, j)),
        out_shape=jax.ShapeDtypeStruct((batch * t, 2 * C_QK + C_V), F32),
        compiler_params=_cparams("parallel", "parallel"),
    )(pm, conv_w)


def _gdn(qkv, pm, gbeta, grow, row_blk0, batch, t, s0, norm_g):
    nc = t // GC
    return pl.pallas_call(
        functools.partial(_gdn_kernel, t=t),
        grid=(batch, H_C),
        in_specs=[pl.BlockSpec((t, DK_C), lambda b, h: (b, h)),
                  pl.BlockSpec((t, DK_C), lambda b, h: (b, H_C + h)),
                  pl.BlockSpec((t, DV_C), lambda b, h: (b, 2 * H_C + h)),
                  pl.BlockSpec((t, DV_C), lambda b, h: (row_blk0 + b, 3 * H_C + h)),
                  pl.BlockSpec((t, 4 * H_C), lambda b, h: (row_blk0 + b, 0)),
                  pl.BlockSpec((1, 2, 1, nc, GC), lambda b, h: (b, 0, h, 0, 0)),
                  pl.BlockSpec((1, 2, 1, DK_C, DV_C), lambda b, h: (b, 0, h, 0, 0)),
                  pl.BlockSpec((1, DV_C), lambda b, h: (0, 0))],
        out_specs=[pl.BlockSpec((t, DV_C), lambda b, h: (b, h)),
                   pl.BlockSpec((1, 2, 1, DK_C, DV_C), lambda b, h: (b, 0, h, 0, 0))],
        out_shape=[jax.ShapeDtypeStruct((batch * t, C_V), F32),
                   jax.ShapeDtypeStruct((batch, 2, H_C, DK_C, DV_C), F32)],
        scratch_shapes=[pltpu.VMEM((t, DV_C), F32), pltpu.VMEM((2, DK_C, DV_C), F32),
                        pltpu.VMEM((2, 2 * t, DK_C), BF16), pltpu.VMEM((2, t, DV_C), F32),
                        pltpu.VMEM((2, t, GC), BF16), pltpu.VMEM((2, t, DK_C), BF16)],
        compiler_params=_cparams("parallel", "parallel"),
    )(qkv, qkv, qkv, pm, gbeta, grow, s0, norm_g.reshape(1, DV_C))


FFN_TM = 512


def _ffn_pre_kernel(x_ref, shift_ref, scale_ref, rw_ref, w13_ref, w2_ref, h_ref, logit_ref, sh_ref):
    h = x_ref[...] * (1.0 + scale_ref[0]) + shift_ref[0]
    hb = h.astype(BF16)
    h_ref[...] = _pack_bf16_pairs(h)
    logit_ref[...] = _dot_hp(h, rw_ref[...])
    up = jnp.dot(hb, w13_ref[...], preferred_element_type=F32)
    hid = _silu(up[:, :D_SHARED]) * up[:, D_SHARED:]
    sh_ref[...] = jnp.dot(hid.astype(BF16), w2_ref[...], preferred_element_type=F32)


def _ffn_pre(x, mod3, router_w_pad, ws13, ws2):
    n = x.shape[0]
    tm = FFN_TM
    row = lambda i: _mod_row_of_tile(i, tm)
    return pl.pallas_call(
        _ffn_pre_kernel,
        grid=(n // tm,),
        in_specs=[pl.BlockSpec((tm, D_MODEL), lambda i: (i, 0)),
                  pl.BlockSpec((1, 1, D_MODEL), lambda i: (row(i), 0, 3)),
                  pl.BlockSpec((1, 1, D_MODEL), lambda i: (row(i), 0, 4)),
                  pl.BlockSpec((D_MODEL, LANES), lambda i: (0, 0)),
                  pl.BlockSpec((D_MODEL, 2 * D_SHARED), lambda i: (0, 0)),
                  pl.BlockSpec((D_SHARED, D_MODEL), lambda i: (0, 0))],
        out_specs=[pl.BlockSpec((tm, D_PACK), lambda i: (i, 0)),
                   pl.BlockSpec((tm, LANES), lambda i: (i, 0)),
                   pl.BlockSpec((tm, D_MODEL), lambda i: (i, 0))],
        out_shape=[jax.ShapeDtypeStruct((n, D_PACK), jnp.int32),
                   jax.ShapeDtypeStruct((n, LANES), F32),
                   jax.ShapeDtypeStruct((n, D_MODEL), F32)],
        compiler_params=_cparams("parallel"),
    )(x, mod3, mod3, router_w_pad, ws13, ws2)


ROUTE_T = 512
PER_GROUP = N_EXPERTS // N_GROUPS
NEG_INF = float("-inf")


def _first_max(x, iota, n):
    m = jnp.max(x, axis=0, keepdims=True)
    first = jnp.min(jnp.where(x == m, iota, n), axis=0, keepdims=True)
    return m, iota == first


def _route_kernel(lg_ref, bias_ref, tri_ref, idx_ref, w_ref, rank_ref, cnt_ref, carry_ref):
    @pl.when(pl.program_id(0) == 0)
    def _():
        carry_ref[...] = jnp.zeros_like(carry_ref)

    t = lg_ref.shape[0]
    logits = jnp.transpose(lg_ref[...])[:N_EXPERTS]
    scores = 1.0 / (1.0 + jnp.exp(-logits))
    sel = scores + bias_ref[...]
    sub_g = lax.broadcasted_iota(jnp.int32, (PER_GROUP, t), 0)
    sub_e = lax.broadcasted_iota(jnp.int32, (N_EXPERTS, t), 0)
    grp_rows = []
    for g in range(N_GROUPS):
        x = sel[g * PER_GROUP:(g + 1) * PER_GROUP]
        m1, hit = _first_max(x, sub_g, PER_GROUP)
        m2 = jnp.max(jnp.where(hit, NEG_INF, x), axis=0, keepdims=True)
        grp_rows.append(m1 + m2)
    cur = jnp.concatenate(grp_rows, axis=0)
    sub_grp = lax.broadcasted_iota(jnp.int32, (N_GROUPS, t), 0)
    grp_on = jnp.zeros((N_GROUPS, t), F32)
    for _ in range(TOPK_GROUPS):
        _, hit = _first_max(cur, sub_grp, N_GROUPS)
        grp_on = jnp.where(hit, 1.0, grp_on)
        cur = jnp.where(hit, NEG_INF, cur)
    exp_on = jnp.concatenate([jnp.broadcast_to(grp_on[g:g + 1], (PER_GROUP, t)) for g in range(N_GROUPS)], axis=0)
    cur = jnp.where(exp_on > 0.0, sel, NEG_INF)
    hits, idx_rows = [], []
    for _ in range(TOP_K):
        _, hit = _first_max(cur, sub_e, N_EXPERTS)
        hits.append(hit)
        idx_rows.append(jnp.sum(jnp.where(hit, sub_e, 0), axis=0, keepdims=True))
        cur = jnp.where(hit, NEG_INF, cur)
    w_rows = [jnp.sum(jnp.where(hit, scores, 0.0), axis=0, keepdims=True) for hit in hits]
    total = functools.reduce(lambda a, b: a + b, w_rows)
    w_ref[...] = jnp.concatenate([w / total * ROUTED_SCALE for w in w_rows], axis=0)
    idx_ref[...] = jnp.concatenate(idx_rows, axis=0)
    chosen = functools.reduce(lambda a, b: a + b, [jnp.where(hit, 1.0, 0.0) for hit in hits])
    incl = jnp.dot(chosen.astype(BF16), tri_ref[...], preferred_element_type=F32)
    before = incl - chosen + carry_ref[:, :1]
    rank_ref[...] = jnp.concatenate(
        [jnp.sum(jnp.where(hit, before, 0.0), axis=0, keepdims=True) for hit in hits], axis=0).astype(jnp.int32)
    carry_ref[...] = carry_ref[...] + incl[:, t - 1:t]
    cnt_ref[...] = carry_ref[...]


def _route(logits, router_bias):
    n = logits.shape[0]
    t = ROUTE_T
    tri = (jnp.arange(t)[:, None] <= jnp.arange(t)[None, :]).astype(BF16)
    bias_b = jnp.broadcast_to(router_bias.astype(F32)[:, None], (N_EXPERTS, t))
    slot = pl.BlockSpec((TOP_K, t), lambda i: (0, i))
    idx, w, rank, cnt = pl.pallas_call(
        _route_kernel,
        grid=(n // t,),
        in_specs=[pl.BlockSpec((t, LANES), lambda i: (i, 0)),
                  pl.BlockSpec((N_EXPERTS, t), lambda i: (0, 0)),
                  pl.BlockSpec((t, t), lambda i: (0, 0))],
        out_specs=[slot, slot, slot, pl.BlockSpec((N_EXPERTS, LANES), lambda i: (0, 0))],
        out_shape=[jax.ShapeDtypeStruct((TOP_K, n), jnp.int32), jax.ShapeDtypeStruct((TOP_K, n), F32),
                   jax.ShapeDtypeStruct((TOP_K, n), jnp.int32), jax.ShapeDtypeStruct((N_EXPERTS, LANES), F32)],
        scratch_shapes=[pltpu.VMEM((N_EXPERTS, LANES), F32)],
        compiler_params=_cparams("arbitrary"),
    )(logits, bias_b, tri)
    return idx, w, rank, cnt[:, 0].astype(jnp.int32)


def _experts_kernel(be_ref, bv_ref, x_ref, w1_ref, w3_ref, w2_ref, o_ref, w1b, w3b, w2b):
    i = pl.program_id(0)
    changed = jnp.logical_or(i == 0, be_ref[i] != be_ref[jnp.maximum(i - 1, 0)])

    @pl.when(changed)
    def _():
        w1b[...] = w1_ref[0].astype(BF16)
        w3b[...] = w3_ref[0].astype(BF16)
        w2b[...] = w2_ref[0].astype(BF16)

    live = lax.broadcasted_iota(jnp.int32, (MOE_BLOCK, 1), 0) < bv_ref[i]
    xb = _unpack_bf16_pairs(jnp.where(live, x_ref[...], 0)).astype(BF16)
    hid = _silu(jnp.dot(xb, w1b[...], preferred_element_type=F32)) * \
        jnp.dot(xb, w3b[...], preferred_element_type=F32)
    o_ref[...] = _pack_bf16_pairs(jnp.dot(hid.astype(BF16), w2b[...], preferred_element_type=F32))


def _grouped_experts(xs, block_expert, block_valid, w1, w3, w2):
    rows = xs.shape[0]
    n_blocks = rows // MOE_BLOCK
    grid_spec = pltpu.PrefetchScalarGridSpec(
        num_scalar_prefetch=2,
        grid=(n_blocks,),
        in_specs=[pl.BlockSpec((MOE_BLOCK, D_PACK), lambda i, be, bv: (i, 0)),
                  pl.BlockSpec((1, D_MODEL, D_EXPERT), lambda i, be, bv: (be[i], 0, 0)),
                  pl.BlockSpec((1, D_MODEL, D_EXPERT), lambda i, be, bv: (be[i], 0, 0)),
                  pl.BlockSpec((1, D_EXPERT, D_MODEL), lambda i, be, bv: (be[i], 0, 0))],
        out_specs=pl.BlockSpec((MOE_BLOCK, D_PACK), lambda i, be, bv: (i, 0)),
        scratch_shapes=[pltpu.VMEM((D_MODEL, D_EXPERT), BF16), pltpu.VMEM((D_MODEL, D_EXPERT), BF16),
                        pltpu.VMEM((D_EXPERT, D_MODEL), BF16)])
    return pl.pallas_call(
        _experts_kernel,
        grid_spec=grid_spec,
        out_shape=jax.ShapeDtypeStruct((rows, D_PACK), jnp.int32),
        compiler_params=_cparams("arbitrary"),
    )(block_expert, block_valid, xs, w1, w3, w2)


SC_CORES = 2
SC_SUBCORES = 16
SC_WORKERS = SC_CORES * SC_SUBCORES
SC_ROWS = 64


def _sc_mesh():
    return plsc.VectorSubcoreMesh(core_axis_name="c", subcore_axis_name="s",
                                  num_cores=SC_CORES, num_subcores=SC_SUBCORES)


def _sc_dispatch(table, dest, rows):
    n, d = table.shape
    kk = dest.shape[0]
    per_worker = n // SC_WORKERS
    n_chunks = per_worker // SC_ROWS
    assert per_worker * SC_WORKERS == n and n_chunks * SC_ROWS == per_worker
    idx = dest.reshape(kk, SC_WORKERS, n_chunks, SC_ROWS).transpose(1, 2, 0, 3).reshape(
        SC_WORKERS, n_chunks * kk, SC_ROWS)

    @functools.partial(
        pl.kernel, mesh=_sc_mesh(),
        out_type=jax.ShapeDtypeStruct((rows, d), table.dtype),
        scratch_types=[pltpu.VMEM((n_chunks * kk, SC_ROWS), jnp.int32),
                       pltpu.VMEM((SC_ROWS, d), table.dtype),
                       pltpu.SemaphoreType.DMA])
    def dispatch(table_hbm, idx_hbm, out_hbm, idx_v, rows_v, sem):
        wid = lax.axis_index("s") * SC_CORES + lax.axis_index("c")
        base = wid * per_worker
        pltpu.sync_copy(idx_hbm.at[wid], idx_v)

        @pl.loop(0, n_chunks)
        def _(j):
            pltpu.sync_copy(table_hbm.at[pl.ds(base + j * SC_ROWS, SC_ROWS)], rows_v)
            copies = [pltpu.async_copy(rows_v, out_hbm.at[idx_v.at[j * kk + k]], sem) for k in range(kk)]
            for cp in copies:
                cp.wait()

    return dispatch(table, idx)


def _sc_gather(table, idx):
    b = idx.shape[0]
    d = table.shape[1]
    per_worker = b // SC_WORKERS
    n_chunks = per_worker // SC_ROWS
    assert per_worker * SC_WORKERS == b and n_chunks * SC_ROWS == per_worker

    @functools.partial(
        pl.kernel, mesh=_sc_mesh(),
        out_type=jax.ShapeDtypeStruct((b, d), table.dtype),
        scratch_types=[pltpu.VMEM((n_chunks, SC_ROWS), jnp.int32),
                       pltpu.VMEM((SC_ROWS, d), table.dtype),
                       pltpu.SemaphoreType.DMA])
    def gather(table_hbm, idx_hbm, out_hbm, idx_v, rows_v, sem):
        wid = lax.axis_index("s") * SC_CORES + lax.axis_index("c")
        base = wid * per_worker
        pltpu.sync_copy(idx_hbm.at[wid], idx_v)

        @pl.loop(0, n_chunks)
        def _(j):
            pltpu.async_copy(table_hbm.at[idx_v.at[j]], rows_v, sem).wait()
            pltpu.sync_copy(rows_v, out_hbm.at[pl.ds(base + j * SC_ROWS, SC_ROWS)])

    return gather(table, idx.reshape(SC_WORKERS, n_chunks, SC_ROWS))


POST_TM = 256


def _ffn_post_kernel(x_ref, yg_ref, w_ref, s_ref, gate_ref, g_ref, b_ref, o_ref):
    w = w_ref[...]
    routed = _unpack_bf16_pairs(yg_ref[0]) * w[:, 0:1]
    for k in range(1, TOP_K):
        routed = routed + _unpack_bf16_pairs(yg_ref[k]) * w[:, k:k + 1]
    r = DEEPNORM_ALPHA * x_ref[...] + gate_ref[0] * (routed + s_ref[...])
    o_ref[...] = _layer_norm_rows(r, g_ref[...], b_ref[...])


def _ffn_post(x, yg, w_tok, shared, mod3, ln_g, ln_b):
    n = x.shape[0]
    tm = POST_TM
    row = lambda i: _mod_row_of_tile(i, tm)
    tile = pl.BlockSpec((tm, D_MODEL), lambda i: (i, 0))
    vec = pl.BlockSpec((1, D_MODEL), lambda i: (0, 0))
    return pl.pallas_call(
        _ffn_post_kernel,
        grid=(n // tm,),
        in_specs=[tile, pl.BlockSpec((TOP_K, tm, D_PACK), lambda i: (0, i, 0)),
                  pl.BlockSpec((tm, TOP_K), lambda i: (i, 0)), tile,
                  pl.BlockSpec((1, 1, D_MODEL), lambda i: (row(i), 0, 5)), vec, vec],
        out_specs=tile,
        out_shape=jax.ShapeDtypeStruct((n, D_MODEL), F32),
        compiler_params=_cparams("parallel"),
    )(x, yg, w_tok, shared, mod3, ln_g.reshape(1, D_MODEL), ln_b.reshape(1, D_MODEL))


def _moe_routed(h, idx, rank, counts, w1, w3, w2):
    n = h.shape[0]
    padded = (counts + MOE_BLOCK - 1) // MOE_BLOCK * MOE_BLOCK
    pad_end = jnp.cumsum(padded)
    pad_start = pad_end - padded
    experts = jnp.arange(N_EXPERTS, dtype=jnp.int32)
    dest = jnp.sum(jnp.where(idx[:, :, None] == experts, pad_start, 0), axis=-1) + rank
    n_blocks = n * TOP_K // MOE_BLOCK + N_EXPERTS
    rows = n_blocks * MOE_BLOCK
    block_start = jnp.arange(n_blocks, dtype=jnp.int32) * MOE_BLOCK
    block_expert = jnp.minimum(jnp.sum(pad_end[None, :] <= block_start[:, None], axis=1), N_EXPERTS - 1).astype(jnp.int32)
    of_block = block_expert[:, None] == experts
    used = jnp.sum(jnp.where(of_block, counts, 0), axis=1) - (block_start - jnp.sum(jnp.where(of_block, pad_start, 0), axis=1))
    block_valid = jnp.clip(used, 0, MOE_BLOCK).astype(jnp.int32)
    xs = _sc_dispatch(h, dest, rows)
    ys = _grouped_experts(xs, block_expert, block_valid, w1, w3, w2)
    return _sc_gather(ys, dest.reshape(-1)).reshape(TOP_K, n, D_PACK)


def _even_layer(x, mod3, j, w_in, w_out, q_norm, k_norm, log_decay, gn_g,
                cache_k, cache_v, state_ret):
    (p,) = _modulated_proj(x, mod3, 0, w_in.astype(BF16), [EVEN_IN], [F32])
    q, k = _attn_prep(p, q_norm, k_norm)
    v = p[:, A_Q + A_KV:A_Q + 2 * A_KV]
    k_p = k[:N_P].reshape(BATCH, SEQ, A_KV)
    v_p = v[:N_P].reshape(BATCH, SEQ, A_KV)
    o_p = _attention(q, k_p, v_p, 0, BATCH, SEQ)
    k_all = jnp.concatenate([k[N_P:].reshape(DEC_BATCH, DEC_SEQ, A_KV),
                             cache_k.reshape(DEC_BATCH, PAST_LEN, A_KV)], axis=1)
    v_all = jnp.concatenate([v[N_P:].reshape(DEC_BATCH, DEC_SEQ, A_KV),
                             cache_v.reshape(DEC_BATCH, PAST_LEN, A_KV)], axis=1)
    o_s = _attention(q, k_all, v_all, N_P // DEC_SEQ, DEC_BATCH, DEC_SEQ)
    o_attn = jnp.concatenate([o_p, o_s], axis=0)
    lg_rows = jnp.broadcast_to(log_decay.reshape(2 * H_B, 1), (2 * H_B, LANES))
    zeros_s = jnp.zeros((BATCH, 2, H_B, DK_B, DV_B), F32)
    r_p, s_p = _retention(p, 0, BATCH, SEQ, lg_rows, zeros_s, gn_g)
    r_s, _ = _retention(p, N_P // DEC_SEQ, DEC_BATCH, DEC_SEQ, lg_rows, state_ret, gn_g)
    o_ret = jnp.concatenate([r_p, r_s], axis=0)
    return ([o_attn, o_ret], w_out.astype(BF16), k_p.reshape(BATCH, SEQ, N_KV_A, HD_A),
            v_p.reshape(BATCH, SEQ, N_KV_A, HD_A), s_p)


def _chunk_cumsum(g, batch, t):
    gc = g.reshape(batch, t // GC, GC, 2, H_C)
    f = jnp.cumsum(gc[:, :, :, 0], axis=2)
    b = jnp.cumsum(gc[:, :, ::-1, 1], axis=2)[:, :, ::-1]
    return jnp.stack([f, b], axis=3)


def _odd_layer(x, mod3, j, w_in, conv_w, a_log, dt_bias, norm_g, w_out, state_gdn):
    w_main = w_in[:, :2 * C_QK + 2 * C_V].astype(BF16)
    w_ab = jnp.pad(w_in[:, 2 * C_QK + 2 * C_V:], ((0, 0), (0, LANES - 4 * H_C))).astype(BF16)
    w_cat = jnp.concatenate([w_main, w_ab], axis=1)
    pm, pab = _modulated_proj(x, mod3, 0, w_cat, [2 * C_QK + 2 * C_V, LANES], [F32, F32])
    qkv_p = _gdn_prep(pm, conv_w, 0, BATCH, SEQ)
    qkv_s = _gdn_prep(pm, conv_w, N_P // DEC_SEQ, DEC_BATCH, DEC_SEQ)
    ab = pab[:, :4 * H_C].reshape(N_TOK, 2, 2, H_C)
    beta = jax.nn.sigmoid(ab[:, 0])
    g = -jnp.exp(a_log.astype(F32)) * jax.nn.softplus(ab[:, 1] + dt_bias.astype(F32))

    gcs_p = _chunk_cumsum(g[:N_P], BATCH, SEQ)
    gcs_s = _chunk_cumsum(g[N_P:], DEC_BATCH, DEC_SEQ)
    gcol = jnp.concatenate([gcs_p.reshape(N_P, 2 * H_C), gcs_s.reshape(N_S, 2 * H_C)], axis=0)
    gbeta = jnp.concatenate([beta.reshape(N_TOK, 2 * H_C), gcol], axis=1)
    zeros_s = jnp.zeros((BATCH, 2, H_C, DK_C, DV_C), F32)
    o_p, s_p = _gdn(qkv_p, pm, gbeta, gcs_p.transpose(0, 3, 4, 1, 2), 0, BATCH, SEQ, zeros_s, norm_g)
    o_s, _ = _gdn(qkv_s, pm, gbeta, gcs_s.transpose(0, 3, 4, 1, 2), N_P // DEC_SEQ, DEC_BATCH, DEC_SEQ,
                  state_gdn, norm_g)
    return [jnp.concatenate([o_p, o_s], axis=0)], w_out.astype(BF16), s_p


def kernel(x_prompt, x_sample, cache_attn_k, cache_attn_v, state_ret, state_gdn, c, c_ctx, mod_w, mod_b, ln_g, ln_b, even_w_in, even_w_out, attn_q_norm, attn_k_norm, ret_log_decay, ret_norm_g, odd_w_in, gdn_conv_w, gdn_a_log, gdn_dt_bias, gdn_norm_g, odd_w_out, router_w, router_bias, expert_w1, expert_w3, expert_w2, shared_w1, shared_w3, shared_w2):
    x = jnp.concatenate([x_prompt.reshape(N_P, D_MODEL), x_sample.reshape(N_S, D_MODEL)], axis=0)
    cvec = jnp.concatenate([c_ctx[None, :], c, jnp.zeros((MOD_ROWS - N_MOD, D_MODEL), F32)], axis=0)
    mod_all = _mod_vectors(cvec, mod_w, mod_b)
    new_k, new_v, new_ret, new_gdn = [], [], [], []
    for l in range(DEPTH):
        j = l // 2
        mod3 = mod_all[l].reshape(MOD_ROWS, 1, 6 * D_MODEL)
        if l % 2 == 0:
            a_list, w_o, k_p, v_p, s_p = _even_layer(
                x, mod3, j, even_w_in[j], even_w_out[j], attn_q_norm[j], attn_k_norm[j],
                ret_log_decay[j], ret_norm_g[j], cache_attn_k[:, j], cache_attn_v[:, j], state_ret[:, j])
            new_k.append(k_p)
            new_v.append(v_p)
            new_ret.append(s_p)
        else:
            a_list, w_o, s_p = _odd_layer(
                x, mod3, j, odd_w_in[j], gdn_conv_w[j], gdn_a_log[j], gdn_dt_bias[j], gdn_norm_g[j],
                odd_w_out[j], state_gdn[:, j])
            new_gdn.append(s_p)
        x = _outproj_ln(a_list, w_o, x, mod3, 2, ln_g[l, 0], ln_b[l, 0])
        rw = jnp.pad(router_w[l], ((0, 0), (0, LANES - N_EXPERTS)))
        ws13 = jnp.concatenate([shared_w1[l], shared_w3[l]], axis=1).astype(BF16)
        h, logits, shared = _ffn_pre(x, mod3, rw, ws13, shared_w2[l].astype(BF16))
        idx, w, rank, counts = _route(logits, router_bias[l])
        yg = _moe_routed(h, idx, rank, counts, expert_w1[l], expert_w3[l], expert_w2[l])
        x = _ffn_post(x, yg, w.T, shared, mod3, ln_g[l, 1], ln_b[l, 1])
    return (x[:N_P].reshape(BATCH, SEQ, D_MODEL), x[N_P:].reshape(DEC_BATCH, DEC_SEQ, D_MODEL),
            jnp.stack(new_k, axis=1), jnp.stack(new_v, axis=1),
            jnp.stack(new_ret, axis=1), jnp.stack(new_gdn, axis=1))
```

```python
import functools
import math

import jax
import jax.numpy as jnp
from jax import lax
from jax.experimental import pallas as pl
from jax.experimental.pallas import tpu as pltpu
from jax.experimental.pallas import tpu_sc as plsc

D_MODEL = 1024
BATCH = 16
SEQ = 256
DEPTH = 4
DEC_BATCH = 4
DEC_SEQ = 4096
PAST_LEN = 512
GRID_W = 64
N_HEADS_A = 8
N_KV_A = 2
HD_A = 64
ROPE_THETA = 10000.0
H_B = 4
DK_B = 64
DV_B = 128
RET_CHUNK = 128
H_C = 8
DK_C = 128
DV_C = 128
CONV_K = 5
GDN_CHUNK = 64
N_EXPERTS = 64
TOP_K = 8
N_GROUPS = 8
TOPK_GROUPS = 4
D_EXPERT = 256
D_SHARED = 256
ROUTED_SCALE = 2.5
MOE_BLOCK = 256
A_Q = N_HEADS_A * HD_A
A_KV = N_KV_A * HD_A
B_QK = H_B * DK_B
B_V = H_B * DV_B
EVEN_IN = A_Q + 2 * A_KV + 2 * B_QK + 2 * B_V
C_QK = H_C * DK_C
C_V = H_C * DV_C
DEEPNORM_ALPHA = (2 * DEPTH) ** 0.25
EPS = 1e-6

N_P = BATCH * SEQ
N_S = DEC_BATCH * DEC_SEQ
N_TOK = N_P + N_S
N_MOD = 1 + DEC_BATCH
MOD_ROWS = 8

LANES = 128
VMEM_LIMIT = 56 * 1024 * 1024

F32 = jnp.float32
BF16 = jnp.bfloat16


def _cparams(*sem):
    return pltpu.CompilerParams(dimension_semantics=sem, vmem_limit_bytes=VMEM_LIMIT)


def _bdot(a, b):
    return jnp.dot(a.astype(BF16), b.astype(BF16), preferred_element_type=F32)


def _bdot_t(a, b):
    return lax.dot_general(a.astype(BF16), b.astype(BF16), (((0,), (0,)), ((), ())),
                           preferred_element_type=F32)


def _bdot_nt(a, b):
    return lax.dot_general(a.astype(BF16), b.astype(BF16), (((1,), (1,)), ((), ())),
                           preferred_element_type=F32)


def _split3(a):
    hi = a.astype(BF16)
    r = a - hi.astype(F32)
    mid = r.astype(BF16)
    lo = (r - mid.astype(F32)).astype(BF16)
    return hi, mid, lo


def _dot_hp(a, b):
    a0, a1, a2 = _split3(a)
    b0, b1, b2 = _split3(b)
    d = lambda x, y: jnp.dot(x, y, preferred_element_type=F32)
    small = d(a0, b2) + d(a2, b0) + d(a1, b1)
    return (d(a0, b1) + d(a1, b0)) + small + d(a0, b0)


def _silu(x):
    return x * (1.0 / (1.0 + jnp.exp(-x)))


HI16 = 0xFFFF0000
D_PACK = D_MODEL // 2


def _pack_bf16_pairs(x):
    c = x.shape[1] // 2
    lo = pltpu.bitcast(x[:, :c].astype(BF16).astype(F32), jnp.uint32) >> 16
    hi = pltpu.bitcast(x[:, c:].astype(BF16).astype(F32), jnp.uint32) & jnp.uint32(HI16)
    return pltpu.bitcast(lo | hi, jnp.int32)


def _unpack_bf16_pairs(p):
    u = pltpu.bitcast(p, jnp.uint32)
    return jnp.concatenate([pltpu.bitcast(u << 16, F32), pltpu.bitcast(u & jnp.uint32(HI16), F32)], axis=1)


def _mod_row_of_tile(i, tile):
    tiles_p = N_P // tile
    tiles_per_b = DEC_SEQ // tile
    return jnp.where(i < tiles_p, 0, 1 + (i - tiles_p) // tiles_per_b)


MOD_TN = 1536


def _mod_kernel(c_ref, w_ref, b_ref, o_ref):
    a = _silu(c_ref[...])
    o_ref[0] = _bdot(a, w_ref[0]) + b_ref[0]


def _mod_vectors(cvec, mod_w, mod_b):
    n6 = 6 * D_MODEL
    return pl.pallas_call(
        _mod_kernel,
        grid=(DEPTH, n6 // MOD_TN),
        in_specs=[pl.BlockSpec((MOD_ROWS, D_MODEL), lambda l, j: (0, 0)),
                  pl.BlockSpec((1, D_MODEL, MOD_TN), lambda l, j: (l, 0, j)),
                  pl.BlockSpec((1, 1, MOD_TN), lambda l, j: (l, 0, j))],
        out_specs=pl.BlockSpec((1, MOD_ROWS, MOD_TN), lambda l, j: (l, 0, j)),
        out_shape=jax.ShapeDtypeStruct((DEPTH, MOD_ROWS, n6), F32),
        compiler_params=_cparams("parallel", "parallel"),
    )(cvec, mod_w, mod_b.reshape(DEPTH, 1, n6))


PROJ_TM = 256


def _proj_kernel(x_ref, shift_ref, scale_ref, w_ref, *o_refs, widths):
    h = (x_ref[...] * (1.0 + scale_ref[0]) + shift_ref[0]).astype(BF16)
    off = 0
    for o_ref, wd in zip(o_refs, widths):
        o_ref[...] = jnp.dot(h, w_ref[:, off:off + wd], preferred_element_type=F32).astype(o_ref.dtype)
        off += wd


def _modulated_proj(x, mod3, shift_blk, w_bf16, widths, dtypes):
    n = x.shape[0]
    tm = PROJ_TM
    row = lambda i: _mod_row_of_tile(i, tm)
    return pl.pallas_call(
        functools.partial(_proj_kernel, widths=tuple(widths)),
        grid=(n // tm,),
        in_specs=[pl.BlockSpec((tm, D_MODEL), lambda i: (i, 0)),
                  pl.BlockSpec((1, 1, D_MODEL), lambda i: (row(i), 0, shift_blk)),
                  pl.BlockSpec((1, 1, D_MODEL), lambda i: (row(i), 0, shift_blk + 1)),
                  pl.BlockSpec((D_MODEL, sum(widths)), lambda i: (0, 0))],
        out_specs=[pl.BlockSpec((tm, wd), lambda i: (i, 0)) for wd in widths],
        out_shape=[jax.ShapeDtypeStruct((n, wd), dt) for wd, dt in zip(widths, dtypes)],
        compiler_params=_cparams("parallel"),
    )(x, mod3, mod3, w_bf16)


OUT_TM = 512


def _layer_norm_rows(r, g, b):
    mu = jnp.mean(r, axis=-1, keepdims=True)
    d = r - mu
    var = jnp.mean(d * d, axis=-1, keepdims=True)
    return d * lax.rsqrt(var + EPS) * g + b


def _outproj_kernel(*refs, n_a):
    a_refs = refs[:n_a]
    w_ref, x_ref, gate_ref, g_ref, b_ref, o_ref = refs[n_a:]
    off = 0
    acc = None
    for a_ref in a_refs:
        wd = a_ref.shape[1]
        part = jnp.dot(a_ref[...].astype(BF16), w_ref[off:off + wd, :], preferred_element_type=F32)
        acc = part if acc is None else acc + part
        off += wd
    r = DEEPNORM_ALPHA * x_ref[...] + gate_ref[0] * acc
    o_ref[...] = _layer_norm_rows(r, g_ref[...], b_ref[...])


def _outproj_ln(a_list, w_bf16, x, mod3, gate_blk, ln_g, ln_b):
    n = x.shape[0]
    tm = OUT_TM
    row = lambda i: _mod_row_of_tile(i, tm)
    kdim = w_bf16.shape[0]
    return pl.pallas_call(
        functools.partial(_outproj_kernel, n_a=len(a_list)),
        grid=(n // tm,),
        in_specs=[pl.BlockSpec((tm, a.shape[1]), lambda i: (i, 0)) for a in a_list] + [
            pl.BlockSpec((kdim, D_MODEL), lambda i: (0, 0)),
            pl.BlockSpec((tm, D_MODEL), lambda i: (i, 0)),
            pl.BlockSpec((1, 1, D_MODEL), lambda i: (row(i), 0, gate_blk)),
            pl.BlockSpec((1, D_MODEL), lambda i: (0, 0)),
            pl.BlockSpec((1, D_MODEL), lambda i: (0, 0))],
        out_specs=pl.BlockSpec((tm, D_MODEL), lambda i: (i, 0)),
        out_shape=jax.ShapeDtypeStruct((n, D_MODEL), F32),
        compiler_params=_cparams("parallel"),
    )(*a_list, w_bf16, x, mod3, ln_g.reshape(1, D_MODEL), ln_b.reshape(1, D_MODEL))


PREP_TM = 256
ROPE_SEG = HD_A // 4


def _head_rms(x, gain, ones_bd):
    parts = _split3(x * x)
    ss = functools.reduce(lambda a, b: a + b,
                          [jnp.dot(p, ones_bd, preferred_element_type=F32) for p in reversed(parts)])
    return x * lax.rsqrt(ss * (1.0 / HD_A) + EPS) * gain


def _rope_lanes(x, cos, sin):
    lane = lax.broadcasted_iota(jnp.int32, (1, LANES), 1)
    first = (lane % (2 * ROPE_SEG)) < ROPE_SEG
    cols = []
    for b in range(x.shape[1] // LANES):
        xb = x[:, b * LANES:(b + 1) * LANES]
        partner = jnp.where(first, pltpu.roll(xb, LANES - ROPE_SEG, axis=1), pltpu.roll(xb, ROPE_SEG, axis=1))
        cols.append(xb * cos + partner * sin)
    return cols[0] if len(cols) == 1 else jnp.concatenate(cols, axis=1)


def _attn_prep_kernel(q_ref, kv_ref, cos_ref, sin_ref, qg_ref, kg_ref, bd_ref, qo_ref, ko_ref):
    cos = cos_ref[...]
    sin = sin_ref[...]
    q = _rope_lanes(_head_rms(q_ref[...], qg_ref[...], bd_ref[...]), cos, sin)
    qo_ref[...] = (q * (HD_A ** -0.5)).astype(BF16)
    k = _head_rms(kv_ref[:, :A_KV], kg_ref[...], bd_ref[:A_KV, :A_KV])
    ko_ref[...] = _rope_lanes(k, cos, sin)


def _rope_tables():
    rows = DEC_SEQ // GRID_W
    row = jnp.repeat(jnp.arange(rows), GRID_W).astype(F32)
    colp = jnp.tile(jnp.arange(GRID_W), rows).astype(F32)
    inv_freq = ROPE_THETA ** (-jnp.arange(ROPE_SEG, dtype=F32) / ROPE_SEG)
    ar = row[:, None] * inv_freq[None, :]
    ac = colp[:, None] * inv_freq[None, :]
    cos = jnp.concatenate([jnp.cos(ar), jnp.cos(ar), jnp.cos(ac), jnp.cos(ac)], axis=-1)
    sin = jnp.concatenate([-jnp.sin(ar), jnp.sin(ar), -jnp.sin(ac), jnp.sin(ac)], axis=-1)
    cos = jnp.concatenate([jnp.ones((PREP_TM, HD_A), F32), cos], axis=0)
    sin = jnp.concatenate([jnp.zeros((PREP_TM, HD_A), F32), sin], axis=0)
    return jnp.tile(cos, (1, LANES // HD_A)), jnp.tile(sin, (1, LANES // HD_A))


def _attn_prep(p, q_norm, k_norm):
    tm = PREP_TM
    tiles_p = N_P // tm
    tiles_seq = DEC_SEQ // tm
    tab = lambda i: (jnp.where(i < tiles_p, 0, 1 + (i - tiles_p) % tiles_seq), 0)
    cos, sin = _rope_tables()
    head_id = jnp.arange(A_Q) // HD_A
    ones_bd = (head_id[:, None] == head_id[None, :]).astype(BF16)
    return pl.pallas_call(
        _attn_prep_kernel,
        grid=(N_TOK // tm,),
        in_specs=[pl.BlockSpec((tm, A_Q), lambda i: (i, 0)),
                  pl.BlockSpec((tm, 2 * A_KV), lambda i: (i, A_Q // (2 * A_KV))),
                  pl.BlockSpec((tm, LANES), tab), pl.BlockSpec((tm, LANES), tab),
                  pl.BlockSpec((1, A_Q), lambda i: (0, 0)), pl.BlockSpec((1, A_KV), lambda i: (0, 0)),
                  pl.BlockSpec((A_Q, A_Q), lambda i: (0, 0))],
        out_specs=[pl.BlockSpec((tm, A_Q), lambda i: (i, 0)), pl.BlockSpec((tm, A_KV), lambda i: (i, 0))],
        out_shape=[jax.ShapeDtypeStruct((N_TOK, A_Q), BF16), jax.ShapeDtypeStruct((N_TOK, A_KV), F32)],
        compiler_params=_cparams("parallel"),
    )(p, p, cos, sin, jnp.tile(q_norm, N_HEADS_A).reshape(1, A_Q), jnp.tile(k_norm, N_KV_A).reshape(1, A_KV),
      ones_bd)


ATT_TQ = 256
GROUP_A = N_HEADS_A // N_KV_A


def _attn_kernel(q_ref, kt_ref, kts_ref, v_ref, vs_ref, o_ref):
    low = lax.broadcasted_iota(jnp.int32, (1, LANES), 1) < HD_A
    for j in range(N_HEADS_A // 2):
        qb = q_ref[:, j * LANES:(j + 1) * LANES]
        outs = []
        for half in range(2):
            kv_head = (2 * j + half) // GROUP_A
            qh = jnp.where(low if half == 0 else jnp.logical_not(low), qb, jnp.zeros_like(qb))
            kt, v = (kt_ref[0], v_ref[0]) if half == kv_head else (kts_ref[0], vs_ref[0])
            s = jnp.dot(qh, kt, preferred_element_type=F32)
            m = jnp.max(s, axis=-1, keepdims=True)
            p = jnp.exp(s - m)
            l = jnp.sum(p, axis=-1, keepdims=True)
            pv = jnp.dot(p.astype(BF16), v, preferred_element_type=F32)
            outs.append(pv * (1.0 / l))
        o_ref[:, j * LANES:(j + 1) * LANES] = jnp.where(low, outs[0], outs[1]).astype(o_ref.dtype)


def _attention(q, k, v, row_blk0, batch, t):
    tk = v.shape[1]
    tq = min(ATT_TQ, t)
    nq = t // tq
    swap = lambda x: jnp.roll(x, HD_A, axis=-1)
    kt = jnp.swapaxes(k, 1, 2).astype(BF16)
    kts = jnp.swapaxes(swap(k), 1, 2).astype(BF16)
    kspec = pl.BlockSpec((1, LANES, tk), lambda b, i: (b, 0, 0))
    vspec = pl.BlockSpec((1, tk, LANES), lambda b, i: (b, 0, 0))
    return pl.pallas_call(
        _attn_kernel,
        grid=(batch, nq),
        in_specs=[pl.BlockSpec((tq, A_Q), lambda b, i: ((row_blk0 + b) * nq + i, 0)), kspec, kspec, vspec, vspec],
        out_specs=pl.BlockSpec((tq, A_Q), lambda b, i: (b * nq + i, 0)),
        out_shape=jax.ShapeDtypeStruct((batch * t, A_Q), BF16),
        compiler_params=_cparams("parallel", "parallel"),
    )(q, kt, kts, v.astype(BF16), swap(v).astype(BF16))


RC = RET_CHUNK


def _ret_kernel(q_ref, k_ref, v_ref, g_ref, lg_ref, s0_ref, gn_ref, o_ref, s_out_ref,
                ob_ref, st_ref, *, t):
    nc = t // RC
    hp = pl.program_id(1)
    ii = lax.broadcasted_iota(jnp.int32, (RC, RC), 0).astype(F32)
    jj = lax.broadcasted_iota(jnp.int32, (RC, RC), 1).astype(F32)
    col_i = lax.broadcasted_iota(jnp.int32, (RC, 1), 0).astype(F32)
    lane = lax.broadcasted_iota(jnp.int32, (1, LANES), 1)
    masks = [(lane >= hh * DK_B) & (lane < (hh + 1) * DK_B) for hh in range(2)]

    consts = []
    for d in range(2):
        for hh in range(2):
            lg = lg_ref[pl.ds(d * H_B + hp * 2 + hh, 1), :][:, :1]
            if d == 0:
                diff = ii - jj
                qe, ke = col_i + 1.0, (RC - 1.0) - col_i
            else:
                diff = jj - ii
                qe, ke = RC - col_i, col_i
            intra = jnp.where(diff >= 0, jnp.exp(jnp.maximum(diff, 0.0) * lg), 0.0)
            consts.append((intra, jnp.exp(qe * lg), jnp.exp(ke * lg), jnp.exp(RC * lg)))
            s0 = s0_ref[0, d, hh]
            z = jnp.zeros((DK_B, DV_B), F32)
            st_ref[d * 2 + hh] = jnp.concatenate([s0, z] if hh == 0 else [z, s0], axis=0)

    def chunk(c, d):
        r0 = pl.multiple_of(c * RC, RC)
        qc = q_ref[pl.ds(r0, RC), :]
        kc = k_ref[pl.ds(r0, RC), :] * (DK_B ** -0.5)
        outs = []
        for hh in range(2):
            intra, q_dec, k_dec, c_dec = consts[d * 2 + hh]
            vc = v_ref[pl.ds(r0, RC), hh * DV_B:(hh + 1) * DV_B]
            qh = jnp.where(masks[hh], qc, 0.0)
            kh = jnp.where(masks[hh], kc, 0.0)
            s = st_ref[d * 2 + hh]
            scores = _bdot_nt(qh, kh) * intra
            o = _bdot(scores, vc) + _bdot(qh * q_dec, s)
            st_ref[d * 2 + hh] = s * c_dec + _bdot_t(kh * k_dec, vc)
            outs.append(o)
        return r0, jnp.concatenate(outs, axis=1)

    def body(c, carry):
        r0, of = chunk(c, 0)
        o_ref[pl.ds(r0, RC), :] = of
        r1, ob = chunk(nc - 1 - c, 1)
        ob_ref[pl.ds(r1, RC), :] = ob
        return carry

    lax.fori_loop(0, nc, body, 0)

    for d in range(2):
        for hh in range(2):
            s_out_ref[0, d, hh] = st_ref[d * 2 + hh][hh * DK_B:(hh + 1) * DK_B, :]

    def finish(c, carry):
        r0 = pl.multiple_of(c * RC, RC)
        o = o_ref[pl.ds(r0, RC), :] + ob_ref[pl.ds(r0, RC), :]
        gate = _silu(g_ref[pl.ds(r0, RC), :])
        ys = []
        for hh in range(2):
            oh = o[:, hh * DV_B:(hh + 1) * DV_B]
            mu = jnp.mean(oh, axis=-1, keepdims=True)
            dlt = oh - mu
            var = jnp.mean(dlt * dlt, axis=-1, keepdims=True)
            ys.append(dlt * lax.rsqrt(var + EPS))
        o_ref[pl.ds(r0, RC), :] = jnp.concatenate(ys, axis=1) * gn_ref[...] * gate
        return carry

    lax.fori_loop(0, nc, finish, 0)


def _retention(p, row_blk0, batch, t, lg_rows, s0, gn_g):
    qb, kb, vb, gb = (A_Q + 2 * A_KV) // LANES, (A_Q + 2 * A_KV + B_QK) // LANES, \
        (A_Q + 2 * A_KV + 2 * B_QK) // (2 * DV_B), (A_Q + 2 * A_KV + 2 * B_QK + B_V) // (2 * DV_B)
    return pl.pallas_call(
        functools.partial(_ret_kernel, t=t),
        grid=(batch, H_B // 2),
        in_specs=[pl.BlockSpec((t, LANES), lambda b, h: (row_blk0 + b, qb + h)),
                  pl.BlockSpec((t, LANES), lambda b, h: (row_blk0 + b, kb + h)),
                  pl.BlockSpec((t, 2 * DV_B), lambda b, h: (row_blk0 + b, vb + h)),
                  pl.BlockSpec((t, 2 * DV_B), lambda b, h: (row_blk0 + b, gb + h)),
                  pl.BlockSpec((2 * H_B, LANES), lambda b, h: (0, 0)),
                  pl.BlockSpec((1, 2, 2, DK_B, DV_B), lambda b, h: (b, 0, h, 0, 0)),
                  pl.BlockSpec((1, 2 * DV_B), lambda b, h: (0, h))],
        out_specs=[pl.BlockSpec((t, 2 * DV_B), lambda b, h: (b, h)),
                   pl.BlockSpec((1, 2, 2, DK_B, DV_B), lambda b, h: (b, 0, h, 0, 0))],
        out_shape=[jax.ShapeDtypeStruct((batch * t, B_V), F32),
                   jax.ShapeDtypeStruct((batch, 2, H_B, DK_B, DV_B), F32)],
        scratch_shapes=[pltpu.VMEM((t, 2 * DV_B), F32), pltpu.VMEM((4, LANES, DV_B), F32)],
        compiler_params=_cparams("parallel", "parallel"),
    )(p, p, p, p, lg_rows, s0, gn_g.reshape(1, B_V))


GC = 256


def _unit_tri_inverses(mats):
    ii = lax.broadcasted_iota(jnp.int32, (GC, GC), 0)
    jj = lax.broadcasted_iota(jnp.int32, (GC, GC), 1)
    block_dist = ii ^ jj
    eye = (ii == jj).astype(F32)
    ms = [jnp.where((block_dist >> 3) == 0, -a, 0.0) for a in mats]
    invs = [eye + m for m in ms]
    for _ in range(2):
        ms = [_bdot(m, m) for m in ms]
        invs = [inv + _bdot(inv, m) for inv, m in zip(invs, ms)]
    for shift in range(3, int(math.log2(GC))):
        ls = [jnp.where((block_dist >> shift) == 1, a, 0.0) for a in mats]
        ts = [_bdot(inv, l) for inv, l in zip(invs, ls)]
        invs = [inv - _bdot(t, inv) for inv, t in zip(invs, ts)]
    return invs


def _gdn_kernel(q_ref, k_ref, v_ref, z_ref, gb_ref, gr_ref, s0_ref, ng_ref, o_ref, s_out_ref,
                ob_ref, st_ref, wq_s, u_s, a_s, kg_s, *, t):
    nc = t // GC
    h = pl.program_id(1)
    ii = lax.broadcasted_iota(jnp.int32, (GC, GC), 0)
    jj = lax.broadcasted_iota(jnp.int32, (GC, GC), 1)
    incl = [ii >= jj, jj >= ii]
    strict = [ii > jj, jj > ii]
    lane32 = lax.broadcasted_iota(jnp.int32, (1, 4 * H_C), 1)
    for d in range(2):
        st_ref[d] = s0_ref[0, d, 0]

    def col(x, idx):
        return jnp.sum(jnp.where(lane32 == idx, x, 0.0), axis=1, keepdims=True)

    def gate_last(c, d):
        grow = gr_ref[0, d, 0, pl.ds(c, 1), :]
        return grow, (grow[:, GC - 1:GC] if d == 0 else grow[:, 0:1])

    def prep(cb, carry):
        mats, rhs, dst = [], [], []
        for ci in range(prep_chunks):
            c = cb * prep_chunks + ci
            rows = pl.ds(pl.multiple_of(c * GC, GC), GC)
            qc = q_ref[rows, :] * (DK_C ** -0.5)
            kc = k_ref[rows, :]
            vc = v_ref[rows, :]
            gb = gb_ref[rows, :]
            kk = _bdot_nt(kc, kc)
            qk = _bdot_nt(qc, kc)
            r2 = pl.multiple_of(c * 2 * GC, 2 * GC)
            for d in range(2):
                beta = col(gb, d * H_C + h)
                gcol = col(gb, 2 * H_C + d * H_C + h)
                grow, glast = gate_last(c, d)
                decay = jnp.exp(jnp.where(incl[d], gcol - grow, -jnp.inf))
                eg = jnp.exp(gcol)
                mats.append(jnp.where(strict[d], kk * beta * decay, 0.0))
                rhs.append(jnp.concatenate([kc * (beta * eg), vc * beta], axis=1).astype(BF16))
                dst.append((d, rows, r2))
                wq_s[d, pl.ds(r2 + GC, GC), :] = (qc * eg).astype(BF16)
                a_s[d, rows, :] = (qk * decay).astype(BF16)
                kg_s[d, rows, :] = (kc * jnp.exp(glast - gcol)).astype(BF16)
        wus = [jnp.dot(tinv.astype(BF16), r, preferred_element_type=F32)
               for tinv, r in zip(_unit_tri_inverses(mats), rhs)]
        for wu, (d, rows, r2) in zip(wus, dst):
            wq_s[d, pl.ds(r2, GC), :] = wu[:, :DK_C].astype(BF16)
            u_s[d, rows, :] = wu[:, DK_C:]
        return carry

    prep_chunks = 2 if nc % 2 == 0 else 1
    lax.fori_loop(0, nc // prep_chunks, prep, 0)

    def step(c, carry):
        ccs = [c, nc - 1 - c]
        rows = [pl.ds(pl.multiple_of(cc * GC, GC), GC) for cc in ccs]
        ss = [st_ref[d] for d in range(2)]
        sbs = [s.astype(BF16) for s in ss]
        wss = [jnp.dot(wq_s[d, pl.ds(pl.multiple_of(ccs[d] * 2 * GC, 2 * GC), 2 * GC), :], sbs[d],
                       preferred_element_type=F32) for d in range(2)]
        vnbs = [(u_s[d, rows[d], :] - wss[d][:GC]).astype(BF16) for d in range(2)]
        os_ = [wss[d][GC:] + jnp.dot(a_s[d, rows[d], :], vnbs[d], preferred_element_type=F32) for d in range(2)]
        kvs = [_bdot_t(kg_s[d, rows[d], :], vnbs[d]) for d in range(2)]
        for d in range(2):
            _, glast = gate_last(ccs[d], d)
            st_ref[d] = ss[d] * jnp.exp(glast) + kvs[d]
        o_ref[rows[0], :] = os_[0]
        ob_ref[rows[1], :] = os_[1]
        return carry

    lax.fori_loop(0, nc, step, 0)

    for d in range(2):
        s_out_ref[0, d, 0] = st_ref[d]

    def finish(c, carry):
        r0 = pl.multiple_of(c * GC, GC)
        o = o_ref[pl.ds(r0, GC), :] + ob_ref[pl.ds(r0, GC), :]
        y = o * lax.rsqrt(jnp.mean(o * o, axis=-1, keepdims=True) + EPS) * ng_ref[...]
        o_ref[pl.ds(r0, GC), :] = y * _silu(z_ref[pl.ds(r0, GC), :])
        return carry

    lax.fori_loop(0, nc, finish, 0)


GDN_PREP_BLOCK_ELEMS = 512 * 1024


def _gdn_prep_kernel(x_ref, w_ref, o_ref, *, t, heads):
    x = x_ref[...]
    row = lax.broadcasted_iota(jnp.int32, (t, 1), 0)
    centre = (CONV_K - 1) // 2
    y = x * w_ref[centre:centre + 1, :]
    for i in range(CONV_K):
        s = i - centre
        if s == 0:
            continue
        shifted = pltpu.roll(x, (-s) % t, axis=0)
        inside = jnp.logical_and(row + s >= 0, row + s < t)
        y = y + jnp.where(inside, shifted, 0.0) * w_ref[i:i + 1, :]
    y = _silu(y)
    for b in range(heads):
        yb = y[:, b * LANES:(b + 1) * LANES]
        inv_norm = lax.rsqrt(jnp.sum(yb * yb, axis=-1, keepdims=True) + EPS)
        is_qk = pl.program_id(1) * heads + b < 2 * H_C
        o_ref[:, b * LANES:(b + 1) * LANES] = yb * jnp.where(is_qk, inv_norm, 1.0)


def _gdn_prep(pm, conv_w, row_blk0, batch, t):
    heads = max(1, min(H_C, GDN_PREP_BLOCK_ELEMS // (t * LANES)))
    n_col = (2 * C_QK + C_V) // (heads * LANES)
    return pl.pallas_call(
        functools.partial(_gdn_prep_kernel, t=t, heads=heads),
        grid=(batch, n_col),
        in_specs=[pl.BlockSpec((t, heads * LANES), lambda b, j: (row_blk0 + b, j)),
                  pl.BlockSpec((CONV_K, heads * LANES), lambda b, j: (0, j))],
        out_specs=pl.BlockSpec((t, heads * LANES), lambda b, j: (b, j)),
        out_shape=jax.ShapeDtypeStruct((batch * t, 2 * C_QK + C_V), F32),
        compiler_params=_cparams("parallel", "parallel"),
    )(pm, conv_w)


def _gdn(qkv, pm, gbeta, grow, row_blk0, batch, t, s0, norm_g):
    nc = t // GC
    return pl.pallas_call(
        functools.partial(_gdn_kernel, t=t),
        grid=(batch, H_C),
        in_specs=[pl.BlockSpec((t, DK_C), lambda b, h: (b, h)),
                  pl.BlockSpec((t, DK_C), lambda b, h: (b, H_C + h)),
                  pl.BlockSpec((t, DV_C), lambda b, h: (b, 2 * H_C + h)),
                  pl.BlockSpec((t, DV_C), lambda b, h: (row_blk0 + b, 3 * H_C + h)),
                  pl.BlockSpec((t, 4 * H_C), lambda b, h: (row_blk0 + b, 0)),
                  pl.BlockSpec((1, 2, 1, nc, GC), lambda b, h: (b, 0, h, 0, 0)),
                  pl.BlockSpec((1, 2, 1, DK_C, DV_C), lambda b, h: (b, 0, h, 0, 0)),
                  pl.BlockSpec((1, DV_C), lambda b, h: (0, 0))],
        out_specs=[pl.BlockSpec((t, DV_C), lambda b, h: (b, h)),
                   pl.BlockSpec((1, 2, 1, DK_C, DV_C), lambda b, h: (b, 0, h, 0, 0))],
        out_shape=[jax.ShapeDtypeStruct((batch * t, C_V), F32),
                   jax.ShapeDtypeStruct((batch, 2, H_C, DK_C, DV_C), F32)],
        scratch_shapes=[pltpu.VMEM((t, DV_C), F32), pltpu.VMEM((2, DK_C, DV_C), F32),
                        pltpu.VMEM((2, 2 * t, DK_C), BF16), pltpu.VMEM((2, t, DV_C), F32),
                        pltpu.VMEM((2, t, GC), BF16), pltpu.VMEM((2, t, DK_C), BF16)],
        compiler_params=_cparams("parallel", "parallel"),
    )(qkv, qkv, qkv, pm, gbeta, grow, s0, norm_g.reshape(1, DV_C))


FFN_TM = 512


def _ffn_pre_kernel(x_ref, shift_ref, scale_ref, rw_ref, w13_ref, w2_ref, h_ref, logit_ref, sh_ref):
    h = x_ref[...] * (1.0 + scale_ref[0]) + shift_ref[0]
    hb = h.astype(BF16)
    h_ref[...] = _pack_bf16_pairs(h)
    logit_ref[...] = _dot_hp(h, rw_ref[...])
    up = jnp.dot(hb, w13_ref[...], preferred_element_type=F32)
    hid = _silu(up[:, :D_SHARED]) * up[:, D_SHARED:]
    sh_ref[...] = jnp.dot(hid.astype(BF16), w2_ref[...], preferred_element_type=F32)


def _ffn_pre(x, mod3, router_w_pad, ws13, ws2):
    n = x.shape[0]
    tm = FFN_TM
    row = lambda i: _mod_row_of_tile(i, tm)
    return pl.pallas_call(
        _ffn_pre_kernel,
        grid=(n // tm,),
        in_specs=[pl.BlockSpec((tm, D_MODEL), lambda i: (i, 0)),
                  pl.BlockSpec((1, 1, D_MODEL), lambda i: (row(i), 0, 3)),
                  pl.BlockSpec((1, 1, D_MODEL), lambda i: (row(i), 0, 4)),
                  pl.BlockSpec((D_MODEL, LANES), lambda i: (0, 0)),
                  pl.BlockSpec((D_MODEL, 2 * D_SHARED), lambda i: (0, 0)),
                  pl.BlockSpec((D_SHARED, D_MODEL), lambda i: (0, 0))],
        out_specs=[pl.BlockSpec((tm, D_PACK), lambda i: (i, 0)),
                   pl.BlockSpec((tm, LANES), lambda i: (i, 0)),
                   pl.BlockSpec((tm, D_MODEL), lambda i: (i, 0))],
        out_shape=[jax.ShapeDtypeStruct((n, D_PACK), jnp.int32),
                   jax.ShapeDtypeStruct((n, LANES), F32),
                   jax.ShapeDtypeStruct((n, D_MODEL), F32)],
        compiler_params=_cparams("parallel"),
    )(x, mod3, mod3, router_w_pad, ws13, ws2)


ROUTE_T = 512
PER_GROUP = N_EXPERTS // N_GROUPS
NEG_INF = float("-inf")


def _first_max(x, iota, n):
    m = jnp.max(x, axis=0, keepdims=True)
    first = jnp.min(jnp.where(x == m, iota, n), axis=0, keepdims=True)
    return m, iota == first


def _route_kernel(lg_ref, bias_ref, tri_ref, idx_ref, w_ref, rank_ref, cnt_ref, carry_ref):
    @pl.when(pl.program_id(0) == 0)
    def _():
        carry_ref[...] = jnp.zeros_like(carry_ref)

    t = lg_ref.shape[0]
    logits = jnp.transpose(lg_ref[...])[:N_EXPERTS]
    scores = 1.0 / (1.0 + jnp.exp(-logits))
    sel = scores + bias_ref[...]
    sub_g = lax.broadcasted_iota(jnp.int32, (PER_GROUP, t), 0)
    sub_e = lax.broadcasted_iota(jnp.int32, (N_EXPERTS, t), 0)
    grp_rows = []
    for g in range(N_GROUPS):
        x = sel[g * PER_GROUP:(g + 1) * PER_GROUP]
        m1, hit = _first_max(x, sub_g, PER_GROUP)
        m2 = jnp.max(jnp.where(hit, NEG_INF, x), axis=0, keepdims=True)
        grp_rows.append(m1 + m2)
    cur = jnp.concatenate(grp_rows, axis=0)
    sub_grp = lax.broadcasted_iota(jnp.int32, (N_GROUPS, t), 0)
    grp_on = jnp.zeros((N_GROUPS, t), F32)
    for _ in range(TOPK_GROUPS):
        _, hit = _first_max(cur, sub_grp, N_GROUPS)
        grp_on = jnp.where(hit, 1.0, grp_on)
        cur = jnp.where(hit, NEG_INF, cur)
    exp_on = jnp.concatenate([jnp.broadcast_to(grp_on[g:g + 1], (PER_GROUP, t)) for g in range(N_GROUPS)], axis=0)
    cur = jnp.where(exp_on > 0.0, sel, NEG_INF)
    hits, idx_rows = [], []
    for _ in range(TOP_K):
        _, hit = _first_max(cur, sub_e, N_EXPERTS)
        hits.append(hit)
        idx_rows.append(jnp.sum(jnp.where(hit, sub_e, 0), axis=0, keepdims=True))
        cur = jnp.where(hit, NEG_INF, cur)
    w_rows = [jnp.sum(jnp.where(hit, scores, 0.0), axis=0, keepdims=True) for hit in hits]
    total = functools.reduce(lambda a, b: a + b, w_rows)
    w_ref[...] = jnp.concatenate([w / total * ROUTED_SCALE for w in w_rows], axis=0)
    idx_ref[...] = jnp.concatenate(idx_rows, axis=0)
    chosen = functools.reduce(lambda a, b: a + b, [jnp.where(hit, 1.0, 0.0) for hit in hits])
    incl = jnp.dot(chosen.astype(BF16), tri_ref[...], preferred_element_type=F32)
    before = incl - chosen + carry_ref[:, :1]
    rank_ref[...] = jnp.concatenate(
        [jnp.sum(jnp.where(hit, before, 0.0), axis=0, keepdims=True) for hit in hits], axis=0).astype(jnp.int32)
    carry_ref[...] = carry_ref[...] + incl[:, t - 1:t]
    cnt_ref[...] = carry_ref[...]


def _route(logits, router_bias):
    n = logits.shape[0]
    t = ROUTE_T
    tri = (jnp.arange(t)[:, None] <= jnp.arange(t)[None, :]).astype(BF16)
    bias_b = jnp.broadcast_to(router_bias.astype(F32)[:, None], (N_EXPERTS, t))
    slot = pl.BlockSpec((TOP_K, t), lambda i: (0, i))
    idx, w, rank, cnt = pl.pallas_call(
        _route_kernel,
        grid=(n // t,),
        in_specs=[pl.BlockSpec((t, LANES), lambda i: (i, 0)),
                  pl.BlockSpec((N_EXPERTS, t), lambda i: (0, 0)),
                  pl.BlockSpec((t, t), lambda i: (0, 0))],
        out_specs=[slot, slot, slot, pl.BlockSpec((N_EXPERTS, LANES), lambda i: (0, 0))],
        out_shape=[jax.ShapeDtypeStruct((TOP_K, n), jnp.int32), jax.ShapeDtypeStruct((TOP_K, n), F32),
                   jax.ShapeDtypeStruct((TOP_K, n), jnp.int32), jax.ShapeDtypeStruct((N_EXPERTS, LANES), F32)],
        scratch_shapes=[pltpu.VMEM((N_EXPERTS, LANES), F32)],
        compiler_params=_cparams("arbitrary"),
    )(logits, bias_b, tri)
    return idx, w, rank, cnt[:, 0].astype(jnp.int32)


def _experts_kernel(be_ref, bv_ref, x_ref, w1_ref, w3_ref, w2_ref, o_ref, w1b, w3b, w2b):
    i = pl.program_id(0)
    changed = jnp.logical_or(i == 0, be_ref[i] != be_ref[jnp.maximum(i - 1, 0)])

    @pl.when(changed)
    def _():
        w1b[...] = w1_ref[0, 0].astype(BF16)
        w3b[...] = w3_ref[0, 0].astype(BF16)
        w2b[...] = w2_ref[0, 0].astype(BF16)

    live = lax.broadcasted_iota(jnp.int32, (MOE_BLOCK, 1), 0) < bv_ref[i]
    xb = _unpack_bf16_pairs(jnp.where(live, x_ref[...], 0)).astype(BF16)
    hid = _silu(jnp.dot(xb, w1b[...], preferred_element_type=F32)) * \
        jnp.dot(xb, w3b[...], preferred_element_type=F32)
    o_ref[...] = _pack_bf16_pairs(jnp.dot(hid.astype(BF16), w2b[...], preferred_element_type=F32))


def _grouped_experts(xs, block_expert, block_valid, w1, w3, w2, layer):
    rows = xs.shape[0]
    n_blocks = rows // MOE_BLOCK
    grid_spec = pltpu.PrefetchScalarGridSpec(
        num_scalar_prefetch=2,
        grid=(n_blocks,),
        in_specs=[pl.BlockSpec((MOE_BLOCK, D_PACK), lambda i, be, bv: (i, 0)),
                  pl.BlockSpec((1, 1, D_MODEL, D_EXPERT), lambda i, be, bv: (layer, be[i], 0, 0)),
                  pl.BlockSpec((1, 1, D_MODEL, D_EXPERT), lambda i, be, bv: (layer, be[i], 0, 0)),
                  pl.BlockSpec((1, 1, D_EXPERT, D_MODEL), lambda i, be, bv: (layer, be[i], 0, 0))],
        out_specs=pl.BlockSpec((MOE_BLOCK, D_PACK), lambda i, be, bv: (i, 0)),
        scratch_shapes=[pltpu.VMEM((D_MODEL, D_EXPERT), BF16), pltpu.VMEM((D_MODEL, D_EXPERT), BF16),
                        pltpu.VMEM((D_EXPERT, D_MODEL), BF16)])
    return pl.pallas_call(
        _experts_kernel,
        grid_spec=grid_spec,
        out_shape=jax.ShapeDtypeStruct((rows, D_PACK), jnp.int32),
        compiler_params=_cparams("arbitrary"),
    )(block_expert, block_valid, xs, w1, w3, w2)


SC_CORES = 2
SC_SUBCORES = 16
SC_WORKERS = SC_CORES * SC_SUBCORES
SC_ROWS = 64


def _sc_mesh():
    return plsc.VectorSubcoreMesh(core_axis_name="c", subcore_axis_name="s",
                                  num_cores=SC_CORES, num_subcores=SC_SUBCORES)


def _sc_dispatch(table, dest, rows):
    n, d = table.shape
    kk = dest.shape[0]
    per_worker = n // SC_WORKERS
    n_chunks = per_worker // SC_ROWS
    assert per_worker * SC_WORKERS == n and n_chunks * SC_ROWS == per_worker
    idx = dest.reshape(kk, SC_WORKERS, n_chunks, SC_ROWS).transpose(1, 2, 0, 3).reshape(
        SC_WORKERS, n_chunks * kk, SC_ROWS)

    @functools.partial(
        pl.kernel, mesh=_sc_mesh(),
        out_type=jax.ShapeDtypeStruct((rows, d), table.dtype),
        scratch_types=[pltpu.VMEM((n_chunks * kk, SC_ROWS), jnp.int32),
                       pltpu.VMEM((SC_ROWS, d), table.dtype),
                       pltpu.SemaphoreType.DMA])
    def dispatch(table_hbm, idx_hbm, out_hbm, idx_v, rows_v, sem):
        wid = lax.axis_index("s") * SC_CORES + lax.axis_index("c")
        base = wid * per_worker
        pltpu.sync_copy(idx_hbm.at[wid], idx_v)

        @pl.loop(0, n_chunks)
        def _(j):
            pltpu.sync_copy(table_hbm.at[pl.ds(base + j * SC_ROWS, SC_ROWS)], rows_v)
            copies = [pltpu.async_copy(rows_v, out_hbm.at[idx_v.at[j * kk + k]], sem) for k in range(kk)]
            for cp in copies:
                cp.wait()

    return dispatch(table, idx)


def _sc_gather(table, idx):
    b = idx.shape[0]
    d = table.shape[1]
    per_worker = b // SC_WORKERS
    n_chunks = per_worker // SC_ROWS
    assert per_worker * SC_WORKERS == b and n_chunks * SC_ROWS == per_worker

    @functools.partial(
        pl.kernel, mesh=_sc_mesh(),
        out_type=jax.ShapeDtypeStruct((b, d), table.dtype),
        scratch_types=[pltpu.VMEM((n_chunks, SC_ROWS), jnp.int32),
                       pltpu.VMEM((SC_ROWS, d), table.dtype),
                       pltpu.SemaphoreType.DMA])
    def gather(table_hbm, idx_hbm, out_hbm, idx_v, rows_v, sem):
        wid = lax.axis_index("s") * SC_CORES + lax.axis_index("c")
        base = wid * per_worker
        pltpu.sync_copy(idx_hbm.at[wid], idx_v)

        @pl.loop(0, n_chunks)
        def _(j):
            pltpu.async_copy(table_hbm.at[idx_v.at[j]], rows_v, sem).wait()
            pltpu.sync_copy(rows_v, out_hbm.at[pl.ds(base + j * SC_ROWS, SC_ROWS)])

    return gather(table, idx.reshape(SC_WORKERS, n_chunks, SC_ROWS))


POST_TM = 256


def _ffn_post_kernel(x_ref, yg_ref, w_ref, s_ref, gate_ref, g_ref, b_ref, o_ref):
    w = w_ref[...]
    routed = _unpack_bf16_pairs(yg_ref[0]) * w[:, 0:1]
    for k in range(1, TOP_K):
        routed = routed + _unpack_bf16_pairs(yg_ref[k]) * w[:, k:k + 1]
    r = DEEPNORM_ALPHA * x_ref[...] + gate_ref[0] * (routed + s_ref[...])
    o_ref[...] = _layer_norm_rows(r, g_ref[...], b_ref[...])


def _ffn_post(x, yg, w_tok, shared, mod3, ln_g, ln_b):
    n = x.shape[0]
    tm = POST_TM
    row = lambda i: _mod_row_of_tile(i, tm)
    tile = pl.BlockSpec((tm, D_MODEL), lambda i: (i, 0))
    vec = pl.BlockSpec((1, D_MODEL), lambda i: (0, 0))
    return pl.pallas_call(
        _ffn_post_kernel,
        grid=(n // tm,),
        in_specs=[tile, pl.BlockSpec((TOP_K, tm, D_PACK), lambda i: (0, i, 0)),
                  pl.BlockSpec((tm, TOP_K), lambda i: (i, 0)), tile,
                  pl.BlockSpec((1, 1, D_MODEL), lambda i: (row(i), 0, 5)), vec, vec],
        out_specs=tile,
        out_shape=jax.ShapeDtypeStruct((n, D_MODEL), F32),
        compiler_params=_cparams("parallel"),
    )(x, yg, w_tok, shared, mod3, ln_g.reshape(1, D_MODEL), ln_b.reshape(1, D_MODEL))


def _moe_routed(h, idx, rank, counts, w1, w3, w2, layer):
    n = h.shape[0]
    padded = (counts + MOE_BLOCK - 1) // MOE_BLOCK * MOE_BLOCK
    pad_end = jnp.cumsum(padded)
    pad_start = pad_end - padded
    experts = jnp.arange(N_EXPERTS, dtype=jnp.int32)
    dest = jnp.sum(jnp.where(idx[:, :, None] == experts, pad_start, 0), axis=-1) + rank
    n_blocks = n * TOP_K // MOE_BLOCK + N_EXPERTS
    rows = n_blocks * MOE_BLOCK
    block_start = jnp.arange(n_blocks, dtype=jnp.int32) * MOE_BLOCK
    block_expert = jnp.minimum(jnp.sum(pad_end[None, :] <= block_start[:, None], axis=1), N_EXPERTS - 1).astype(jnp.int32)
    of_block = block_expert[:, None] == experts
    used = jnp.sum(jnp.where(of_block, counts, 0), axis=1) - (block_start - jnp.sum(jnp.where(of_block, pad_start, 0), axis=1))
    block_valid = jnp.clip(used, 0, MOE_BLOCK).astype(jnp.int32)
    xs = _sc_dispatch(h, dest, rows)
    ys = _grouped_experts(xs, block_expert, block_valid, w1, w3, w2, layer)
    return _sc_gather(ys, dest.reshape(-1)).reshape(TOP_K, n, D_PACK)


def _even_layer(x, mod3, j, w_in, w_out, q_norm, k_norm, log_decay, gn_g,
                cache_k, cache_v, state_ret):
    (p,) = _modulated_proj(x, mod3, 0, w_in.astype(BF16), [EVEN_IN], [F32])
    q, k = _attn_prep(p, q_norm, k_norm)
    v = p[:, A_Q + A_KV:A_Q + 2 * A_KV]
    k_p = k[:N_P].reshape(BATCH, SEQ, A_KV)
    v_p = v[:N_P].reshape(BATCH, SEQ, A_KV)
    o_p = _attention(q, k_p, v_p, 0, BATCH, SEQ)
    k_all = jnp.concatenate([k[N_P:].reshape(DEC_BATCH, DEC_SEQ, A_KV),
                             cache_k.reshape(DEC_BATCH, PAST_LEN, A_KV)], axis=1)
    v_all = jnp.concatenate([v[N_P:].reshape(DEC_BATCH, DEC_SEQ, A_KV),
                             cache_v.reshape(DEC_BATCH, PAST_LEN, A_KV)], axis=1)
    o_s = _attention(q, k_all, v_all, N_P // DEC_SEQ, DEC_BATCH, DEC_SEQ)
    o_attn = jnp.concatenate([o_p, o_s], axis=0)
    lg_rows = jnp.broadcast_to(log_decay.reshape(2 * H_B, 1), (2 * H_B, LANES))
    zeros_s = jnp.zeros((BATCH, 2, H_B, DK_B, DV_B), F32)
    r_p, s_p = _retention(p, 0, BATCH, SEQ, lg_rows, zeros_s, gn_g)
    r_s, _ = _retention(p, N_P // DEC_SEQ, DEC_BATCH, DEC_SEQ, lg_rows, state_ret, gn_g)
    o_ret = jnp.concatenate([r_p, r_s], axis=0)
    return ([o_attn, o_ret], w_out.astype(BF16), k_p.reshape(BATCH, SEQ, N_KV_A, HD_A),
            v_p.reshape(BATCH, SEQ, N_KV_A, HD_A), s_p)


def _chunk_cumsum(g, batch, t):
    gc = g.reshape(batch, t // GC, GC, 2, H_C)
    f = jnp.cumsum(gc[:, :, :, 0], axis=2)
    b = jnp.cumsum(gc[:, :, ::-1, 1], axis=2)[:, :, ::-1]
    return jnp.stack([f, b], axis=3)


def _odd_layer(x, mod3, j, w_in, conv_w, a_log, dt_bias, norm_g, w_out, state_gdn):
    w_main = w_in[:, :2 * C_QK + 2 * C_V].astype(BF16)
    w_ab = jnp.pad(w_in[:, 2 * C_QK + 2 * C_V:], ((0, 0), (0, LANES - 4 * H_C))).astype(BF16)
    w_cat = jnp.concatenate([w_main, w_ab], axis=1)
    pm, pab = _modulated_proj(x, mod3, 0, w_cat, [2 * C_QK + 2 * C_V, LANES], [F32, F32])
    qkv_p = _gdn_prep(pm, conv_w, 0, BATCH, SEQ)
    qkv_s = _gdn_prep(pm, conv_w, N_P // DEC_SEQ, DEC_BATCH, DEC_SEQ)
    ab = pab[:, :4 * H_C].reshape(N_TOK, 2, 2, H_C)
    beta = jax.nn.sigmoid(ab[:, 0])
    g = -jnp.exp(a_log.astype(F32)) * jax.nn.softplus(ab[:, 1] + dt_bias.astype(F32))

    gcs_p = _chunk_cumsum(g[:N_P], BATCH, SEQ)
    gcs_s = _chunk_cumsum(g[N_P:], DEC_BATCH, DEC_SEQ)
    gcol = jnp.concatenate([gcs_p.reshape(N_P, 2 * H_C), gcs_s.reshape(N_S, 2 * H_C)], axis=0)
    gbeta = jnp.concatenate([beta.reshape(N_TOK, 2 * H_C), gcol], axis=1)
    zeros_s = jnp.zeros((BATCH, 2, H_C, DK_C, DV_C), F32)
    o_p, s_p = _gdn(qkv_p, pm, gbeta, gcs_p.transpose(0, 3, 4, 1, 2), 0, BATCH, SEQ, zeros_s, norm_g)
    o_s, _ = _gdn(qkv_s, pm, gbeta, gcs_s.transpose(0, 3, 4, 1, 2), N_P // DEC_SEQ, DEC_BATCH, DEC_SEQ,
                  state_gdn, norm_g)
    return [jnp.concatenate([o_p, o_s], axis=0)], w_out.astype(BF16), s_p


def kernel(x_prompt, x_sample, cache_attn_k, cache_attn_v, state_ret, state_gdn, c, c_ctx, mod_w, mod_b, ln_g, ln_b, even_w_in, even_w_out, attn_q_norm, attn_k_norm, ret_log_decay, ret_norm_g, odd_w_in, gdn_conv_w, gdn_a_log, gdn_dt_bias, gdn_norm_g, odd_w_out, router_w, router_bias, expert_w1, expert_w3, expert_w2, shared_w1, shared_w3, shared_w2):
    x = jnp.concatenate([x_prompt.reshape(N_P, D_MODEL), x_sample.reshape(N_S, D_MODEL)], axis=0)
    cvec = jnp.concatenate([c_ctx[None, :], c, jnp.zeros((MOD_ROWS - N_MOD, D_MODEL), F32)], axis=0)
    mod_all = _mod_vectors(cvec, mod_w, mod_b)
    new_k, new_v, new_ret, new_gdn = [], [], [], []
    for l in range(DEPTH):
        j = l // 2
        mod3 = mod_all[l].reshape(MOD_ROWS, 1, 6 * D_MODEL)
        if l % 2 == 0:
            a_list, w_o, k_p, v_p, s_p = _even_layer(
                x, mod3, j, even_w_in[j], even_w_out[j], attn_q_norm[j], attn_k_norm[j],
                ret_log_decay[j], ret_norm_g[j], cache_attn_k[:, j], cache_attn_v[:, j], state_ret[:, j])
            new_k.append(k_p)
            new_v.append(v_p)
            new_ret.append(s_p)
        else:
            a_list, w_o, s_p = _odd_layer(
                x, mod3, j, odd_w_in[j], gdn_conv_w[j], gdn_a_log[j], gdn_dt_bias[j], gdn_norm_g[j],
                odd_w_out[j], state_gdn[:, j])
            new_gdn.append(s_p)
        x = _outproj_ln(a_list, w_o, x, mod3, 2, ln_g[l, 0], ln_b[l, 0])
        rw = jnp.pad(router_w[l], ((0, 0), (0, LANES - N_EXPERTS)))
        ws13 = jnp.concatenate([shared_w1[l], shared_w3[l]], axis=1).astype(BF16)
        h, logits, shared = _ffn_pre(x, mod3, rw, ws13, shared_w2[l].astype(BF16))
        idx, w, rank, counts = _route(logits, router_bias[l])
        yg = _moe_routed(h, idx, rank, counts, expert_w1, expert_w3, expert_w2, l)
        x = _ffn_post(x, yg, w.T, shared, mod3, ln_g[l, 1], ln_b[l, 1])
    return (x[:N_P].reshape(BATCH, SEQ, D_MODEL), x[N_P:].reshape(DEC_BATCH, DEC_SEQ, D_MODEL),
            jnp.stack(new_k, axis=1), jnp.stack(new_v, axis=1),
            jnp.stack(new_ret, axis=1), jnp.stack(new_gdn, axis=1))
```

```python
import functools
import math

import jax
import jax.numpy as jnp
from jax import lax
from jax.experimental import pallas as pl
from jax.experimental.pallas import tpu as pltpu
from jax.experimental.pallas import tpu_sc as plsc

D_MODEL = 1024
BATCH = 16
SEQ = 256
DEPTH = 4
DEC_BATCH = 4
DEC_SEQ = 4096
PAST_LEN = 512
GRID_W = 64
N_HEADS_A = 8
N_KV_A = 2
HD_A = 64
ROPE_THETA = 10000.0
H_B = 4
DK_B = 64
DV_B = 128
RET_CHUNK = 128
H_C = 8
DK_C = 128
DV_C = 128
CONV_K = 5
GDN_CHUNK = 64
N_EXPERTS = 64
TOP_K = 8
N_GROUPS = 8
TOPK_GROUPS = 4
D_EXPERT = 256
D_SHARED = 256
ROUTED_SCALE = 2.5
MOE_BLOCK = 512
A_Q = N_HEADS_A * HD_A
A_KV = N_KV_A * HD_A
B_QK = H_B * DK_B
B_V = H_B * DV_B
EVEN_IN = A_Q + 2 * A_KV + 2 * B_QK + 2 * B_V
C_QK = H_C * DK_C
C_V = H_C * DV_C
DEEPNORM_ALPHA = (2 * DEPTH) ** 0.25
EPS = 1e-6

N_P = BATCH * SEQ
N_S = DEC_BATCH * DEC_SEQ
N_TOK = N_P + N_S
N_MOD = 1 + DEC_BATCH
MOD_ROWS = 8

LANES = 128
VMEM_LIMIT = 56 * 1024 * 1024

F32 = jnp.float32
BF16 = jnp.bfloat16


def _cparams(*sem):
    return pltpu.CompilerParams(dimension_semantics=sem, vmem_limit_bytes=VMEM_LIMIT)


def _bdot(a, b):
    return jnp.dot(a.astype(BF16), b.astype(BF16), preferred_element_type=F32)


def _bdot_t(a, b):
    return lax.dot_general(a.astype(BF16), b.astype(BF16), (((0,), (0,)), ((), ())),
                           preferred_element_type=F32)


def _bdot_nt(a, b):
    return lax.dot_general(a.astype(BF16), b.astype(BF16), (((1,), (1,)), ((), ())),
                           preferred_element_type=F32)


def _split3(a):
    hi = a.astype(BF16)
    r = a - hi.astype(F32)
    mid = r.astype(BF16)
    lo = (r - mid.astype(F32)).astype(BF16)
    return hi, mid, lo


def _dot_hp(a, b):
    a0, a1, a2 = _split3(a)
    b0, b1, b2 = _split3(b)
    d = lambda x, y: jnp.dot(x, y, preferred_element_type=F32)
    small = d(a0, b2) + d(a2, b0) + d(a1, b1)
    return (d(a0, b1) + d(a1, b0)) + small + d(a0, b0)


def _silu(x):
    return x * (1.0 / (1.0 + jnp.exp(-x)))


HI16 = 0xFFFF0000
D_PACK = D_MODEL // 2


def _pack_bf16_pairs(x):
    c = x.shape[1] // 2
    lo = pltpu.bitcast(x[:, :c].astype(BF16).astype(F32), jnp.uint32) >> 16
    hi = pltpu.bitcast(x[:, c:].astype(BF16).astype(F32), jnp.uint32) & jnp.uint32(HI16)
    return pltpu.bitcast(lo | hi, jnp.int32)


def _unpack_bf16_pairs(p):
    u = pltpu.bitcast(p, jnp.uint32)
    return jnp.concatenate([pltpu.bitcast(u << 16, F32), pltpu.bitcast(u & jnp.uint32(HI16), F32)], axis=1)


def _mod_row_of_tile(i, tile):
    tiles_p = N_P // tile
    tiles_per_b = DEC_SEQ // tile
    return jnp.where(i < tiles_p, 0, 1 + (i - tiles_p) // tiles_per_b)


MOD_TN = 1536


def _mod_kernel(c_ref, w_ref, b_ref, o_ref):
    a = _silu(c_ref[...])
    o_ref[0] = _bdot(a, w_ref[0]) + b_ref[0]


def _mod_vectors(cvec, mod_w, mod_b):
    n6 = 6 * D_MODEL
    return pl.pallas_call(
        _mod_kernel,
        grid=(DEPTH, n6 // MOD_TN),
        in_specs=[pl.BlockSpec((MOD_ROWS, D_MODEL), lambda l, j: (0, 0)),
                  pl.BlockSpec((1, D_MODEL, MOD_TN), lambda l, j: (l, 0, j)),
                  pl.BlockSpec((1, 1, MOD_TN), lambda l, j: (l, 0, j))],
        out_specs=pl.BlockSpec((1, MOD_ROWS, MOD_TN), lambda l, j: (l, 0, j)),
        out_shape=jax.ShapeDtypeStruct((DEPTH, MOD_ROWS, n6), F32),
        compiler_params=_cparams("parallel", "parallel"),
    )(cvec, mod_w, mod_b.reshape(DEPTH, 1, n6))


PROJ_TM = 256


def _proj_kernel(x_ref, shift_ref, scale_ref, w_ref, *o_refs, widths):
    h = (x_ref[...] * (1.0 + scale_ref[0]) + shift_ref[0]).astype(BF16)
    off = 0
    for o_ref, wd in zip(o_refs, widths):
        o_ref[...] = jnp.dot(h, w_ref[:, off:off + wd], preferred_element_type=F32).astype(o_ref.dtype)
        off += wd


def _modulated_proj(x, mod3, shift_blk, w_bf16, widths, dtypes):
    n = x.shape[0]
    tm = PROJ_TM
    row = lambda i: _mod_row_of_tile(i, tm)
    return pl.pallas_call(
        functools.partial(_proj_kernel, widths=tuple(widths)),
        grid=(n // tm,),
        in_specs=[pl.BlockSpec((tm, D_MODEL), lambda i: (i, 0)),
                  pl.BlockSpec((1, 1, D_MODEL), lambda i: (row(i), 0, shift_blk)),
                  pl.BlockSpec((1, 1, D_MODEL), lambda i: (row(i), 0, shift_blk + 1)),
                  pl.BlockSpec((D_MODEL, sum(widths)), lambda i: (0, 0))],
        out_specs=[pl.BlockSpec((tm, wd), lambda i: (i, 0)) for wd in widths],
        out_shape=[jax.ShapeDtypeStruct((n, wd), dt) for wd, dt in zip(widths, dtypes)],
        compiler_params=_cparams("parallel"),
    )(x, mod3, mod3, w_bf16)


OUT_TM = 512


def _layer_norm_rows(r, g, b):
    mu = jnp.mean(r, axis=-1, keepdims=True)
    d = r - mu
    var = jnp.mean(d * d, axis=-1, keepdims=True)
    return d * lax.rsqrt(var + EPS) * g + b


def _outproj_kernel(*refs, n_a):
    a_refs = refs[:n_a]
    w_ref, x_ref, gate_ref, g_ref, b_ref, o_ref = refs[n_a:]
    off = 0
    acc = None
    for a_ref in a_refs:
        wd = a_ref.shape[1]
        part = jnp.dot(a_ref[...].astype(BF16), w_ref[off:off + wd, :], preferred_element_type=F32)
        acc = part if acc is None else acc + part
        off += wd
    r = DEEPNORM_ALPHA * x_ref[...] + gate_ref[0] * acc
    o_ref[...] = _layer_norm_rows(r, g_ref[...], b_ref[...])


def _outproj_ln(a_list, w_bf16, x, mod3, gate_blk, ln_g, ln_b):
    n = x.shape[0]
    tm = OUT_TM
    row = lambda i: _mod_row_of_tile(i, tm)
    kdim = w_bf16.shape[0]
    return pl.pallas_call(
        functools.partial(_outproj_kernel, n_a=len(a_list)),
        grid=(n // tm,),
        in_specs=[pl.BlockSpec((tm, a.shape[1]), lambda i: (i, 0)) for a in a_list] + [
            pl.BlockSpec((kdim, D_MODEL), lambda i: (0, 0)),
            pl.BlockSpec((tm, D_MODEL), lambda i: (i, 0)),
            pl.BlockSpec((1, 1, D_MODEL), lambda i: (row(i), 0, gate_blk)),
            pl.BlockSpec((1, D_MODEL), lambda i: (0, 0)),
            pl.BlockSpec((1, D_MODEL), lambda i: (0, 0))],
        out_specs=pl.BlockSpec((tm, D_MODEL), lambda i: (i, 0)),
        out_shape=jax.ShapeDtypeStruct((n, D_MODEL), F32),
        compiler_params=_cparams("parallel"),
    )(*a_list, w_bf16, x, mod3, ln_g.reshape(1, D_MODEL), ln_b.reshape(1, D_MODEL))


PREP_TM = 256
ROPE_SEG = HD_A // 4


def _head_rms(x, gain, ones_bd):
    parts = _split3(x * x)
    ss = functools.reduce(lambda a, b: a + b,
                          [jnp.dot(p, ones_bd, preferred_element_type=F32) for p in reversed(parts)])
    return x * lax.rsqrt(ss * (1.0 / HD_A) + EPS) * gain


def _rope_lanes(x, cos, sin):
    lane = lax.broadcasted_iota(jnp.int32, (1, LANES), 1)
    first = (lane % (2 * ROPE_SEG)) < ROPE_SEG
    cols = []
    for b in range(x.shape[1] // LANES):
        xb = x[:, b * LANES:(b + 1) * LANES]
        partner = jnp.where(first, pltpu.roll(xb, LANES - ROPE_SEG, axis=1), pltpu.roll(xb, ROPE_SEG, axis=1))
        cols.append(xb * cos + partner * sin)
    return cols[0] if len(cols) == 1 else jnp.concatenate(cols, axis=1)


def _attn_prep_kernel(q_ref, kv_ref, cos_ref, sin_ref, qg_ref, kg_ref, bd_ref, qo_ref, ko_ref):
    cos = cos_ref[...]
    sin = sin_ref[...]
    q = _rope_lanes(_head_rms(q_ref[...], qg_ref[...], bd_ref[...]), cos, sin)
    qo_ref[...] = (q * (HD_A ** -0.5)).astype(BF16)
    k = _head_rms(kv_ref[:, :A_KV], kg_ref[...], bd_ref[:A_KV, :A_KV])
    ko_ref[...] = _rope_lanes(k, cos, sin)


def _rope_tables():
    rows = DEC_SEQ // GRID_W
    row = jnp.repeat(jnp.arange(rows), GRID_W).astype(F32)
    colp = jnp.tile(jnp.arange(GRID_W), rows).astype(F32)
    inv_freq = ROPE_THETA ** (-jnp.arange(ROPE_SEG, dtype=F32) / ROPE_SEG)
    ar = row[:, None] * inv_freq[None, :]
    ac = colp[:, None] * inv_freq[None, :]
    cos = jnp.concatenate([jnp.cos(ar), jnp.cos(ar), jnp.cos(ac), jnp.cos(ac)], axis=-1)
    sin = jnp.concatenate([-jnp.sin(ar), jnp.sin(ar), -jnp.sin(ac), jnp.sin(ac)], axis=-1)
    cos = jnp.concatenate([jnp.ones((PREP_TM, HD_A), F32), cos], axis=0)
    sin = jnp.concatenate([jnp.zeros((PREP_TM, HD_A), F32), sin], axis=0)
    return jnp.tile(cos, (1, LANES // HD_A)), jnp.tile(sin, (1, LANES // HD_A))


def _attn_prep(p, q_norm, k_norm):
    tm = PREP_TM
    tiles_p = N_P // tm
    tiles_seq = DEC_SEQ // tm
    tab = lambda i: (jnp.where(i < tiles_p, 0, 1 + (i - tiles_p) % tiles_seq), 0)
    cos, sin = _rope_tables()
    head_id = jnp.arange(A_Q) // HD_A
    ones_bd = (head_id[:, None] == head_id[None, :]).astype(BF16)
    return pl.pallas_call(
        _attn_prep_kernel,
        grid=(N_TOK // tm,),
        in_specs=[pl.BlockSpec((tm, A_Q), lambda i: (i, 0)),
                  pl.BlockSpec((tm, 2 * A_KV), lambda i: (i, A_Q // (2 * A_KV))),
                  pl.BlockSpec((tm, LANES), tab), pl.BlockSpec((tm, LANES), tab),
                  pl.BlockSpec((1, A_Q), lambda i: (0, 0)), pl.BlockSpec((1, A_KV), lambda i: (0, 0)),
                  pl.BlockSpec((A_Q, A_Q), lambda i: (0, 0))],
        out_specs=[pl.BlockSpec((tm, A_Q), lambda i: (i, 0)), pl.BlockSpec((tm, A_KV), lambda i: (i, 0))],
        out_shape=[jax.ShapeDtypeStruct((N_TOK, A_Q), BF16), jax.ShapeDtypeStruct((N_TOK, A_KV), F32)],
        compiler_params=_cparams("parallel"),
    )(p, p, cos, sin, jnp.tile(q_norm, N_HEADS_A).reshape(1, A_Q), jnp.tile(k_norm, N_KV_A).reshape(1, A_KV),
      ones_bd)


ATT_TQ = 256
GROUP_A = N_HEADS_A // N_KV_A


def _attn_kernel(q_ref, kt_ref, kts_ref, v_ref, vs_ref, o_ref):
    low = lax.broadcasted_iota(jnp.int32, (1, LANES), 1) < HD_A
    for j in range(N_HEADS_A // 2):
        qb = q_ref[:, j * LANES:(j + 1) * LANES]
        outs = []
        for half in range(2):
            kv_head = (2 * j + half) // GROUP_A
            qh = jnp.where(low if half == 0 else jnp.logical_not(low), qb, jnp.zeros_like(qb))
            kt, v = (kt_ref[0], v_ref[0]) if half == kv_head else (kts_ref[0], vs_ref[0])
            s = jnp.dot(qh, kt, preferred_element_type=F32)
            m = jnp.max(s, axis=-1, keepdims=True)
            p = jnp.exp(s - m)
            l = jnp.sum(p, axis=-1, keepdims=True)
            pv = jnp.dot(p.astype(BF16), v, preferred_element_type=F32)
            outs.append(pv * (1.0 / l))
        o_ref[:, j * LANES:(j + 1) * LANES] = jnp.where(low, outs[0], outs[1]).astype(o_ref.dtype)


def _attention(q, k, v, row_blk0, batch, t):
    tk = v.shape[1]
    tq = min(ATT_TQ, t)
    nq = t // tq
    swap = lambda x: jnp.roll(x, HD_A, axis=-1)
    kt = jnp.swapaxes(k, 1, 2).astype(BF16)
    kts = jnp.swapaxes(swap(k), 1, 2).astype(BF16)
    kspec = pl.BlockSpec((1, LANES, tk), lambda b, i: (b, 0, 0))
    vspec = pl.BlockSpec((1, tk, LANES), lambda b, i: (b, 0, 0))
    return pl.pallas_call(
        _attn_kernel,
        grid=(batch, nq),
        in_specs=[pl.BlockSpec((tq, A_Q), lambda b, i: ((row_blk0 + b) * nq + i, 0)), kspec, kspec, vspec, vspec],
        out_specs=pl.BlockSpec((tq, A_Q), lambda b, i: (b * nq + i, 0)),
        out_shape=jax.ShapeDtypeStruct((batch * t, A_Q), BF16),
        compiler_params=_cparams("parallel", "parallel"),
    )(q, kt, kts, v.astype(BF16), swap(v).astype(BF16))


RC = RET_CHUNK


def _ret_kernel(q_ref, k_ref, v_ref, g_ref, lg_ref, s0_ref, gn_ref, o_ref, s_out_ref,
                ob_ref, st_ref, *, t):
    nc = t // RC
    hp = pl.program_id(1)
    ii = lax.broadcasted_iota(jnp.int32, (RC, RC), 0).astype(F32)
    jj = lax.broadcasted_iota(jnp.int32, (RC, RC), 1).astype(F32)
    col_i = lax.broadcasted_iota(jnp.int32, (RC, 1), 0).astype(F32)
    lane = lax.broadcasted_iota(jnp.int32, (1, LANES), 1)
    masks = [(lane >= hh * DK_B) & (lane < (hh + 1) * DK_B) for hh in range(2)]

    consts = []
    for d in range(2):
        for hh in range(2):
            lg = lg_ref[pl.ds(d * H_B + hp * 2 + hh, 1), :][:, :1]
            if d == 0:
                diff = ii - jj
                qe, ke = col_i + 1.0, (RC - 1.0) - col_i
            else:
                diff = jj - ii
                qe, ke = RC - col_i, col_i
            intra = jnp.where(diff >= 0, jnp.exp(jnp.maximum(diff, 0.0) * lg), 0.0)
            consts.append((intra, jnp.exp(qe * lg), jnp.exp(ke * lg), jnp.exp(RC * lg)))
            s0 = s0_ref[0, d, hh]
            z = jnp.zeros((DK_B, DV_B), F32)
            st_ref[d * 2 + hh] = jnp.concatenate([s0, z] if hh == 0 else [z, s0], axis=0)

    def body(c, carry):
        combos = [(d, hh) for d in range(2) for hh in range(2)]
        rows = [pl.ds(pl.multiple_of(cc * RC, RC), RC) for cc in (c, nc - 1 - c)]
        qcs = [q_ref[rows[d], :] for d in range(2)]
        kcs = [k_ref[rows[d], :] * (DK_B ** -0.5) for d in range(2)]
        qhs = [jnp.where(masks[hh], qcs[d], 0.0) for d, hh in combos]
        khs = [jnp.where(masks[hh], kcs[d], 0.0) for d, hh in combos]
        vcs = [v_ref[rows[d], hh * DV_B:(hh + 1) * DV_B].astype(BF16) for d, hh in combos]
        sts = [st_ref[d * 2 + hh] for d, hh in combos]
        scores = [_bdot_nt(qhs[i], khs[i]) * consts[i][0] for i in range(4)]
        kvs = [_bdot_t(khs[i] * consts[i][2], vcs[i]) for i in range(4)]
        outs = [jnp.dot(jnp.concatenate([scores[i], qhs[i] * consts[i][1]], axis=1).astype(BF16),
                        jnp.concatenate([vcs[i], sts[i].astype(BF16)], axis=0),
                        preferred_element_type=F32) for i in range(4)]
        for i, (d, hh) in enumerate(combos):
            st_ref[d * 2 + hh] = sts[i] * consts[i][3] + kvs[i]
        o_ref[rows[0], :] = jnp.concatenate(outs[0:2], axis=1)
        ob_ref[rows[1], :] = jnp.concatenate(outs[2:4], axis=1)
        return carry

    lax.fori_loop(0, nc, body, 0)

    for d in range(2):
        for hh in range(2):
            s_out_ref[0, d, hh] = st_ref[d * 2 + hh][hh * DK_B:(hh + 1) * DK_B, :]

    def finish(c, carry):
        r0 = pl.multiple_of(c * RC, RC)
        o = o_ref[pl.ds(r0, RC), :] + ob_ref[pl.ds(r0, RC), :]
        gate = _silu(g_ref[pl.ds(r0, RC), :])
        ys = []
        for hh in range(2):
            oh = o[:, hh * DV_B:(hh + 1) * DV_B]
            mu = jnp.mean(oh, axis=-1, keepdims=True)
            dlt = oh - mu
            var = jnp.mean(dlt * dlt, axis=-1, keepdims=True)
            ys.append(dlt * lax.rsqrt(var + EPS))
        o_ref[pl.ds(r0, RC), :] = jnp.concatenate(ys, axis=1) * gn_ref[...] * gate
        return carry

    lax.fori_loop(0, nc, finish, 0)


def _retention(p, row_blk0, batch, t, lg_rows, s0, gn_g):
    qb, kb, vb, gb = (A_Q + 2 * A_KV) // LANES, (A_Q + 2 * A_KV + B_QK) // LANES, \
        (A_Q + 2 * A_KV + 2 * B_QK) // (2 * DV_B), (A_Q + 2 * A_KV + 2 * B_QK + B_V) // (2 * DV_B)
    return pl.pallas_call(
        functools.partial(_ret_kernel, t=t),
        grid=(batch, H_B // 2),
        in_specs=[pl.BlockSpec((t, LANES), lambda b, h: (row_blk0 + b, qb + h)),
                  pl.BlockSpec((t, LANES), lambda b, h: (row_blk0 + b, kb + h)),
                  pl.BlockSpec((t, 2 * DV_B), lambda b, h: (row_blk0 + b, vb + h)),
                  pl.BlockSpec((t, 2 * DV_B), lambda b, h: (row_blk0 + b, gb + h)),
                  pl.BlockSpec((2 * H_B, LANES), lambda b, h: (0, 0)),
                  pl.BlockSpec((1, 2, 2, DK_B, DV_B), lambda b, h: (b, 0, h, 0, 0)),
                  pl.BlockSpec((1, 2 * DV_B), lambda b, h: (0, h))],
        out_specs=[pl.BlockSpec((t, 2 * DV_B), lambda b, h: (b, h)),
                   pl.BlockSpec((1, 2, 2, DK_B, DV_B), lambda b, h: (b, 0, h, 0, 0))],
        out_shape=[jax.ShapeDtypeStruct((batch * t, B_V), F32),
                   jax.ShapeDtypeStruct((batch, 2, H_B, DK_B, DV_B), F32)],
        scratch_shapes=[pltpu.VMEM((t, 2 * DV_B), F32), pltpu.VMEM((4, LANES, DV_B), F32)],
        compiler_params=_cparams("parallel", "parallel"),
    )(p, p, p, p, lg_rows, s0, gn_g.reshape(1, B_V))


GC = 256


def _unit_tri_inverses(mats):
    ii = lax.broadcasted_iota(jnp.int32, (GC, GC), 0)
    jj = lax.broadcasted_iota(jnp.int32, (GC, GC), 1)
    block_dist = ii ^ jj
    eye = (ii == jj).astype(F32)
    ms = [jnp.where((block_dist >> 3) == 0, -a, 0.0) for a in mats]
    invs = [eye + m for m in ms]
    for _ in range(2):
        ms = [_bdot(m, m) for m in ms]
        invs = [inv + _bdot(inv, m) for inv, m in zip(invs, ms)]
    for shift in range(3, int(math.log2(GC))):
        ls = [jnp.where((block_dist >> shift) == 1, a, 0.0) for a in mats]
        ts = [_bdot(inv, l) for inv, l in zip(invs, ls)]
        invs = [inv - _bdot(t, inv) for inv, t in zip(invs, ts)]
    return invs


def _gdn_kernel(q_ref, k_ref, v_ref, z_ref, gb_ref, gr_ref, s0_ref, ng_ref, o_ref, s_out_ref,
                ob_ref, st_ref, wq_s, u_s, a_s, kg_s, *, t):
    nc = t // GC
    h = pl.program_id(1)
    ii = lax.broadcasted_iota(jnp.int32, (GC, GC), 0)
    jj = lax.broadcasted_iota(jnp.int32, (GC, GC), 1)
    incl = [ii >= jj, jj >= ii]
    strict = [ii > jj, jj > ii]
    lane32 = lax.broadcasted_iota(jnp.int32, (1, 4 * H_C), 1)
    for d in range(2):
        st_ref[d] = s0_ref[0, d, 0]

    def col(x, idx):
        return jnp.sum(jnp.where(lane32 == idx, x, 0.0), axis=1, keepdims=True)

    def gate_last(c, d):
        grow = gr_ref[0, d, 0, pl.ds(c, 1), :]
        return grow, (grow[:, GC - 1:GC] if d == 0 else grow[:, 0:1])

    def prep(cb, carry):
        mats, rhs, dst = [], [], []
        for ci in range(prep_chunks):
            c = cb * prep_chunks + ci
            rows = pl.ds(pl.multiple_of(c * GC, GC), GC)
            qc = q_ref[rows, :] * (DK_C ** -0.5)
            kc = k_ref[rows, :]
            vc = v_ref[rows, :]
            gb = gb_ref[rows, :]
            kk = _bdot_nt(kc, kc)
            qk = _bdot_nt(qc, kc)
            r2 = pl.multiple_of(c * 2 * GC, 2 * GC)
            for d in range(2):
                beta = col(gb, d * H_C + h)
                gcol = col(gb, 2 * H_C + d * H_C + h)
                grow, glast = gate_last(c, d)
                decay = jnp.exp(jnp.where(incl[d], gcol - grow, -jnp.inf))
                eg = jnp.exp(gcol)
                mats.append(jnp.where(strict[d], kk * beta * decay, 0.0))
                rhs.append(jnp.concatenate([kc * (beta * eg), vc * beta], axis=1).astype(BF16))
                dst.append((d, rows, r2))
                wq_s[d, pl.ds(r2 + GC, GC), :] = (qc * eg).astype(BF16)
                a_s[d, rows, :] = (qk * decay).astype(BF16)
                kg_s[d, rows, :] = (kc * jnp.exp(glast - gcol)).astype(BF16)
        wus = [jnp.dot(tinv.astype(BF16), r, preferred_element_type=F32)
               for tinv, r in zip(_unit_tri_inverses(mats), rhs)]
        for wu, (d, rows, r2) in zip(wus, dst):
            wq_s[d, pl.ds(r2, GC), :] = wu[:, :DK_C].astype(BF16)
            u_s[d, rows, :] = wu[:, DK_C:]
        return carry

    prep_chunks = 2 if nc % 2 == 0 else 1
    lax.fori_loop(0, nc // prep_chunks, prep, 0)

    def step(c, carry):
        ccs = [c, nc - 1 - c]
        rows = [pl.ds(pl.multiple_of(cc * GC, GC), GC) for cc in ccs]
        ss = [st_ref[d] for d in range(2)]
        sbs = [s.astype(BF16) for s in ss]
        wss = [jnp.dot(wq_s[d, pl.ds(pl.multiple_of(ccs[d] * 2 * GC, 2 * GC), 2 * GC), :], sbs[d],
                       preferred_element_type=F32) for d in range(2)]
        vnbs = [(u_s[d, rows[d], :] - wss[d][:GC]).astype(BF16) for d in range(2)]
        os_ = [wss[d][GC:] + jnp.dot(a_s[d, rows[d], :], vnbs[d], preferred_element_type=F32) for d in range(2)]
        kvs = [_bdot_t(kg_s[d, rows[d], :], vnbs[d]) for d in range(2)]
        for d in range(2):
            _, glast = gate_last(ccs[d], d)
            st_ref[d] = ss[d] * jnp.exp(glast) + kvs[d]
        o_ref[rows[0], :] = os_[0]
        ob_ref[rows[1], :] = os_[1]
        return carry

    lax.fori_loop(0, nc, step, 0)

    for d in range(2):
        s_out_ref[0, d, 0] = st_ref[d]

    def finish(c, carry):
        r0 = pl.multiple_of(c * GC, GC)
        o = o_ref[pl.ds(r0, GC), :] + ob_ref[pl.ds(r0, GC), :]
        y = o * lax.rsqrt(jnp.mean(o * o, axis=-1, keepdims=True) + EPS) * ng_ref[...]
        o_ref[pl.ds(r0, GC), :] = y * _silu(z_ref[pl.ds(r0, GC), :])
        return carry

    lax.fori_loop(0, nc, finish, 0)


GDN_PREP_BLOCK_ELEMS = 512 * 1024


def _gdn_prep_kernel(x_ref, w_ref, o_ref, *, t, heads):
    x = x_ref[...]
    row = lax.broadcasted_iota(jnp.int32, (t, 1), 0)
    centre = (CONV_K - 1) // 2
    y = x * w_ref[centre:centre + 1, :]
    for i in range(CONV_K):
        s = i - centre
        if s == 0:
            continue
        shifted = pltpu.roll(x, (-s) % t, axis=0)
        inside = jnp.logical_and(row + s >= 0, row + s < t)
        y = y + jnp.where(inside, shifted, 0.0) * w_ref[i:i + 1, :]
    y = _silu(y)
    for b in range(heads):
        yb = y[:, b * LANES:(b + 1) * LANES]
        inv_norm = lax.rsqrt(jnp.sum(yb * yb, axis=-1, keepdims=True) + EPS)
        is_qk = pl.program_id(1) * heads + b < 2 * H_C
        o_ref[:, b * LANES:(b + 1) * LANES] = yb * jnp.where(is_qk, inv_norm, 1.0)


def _gdn_prep(pm, conv_w, row_blk0, batch, t):
    heads = max(1, min(H_C, GDN_PREP_BLOCK_ELEMS // (t * LANES)))
    n_col = (2 * C_QK + C_V) // (heads * LANES)
    return pl.pallas_call(
        functools.partial(_gdn_prep_kernel, t=t, heads=heads),
        grid=(batch, n_col),
        in_specs=[pl.BlockSpec((t, heads * LANES), lambda b, j: (row_blk0 + b, j)),
                  pl.BlockSpec((CONV_K, heads * LANES), lambda b, j: (0, j))],
        out_specs=pl.BlockSpec((t, heads * LANES), lambda b, j: (b, j)),
        out_shape=jax.ShapeDtypeStruct((batch * t, 2 * C_QK + C_V), F32),
        compiler_params=_cparams("parallel", "parallel"),
    )(pm, conv_w)


def _gdn(qkv, pm, gbeta, grow, row_blk0, batch, t, s0, norm_g):
    nc = t // GC
    return pl.pallas_call(
        functools.partial(_gdn_kernel, t=t),
        grid=(batch, H_C),
        in_specs=[pl.BlockSpec((t, DK_C), lambda b, h: (b, h)),
                  pl.BlockSpec((t, DK_C), lambda b, h: (b, H_C + h)),
                  pl.BlockSpec((t, DV_C), lambda b, h: (b, 2 * H_C + h)),
                  pl.BlockSpec((t, DV_C), lambda b, h: (row_blk0 + b, 3 * H_C + h)),
                  pl.BlockSpec((t, 4 * H_C), lambda b, h: (row_blk0 + b, 0)),
                  pl.BlockSpec((1, 2, 1, nc, GC), lambda b, h: (b, 0, h, 0, 0)),
                  pl.BlockSpec((1, 2, 1, DK_C, DV_C), lambda b, h: (b, 0, h, 0, 0)),
                  pl.BlockSpec((1, DV_C), lambda b, h: (0, 0))],
        out_specs=[pl.BlockSpec((t, DV_C), lambda b, h: (b, h)),
                   pl.BlockSpec((1, 2, 1, DK_C, DV_C), lambda b, h: (b, 0, h, 0, 0))],
        out_shape=[jax.ShapeDtypeStruct((batch * t, C_V), F32),
                   jax.ShapeDtypeStruct((batch, 2, H_C, DK_C, DV_C), F32)],
        scratch_shapes=[pltpu.VMEM((t, DV_C), F32), pltpu.VMEM((2, DK_C, DV_C), F32),
                        pltpu.VMEM((2, 2 * t, DK_C), BF16), pltpu.VMEM((2, t, DV_C), F32),
                        pltpu.VMEM((2, t, GC), BF16), pltpu.VMEM((2, t, DK_C), BF16)],
        compiler_params=_cparams("parallel", "parallel"),
    )(qkv, qkv, qkv, pm, gbeta, grow, s0, norm_g.reshape(1, DV_C))


FFN_TM = 512


def _ffn_pre_kernel(x_ref, shift_ref, scale_ref, rw_ref, w13_ref, w2_ref, h_ref, logit_ref, sh_ref):
    h = x_ref[...] * (1.0 + scale_ref[0]) + shift_ref[0]
    hb = h.astype(BF16)
    h_ref[...] = _pack_bf16_pairs(h)
    logit_ref[...] = _dot_hp(h, rw_ref[...])
    up = jnp.dot(hb, w13_ref[...], preferred_element_type=F32)
    hid = _silu(up[:, :D_SHARED]) * up[:, D_SHARED:]
    sh_ref[...] = jnp.dot(hid.astype(BF16), w2_ref[...], preferred_element_type=F32).astype(sh_ref.dtype)


def _ffn_pre(x, mod3, router_w_pad, ws13, ws2):
    n = x.shape[0]
    tm = FFN_TM
    row = lambda i: _mod_row_of_tile(i, tm)
    return pl.pallas_call(
        _ffn_pre_kernel,
        grid=(n // tm,),
        in_specs=[pl.BlockSpec((tm, D_MODEL), lambda i: (i, 0)),
                  pl.BlockSpec((1, 1, D_MODEL), lambda i: (row(i), 0, 3)),
                  pl.BlockSpec((1, 1, D_MODEL), lambda i: (row(i), 0, 4)),
                  pl.BlockSpec((D_MODEL, LANES), lambda i: (0, 0)),
                  pl.BlockSpec((D_MODEL, 2 * D_SHARED), lambda i: (0, 0)),
                  pl.BlockSpec((D_SHARED, D_MODEL), lambda i: (0, 0))],
        out_specs=[pl.BlockSpec((tm, D_PACK), lambda i: (i, 0)),
                   pl.BlockSpec((tm, LANES), lambda i: (i, 0)),
                   pl.BlockSpec((tm, D_MODEL), lambda i: (i, 0))],
        out_shape=[jax.ShapeDtypeStruct((n, D_PACK), jnp.int32),
                   jax.ShapeDtypeStruct((n, LANES), F32),
                   jax.ShapeDtypeStruct((n, D_MODEL), BF16)],
        compiler_params=_cparams("parallel"),
    )(x, mod3, mod3, router_w_pad, ws13, ws2)


ROUTE_T = 512
PER_GROUP = N_EXPERTS // N_GROUPS
NEG_INF = float("-inf")


def _first_max(x, iota, n):
    m = jnp.max(x, axis=0, keepdims=True)
    first = jnp.min(jnp.where(x == m, iota, n), axis=0, keepdims=True)
    return m, iota == first


def _route_kernel(lg_ref, bias_ref, tri_ref, idx_ref, w_ref, rank_ref, cnt_ref, carry_ref):
    @pl.when(pl.program_id(0) == 0)
    def _():
        carry_ref[...] = jnp.zeros_like(carry_ref)

    t = lg_ref.shape[0]
    logits = jnp.transpose(lg_ref[...])[:N_EXPERTS]
    scores = 1.0 / (1.0 + jnp.exp(-logits))
    sel = scores + bias_ref[...]
    sub_g = lax.broadcasted_iota(jnp.int32, (PER_GROUP, t), 0)
    sub_e = lax.broadcasted_iota(jnp.int32, (N_EXPERTS, t), 0)
    grp_rows = []
    for g in range(N_GROUPS):
        x = sel[g * PER_GROUP:(g + 1) * PER_GROUP]
        m1, hit = _first_max(x, sub_g, PER_GROUP)
        m2 = jnp.max(jnp.where(hit, NEG_INF, x), axis=0, keepdims=True)
        grp_rows.append(m1 + m2)
    cur = jnp.concatenate(grp_rows, axis=0)
    sub_grp = lax.broadcasted_iota(jnp.int32, (N_GROUPS, t), 0)
    grp_on = jnp.zeros((N_GROUPS, t), F32)
    for _ in range(TOPK_GROUPS):
        _, hit = _first_max(cur, sub_grp, N_GROUPS)
        grp_on = jnp.where(hit, 1.0, grp_on)
        cur = jnp.where(hit, NEG_INF, cur)
    exp_on = jnp.concatenate([jnp.broadcast_to(grp_on[g:g + 1], (PER_GROUP, t)) for g in range(N_GROUPS)], axis=0)
    cur = jnp.where(exp_on > 0.0, sel, NEG_INF)
    hits, idx_rows = [], []
    for _ in range(TOP_K):
        _, hit = _first_max(cur, sub_e, N_EXPERTS)
        hits.append(hit)
        idx_rows.append(jnp.sum(jnp.where(hit, sub_e, 0), axis=0, keepdims=True))
        cur = jnp.where(hit, NEG_INF, cur)
    w_rows = [jnp.sum(jnp.where(hit, scores, 0.0), axis=0, keepdims=True) for hit in hits]
    total = functools.reduce(lambda a, b: a + b, w_rows)
    w_ref[...] = jnp.concatenate([w / total * ROUTED_SCALE for w in w_rows], axis=0)
    idx_ref[...] = jnp.concatenate(idx_rows, axis=0)
    chosen = functools.reduce(lambda a, b: a + b, [jnp.where(hit, 1.0, 0.0) for hit in hits])
    incl = jnp.dot(chosen.astype(BF16), tri_ref[...], preferred_element_type=F32)
    before = incl - chosen + carry_ref[:, :1]
    rank_ref[...] = jnp.concatenate(
        [jnp.sum(jnp.where(hit, before, 0.0), axis=0, keepdims=True) for hit in hits], axis=0).astype(jnp.int32)
    carry_ref[...] = carry_ref[...] + incl[:, t - 1:t]
    cnt_ref[...] = carry_ref[...]


def _route(logits, router_bias):
    n = logits.shape[0]
    t = ROUTE_T
    tri = (jnp.arange(t)[:, None] <= jnp.arange(t)[None, :]).astype(BF16)
    bias_b = jnp.broadcast_to(router_bias.astype(F32)[:, None], (N_EXPERTS, t))
    slot = pl.BlockSpec((TOP_K, t), lambda i: (0, i))
    idx, w, rank, cnt = pl.pallas_call(
        _route_kernel,
        grid=(n // t,),
        in_specs=[pl.BlockSpec((t, LANES), lambda i: (i, 0)),
                  pl.BlockSpec((N_EXPERTS, t), lambda i: (0, 0)),
                  pl.BlockSpec((t, t), lambda i: (0, 0))],
        out_specs=[slot, slot, slot, pl.BlockSpec((N_EXPERTS, LANES), lambda i: (0, 0))],
        out_shape=[jax.ShapeDtypeStruct((TOP_K, n), jnp.int32), jax.ShapeDtypeStruct((TOP_K, n), F32),
                   jax.ShapeDtypeStruct((TOP_K, n), jnp.int32), jax.ShapeDtypeStruct((N_EXPERTS, LANES), F32)],
        scratch_shapes=[pltpu.VMEM((N_EXPERTS, LANES), F32)],
        compiler_params=_cparams("arbitrary"),
    )(logits, bias_b, tri)
    return idx, w, rank, cnt[:, 0].astype(jnp.int32)


def _experts_kernel(be_ref, bv_ref, x_ref, w1_ref, w3_ref, w2_ref, o_ref, w1b, w3b, w2b):
    i = pl.program_id(0)
    changed = jnp.logical_or(i == 0, be_ref[i] != be_ref[jnp.maximum(i - 1, 0)])

    @pl.when(changed)
    def _():
        w1b[...] = w1_ref[0, 0].astype(BF16)
        w3b[...] = w3_ref[0, 0].astype(BF16)
        w2b[...] = w2_ref[0, 0].astype(BF16)

    live = lax.broadcasted_iota(jnp.int32, (MOE_BLOCK, 1), 0) < bv_ref[i]
    xb = _unpack_bf16_pairs(jnp.where(live, x_ref[...], 0)).astype(BF16)
    hid = _silu(jnp.dot(xb, w1b[...], preferred_element_type=F32)) * \
        jnp.dot(xb, w3b[...], preferred_element_type=F32)
    o_ref[...] = _pack_bf16_pairs(jnp.dot(hid.astype(BF16), w2b[...], preferred_element_type=F32))


def _grouped_experts(xs, block_expert, block_valid, w1, w3, w2, layer):
    rows = xs.shape[0]
    n_blocks = rows // MOE_BLOCK
    grid_spec = pltpu.PrefetchScalarGridSpec(
        num_scalar_prefetch=2,
        grid=(n_blocks,),
        in_specs=[pl.BlockSpec((MOE_BLOCK, D_PACK), lambda i, be, bv: (i, 0)),
                  pl.BlockSpec((1, 1, D_MODEL, D_EXPERT), lambda i, be, bv: (layer, be[i], 0, 0)),
                  pl.BlockSpec((1, 1, D_MODEL, D_EXPERT), lambda i, be, bv: (layer, be[i], 0, 0)),
                  pl.BlockSpec((1, 1, D_EXPERT, D_MODEL), lambda i, be, bv: (layer, be[i], 0, 0))],
        out_specs=pl.BlockSpec((MOE_BLOCK, D_PACK), lambda i, be, bv: (i, 0)),
        scratch_shapes=[pltpu.VMEM((D_MODEL, D_EXPERT), BF16), pltpu.VMEM((D_MODEL, D_EXPERT), BF16),
                        pltpu.VMEM((D_EXPERT, D_MODEL), BF16)])
    return pl.pallas_call(
        _experts_kernel,
        grid_spec=grid_spec,
        out_shape=jax.ShapeDtypeStruct((rows, D_PACK), jnp.int32),
        compiler_params=_cparams("arbitrary"),
    )(block_expert, block_valid, xs, w1, w3, w2)


SC_CORES = 2
SC_SUBCORES = 16
SC_WORKERS = SC_CORES * SC_SUBCORES
SC_ROWS = 64


def _sc_mesh():
    return plsc.VectorSubcoreMesh(core_axis_name="c", subcore_axis_name="s",
                                  num_cores=SC_CORES, num_subcores=SC_SUBCORES)


def _sc_dispatch(table, dest, rows):
    n, d = table.shape
    kk = dest.shape[0]
    per_worker = n // SC_WORKERS
    n_chunks = per_worker // SC_ROWS
    assert per_worker * SC_WORKERS == n and n_chunks * SC_ROWS == per_worker
    idx = dest.reshape(kk, SC_WORKERS, n_chunks, SC_ROWS).transpose(1, 2, 0, 3).reshape(
        SC_WORKERS, n_chunks * kk, SC_ROWS)

    @functools.partial(
        pl.kernel, mesh=_sc_mesh(),
        out_type=jax.ShapeDtypeStruct((rows, d), table.dtype),
        scratch_types=[pltpu.VMEM((n_chunks * kk, SC_ROWS), jnp.int32),
                       pltpu.VMEM((SC_ROWS, d), table.dtype),
                       pltpu.SemaphoreType.DMA])
    def dispatch(table_hbm, idx_hbm, out_hbm, idx_v, rows_v, sem):
        wid = lax.axis_index("s") * SC_CORES + lax.axis_index("c")
        base = wid * per_worker
        pltpu.sync_copy(idx_hbm.at[wid], idx_v)

        @pl.loop(0, n_chunks)
        def _(j):
            pltpu.sync_copy(table_hbm.at[pl.ds(base + j * SC_ROWS, SC_ROWS)], rows_v)
            copies = [pltpu.async_copy(rows_v, out_hbm.at[idx_v.at[j * kk + k]], sem) for k in range(kk)]
            for cp in copies:
                cp.wait()

    return dispatch(table, idx)


def _sc_gather(table, idx):
    b = idx.shape[0]
    d = table.shape[1]
    per_worker = b // SC_WORKERS
    n_chunks = per_worker // SC_ROWS
    assert per_worker * SC_WORKERS == b and n_chunks * SC_ROWS == per_worker

    @functools.partial(
        pl.kernel, mesh=_sc_mesh(),
        out_type=jax.ShapeDtypeStruct((b, d), table.dtype),
        scratch_types=[pltpu.VMEM((n_chunks, SC_ROWS), jnp.int32),
                       pltpu.VMEM((SC_ROWS, d), table.dtype),
                       pltpu.SemaphoreType.DMA])
    def gather(table_hbm, idx_hbm, out_hbm, idx_v, rows_v, sem):
        wid = lax.axis_index("s") * SC_CORES + lax.axis_index("c")
        base = wid * per_worker
        pltpu.sync_copy(idx_hbm.at[wid], idx_v)

        @pl.loop(0, n_chunks)
        def _(j):
            pltpu.async_copy(table_hbm.at[idx_v.at[j]], rows_v, sem).wait()
            pltpu.sync_copy(rows_v, out_hbm.at[pl.ds(base + j * SC_ROWS, SC_ROWS)])

    return gather(table, idx.reshape(SC_WORKERS, n_chunks, SC_ROWS))


POST_TM = 256


def _ffn_post_kernel(x_ref, yg_ref, w_ref, s_ref, gate_ref, g_ref, b_ref, o_ref):
    w = w_ref[...]
    routed = _unpack_bf16_pairs(yg_ref[0]) * w[:, 0:1]
    for k in range(1, TOP_K):
        routed = routed + _unpack_bf16_pairs(yg_ref[k]) * w[:, k:k + 1]
    r = DEEPNORM_ALPHA * x_ref[...] + gate_ref[0] * (routed + s_ref[...])
    o_ref[...] = _layer_norm_rows(r, g_ref[...], b_ref[...])


def _ffn_post(x, yg, w_tok, shared, mod3, ln_g, ln_b):
    n = x.shape[0]
    tm = POST_TM
    row = lambda i: _mod_row_of_tile(i, tm)
    tile = pl.BlockSpec((tm, D_MODEL), lambda i: (i, 0))
    vec = pl.BlockSpec((1, D_MODEL), lambda i: (0, 0))
    return pl.pallas_call(
        _ffn_post_kernel,
        grid=(n // tm,),
        in_specs=[tile, pl.BlockSpec((TOP_K, tm, D_PACK), lambda i: (0, i, 0)),
                  pl.BlockSpec((tm, TOP_K), lambda i: (i, 0)), tile,
                  pl.BlockSpec((1, 1, D_MODEL), lambda i: (row(i), 0, 5)), vec, vec],
        out_specs=tile,
        out_shape=jax.ShapeDtypeStruct((n, D_MODEL), F32),
        compiler_params=_cparams("parallel"),
    )(x, yg, w_tok, shared, mod3, ln_g.reshape(1, D_MODEL), ln_b.reshape(1, D_MODEL))


def _moe_routed(h, idx, rank, counts, w1, w3, w2, layer):
    n = h.shape[0]
    padded = (counts + MOE_BLOCK - 1) // MOE_BLOCK * MOE_BLOCK
    pad_end = jnp.cumsum(padded)
    pad_start = pad_end - padded
    experts = jnp.arange(N_EXPERTS, dtype=jnp.int32)
    dest = jnp.sum(jnp.where(idx[:, :, None] == experts, pad_start, 0), axis=-1) + rank
    n_blocks = n * TOP_K // MOE_BLOCK + N_EXPERTS
    rows = n_blocks * MOE_BLOCK
    block_start = jnp.arange(n_blocks, dtype=jnp.int32) * MOE_BLOCK
    block_expert = jnp.minimum(jnp.sum(pad_end[None, :] <= block_start[:, None], axis=1), N_EXPERTS - 1).astype(jnp.int32)
    of_block = block_expert[:, None] == experts
    used = jnp.sum(jnp.where(of_block, counts, 0), axis=1) - (block_start - jnp.sum(jnp.where(of_block, pad_start, 0), axis=1))
    block_valid = jnp.clip(used, 0, MOE_BLOCK).astype(jnp.int32)
    xs = _sc_dispatch(h, dest, rows)
    ys = _grouped_experts(xs, block_expert, block_valid, w1, w3, w2, layer)
    return _sc_gather(ys, dest.reshape(-1)).reshape(TOP_K, n, D_PACK)


def _even_layer(x, mod3, j, w_in, w_out, q_norm, k_norm, log_decay, gn_g,
                cache_k, cache_v, state_ret):
    (p,) = _modulated_proj(x, mod3, 0, w_in.astype(BF16), [EVEN_IN], [F32])
    q, k = _attn_prep(p, q_norm, k_norm)
    v = p[:, A_Q + A_KV:A_Q + 2 * A_KV]
    k_p = k[:N_P].reshape(BATCH, SEQ, A_KV)
    v_p = v[:N_P].reshape(BATCH, SEQ, A_KV)
    o_p = _attention(q, k_p, v_p, 0, BATCH, SEQ)
    k_all = jnp.concatenate([k[N_P:].reshape(DEC_BATCH, DEC_SEQ, A_KV),
                             cache_k.reshape(DEC_BATCH, PAST_LEN, A_KV)], axis=1)
    v_all = jnp.concatenate([v[N_P:].reshape(DEC_BATCH, DEC_SEQ, A_KV),
                             cache_v.reshape(DEC_BATCH, PAST_LEN, A_KV)], axis=1)
    o_s = _attention(q, k_all, v_all, N_P // DEC_SEQ, DEC_BATCH, DEC_SEQ)
    o_attn = jnp.concatenate([o_p, o_s], axis=0)
    lg_rows = jnp.broadcast_to(log_decay.reshape(2 * H_B, 1), (2 * H_B, LANES))
    zeros_s = jnp.zeros((BATCH, 2, H_B, DK_B, DV_B), F32)
    r_p, s_p = _retention(p, 0, BATCH, SEQ, lg_rows, zeros_s, gn_g)
    r_s, _ = _retention(p, N_P // DEC_SEQ, DEC_BATCH, DEC_SEQ, lg_rows, state_ret, gn_g)
    o_ret = jnp.concatenate([r_p, r_s], axis=0)
    return ([o_attn, o_ret], w_out.astype(BF16), k_p.reshape(BATCH, SEQ, N_KV_A, HD_A),
            v_p.reshape(BATCH, SEQ, N_KV_A, HD_A), s_p)


def _chunk_cumsum(g, batch, t):
    gc = g.reshape(batch, t // GC, GC, 2, H_C)
    f = jnp.cumsum(gc[:, :, :, 0], axis=2)
    b = jnp.cumsum(gc[:, :, ::-1, 1], axis=2)[:, :, ::-1]
    return jnp.stack([f, b], axis=3)


def _odd_layer(x, mod3, j, w_in, conv_w, a_log, dt_bias, norm_g, w_out, state_gdn):
    w_main = w_in[:, :2 * C_QK + 2 * C_V].astype(BF16)
    w_ab = jnp.pad(w_in[:, 2 * C_QK + 2 * C_V:], ((0, 0), (0, LANES - 4 * H_C))).astype(BF16)
    w_cat = jnp.concatenate([w_main, w_ab], axis=1)
    pm, pab = _modulated_proj(x, mod3, 0, w_cat, [2 * C_QK + 2 * C_V, LANES], [F32, F32])
    qkv_p = _gdn_prep(pm, conv_w, 0, BATCH, SEQ)
    qkv_s = _gdn_prep(pm, conv_w, N_P // DEC_SEQ, DEC_BATCH, DEC_SEQ)
    ab = pab[:, :4 * H_C].reshape(N_TOK, 2, 2, H_C)
    beta = jax.nn.sigmoid(ab[:, 0])
    g = -jnp.exp(a_log.astype(F32)) * jax.nn.softplus(ab[:, 1] + dt_bias.astype(F32))

    gcs_p = _chunk_cumsum(g[:N_P], BATCH, SEQ)
    gcs_s = _chunk_cumsum(g[N_P:], DEC_BATCH, DEC_SEQ)
    gcol = jnp.concatenate([gcs_p.reshape(N_P, 2 * H_C), gcs_s.reshape(N_S, 2 * H_C)], axis=0)
    gbeta = jnp.concatenate([beta.reshape(N_TOK, 2 * H_C), gcol], axis=1)
    zeros_s = jnp.zeros((BATCH, 2, H_C, DK_C, DV_C), F32)
    o_p, s_p = _gdn(qkv_p, pm, gbeta, gcs_p.transpose(0, 3, 4, 1, 2), 0, BATCH, SEQ, zeros_s, norm_g)
    o_s, _ = _gdn(qkv_s, pm, gbeta, gcs_s.transpose(0, 3, 4, 1, 2), N_P // DEC_SEQ, DEC_BATCH, DEC_SEQ,
                  state_gdn, norm_g)
    return [jnp.concatenate([o_p, o_s], axis=0)], w_out.astype(BF16), s_p


def kernel(x_prompt, x_sample, cache_attn_k, cache_attn_v, state_ret, state_gdn, c, c_ctx, mod_w, mod_b, ln_g, ln_b, even_w_in, even_w_out, attn_q_norm, attn_k_norm, ret_log_decay, ret_norm_g, odd_w_in, gdn_conv_w, gdn_a_log, gdn_dt_bias, gdn_norm_g, odd_w_out, router_w, router_bias, expert_w1, expert_w3, expert_w2, shared_w1, shared_w3, shared_w2):
    x = jnp.concatenate([x_prompt.reshape(N_P, D_MODEL), x_sample.reshape(N_S, D_MODEL)], axis=0)
    cvec = jnp.concatenate([c_ctx[None, :], c, jnp.zeros((MOD_ROWS - N_MOD, D_MODEL), F32)], axis=0)
    mod_all = _mod_vectors(cvec, mod_w, mod_b)
    new_k, new_v, new_ret, new_gdn = [], [], [], []
    for l in range(DEPTH):
        j = l // 2
        mod3 = mod_all[l].reshape(MOD_ROWS, 1, 6 * D_MODEL)
        if l % 2 == 0:
            a_list, w_o, k_p, v_p, s_p = _even_layer(
                x, mod3, j, even_w_in[j], even_w_out[j], attn_q_norm[j], attn_k_norm[j],
                ret_log_decay[j], ret_norm_g[j], cache_attn_k[:, j], cache_attn_v[:, j], state_ret[:, j])
            new_k.append(k_p)
            new_v.append(v_p)
            new_ret.append(s_p)
        else:
            a_list, w_o, s_p = _odd_layer(
                x, mod3, j, odd_w_in[j], gdn_conv_w[j], gdn_a_log[j], gdn_dt_bias[j], gdn_norm_g[j],
                odd_w_out[j], state_gdn[:, j])
            new_gdn.append(s_p)
        x = _outproj_ln(a_list, w_o, x, mod3, 2, ln_g[l, 0], ln_b[l, 0])
        rw = jnp.pad(router_w[l], ((0, 0), (0, LANES - N_EXPERTS)))
        ws13 = jnp.concatenate([shared_w1[l], shared_w3[l]], axis=1).astype(BF16)
        h, logits, shared = _ffn_pre(x, mod3, rw, ws13, shared_w2[l].astype(BF16))
        idx, w, rank, counts = _route(logits, router_bias[l])
        yg = _moe_routed(h, idx, rank, counts, expert_w1, expert_w3, expert_w2, l)
        x = _ffn_post(x, yg, w.T, shared, mod3, ln_g[l, 1], ln_b[l, 1])
    return (x[:N_P].reshape(BATCH, SEQ, D_MODEL), x[N_P:].reshape(DEC_BATCH, DEC_SEQ, D_MODEL),
            jnp.stack(new_k, axis=1), jnp.stack(new_v, axis=1),
            jnp.stack(new_ret, axis=1), jnp.stack(new_gdn, axis=1))
```

```python
import functools
import math

import jax
import jax.numpy as jnp
from jax import lax
from jax.experimental import pallas as pl
from jax.experimental.pallas import tpu as pltpu
from jax.experimental.pallas import tpu_sc as plsc

D_MODEL = 1024
BATCH = 16
SEQ = 256
DEPTH = 4
DEC_BATCH = 4
DEC_SEQ = 4096
PAST_LEN = 512
GRID_W = 64
N_HEADS_A = 8
N_KV_A = 2
HD_A = 64
ROPE_THETA = 10000.0
H_B = 4
DK_B = 64
DV_B = 128
RET_CHUNK = 128
H_C = 8
DK_C = 128
DV_C = 128
CONV_K = 5
GDN_CHUNK = 64
N_EXPERTS = 64
TOP_K = 8
N_GROUPS = 8
TOPK_GROUPS = 4
D_EXPERT = 256
D_SHARED = 256
ROUTED_SCALE = 2.5
MOE_BLOCK = 512
A_Q = N_HEADS_A * HD_A
A_KV = N_KV_A * HD_A
B_QK = H_B * DK_B
B_V = H_B * DV_B
EVEN_IN = A_Q + 2 * A_KV + 2 * B_QK + 2 * B_V
C_QK = H_C * DK_C
C_V = H_C * DV_C
DEEPNORM_ALPHA = (2 * DEPTH) ** 0.25
EPS = 1e-6

N_P = BATCH * SEQ
N_S = DEC_BATCH * DEC_SEQ
N_TOK = N_P + N_S
N_MOD = 1 + DEC_BATCH
MOD_ROWS = 8

LANES = 128
VMEM_LIMIT = 56 * 1024 * 1024

F32 = jnp.float32
BF16 = jnp.bfloat16


def _cparams(*sem):
    return pltpu.CompilerParams(dimension_semantics=sem, vmem_limit_bytes=VMEM_LIMIT)


def _bdot(a, b):
    return jnp.dot(a.astype(BF16), b.astype(BF16), preferred_element_type=F32)


def _bdot_t(a, b):
    return lax.dot_general(a.astype(BF16), b.astype(BF16), (((0,), (0,)), ((), ())),
                           preferred_element_type=F32)


def _bdot_nt(a, b):
    return lax.dot_general(a.astype(BF16), b.astype(BF16), (((1,), (1,)), ((), ())),
                           preferred_element_type=F32)


def _split3(a):
    hi = a.astype(BF16)
    r = a - hi.astype(F32)
    mid = r.astype(BF16)
    lo = (r - mid.astype(F32)).astype(BF16)
    return hi, mid, lo


def _dot_hp(a, b):
    a0, a1, a2 = _split3(a)
    b0, b1, b2 = _split3(b)
    d = lambda x, y: jnp.dot(x, y, preferred_element_type=F32)
    small = d(a0, b2) + d(a2, b0) + d(a1, b1)
    return (d(a0, b1) + d(a1, b0)) + small + d(a0, b0)


def _silu(x):
    return x * (1.0 / (1.0 + jnp.exp(-x)))


HI16 = 0xFFFF0000
D_PACK = D_MODEL // 2


def _pack_bf16_pairs(x):
    c = x.shape[1] // 2
    lo = pltpu.bitcast(x[:, :c].astype(BF16).astype(F32), jnp.uint32) >> 16
    hi = pltpu.bitcast(x[:, c:].astype(BF16).astype(F32), jnp.uint32) & jnp.uint32(HI16)
    return pltpu.bitcast(lo | hi, jnp.int32)


def _unpack_bf16_pairs(p):
    u = pltpu.bitcast(p, jnp.uint32)
    return jnp.concatenate([pltpu.bitcast(u << 16, F32), pltpu.bitcast(u & jnp.uint32(HI16), F32)], axis=1)


def _mod_row_of_tile(i, tile):
    tiles_p = N_P // tile
    tiles_per_b = DEC_SEQ // tile
    return jnp.where(i < tiles_p, 0, 1 + (i - tiles_p) // tiles_per_b)


MOD_TN = 1536


def _mod_kernel(c_ref, w_ref, b_ref, o_ref):
    a = _silu(c_ref[...])
    o_ref[0] = _bdot(a, w_ref[0]) + b_ref[0]


def _mod_vectors(cvec, mod_w, mod_b):
    n6 = 6 * D_MODEL
    return pl.pallas_call(
        _mod_kernel,
        grid=(DEPTH, n6 // MOD_TN),
        in_specs=[pl.BlockSpec((MOD_ROWS, D_MODEL), lambda l, j: (0, 0)),
                  pl.BlockSpec((1, D_MODEL, MOD_TN), lambda l, j: (l, 0, j)),
                  pl.BlockSpec((1, 1, MOD_TN), lambda l, j: (l, 0, j))],
        out_specs=pl.BlockSpec((1, MOD_ROWS, MOD_TN), lambda l, j: (l, 0, j)),
        out_shape=jax.ShapeDtypeStruct((DEPTH, MOD_ROWS, n6), F32),
        compiler_params=_cparams("parallel", "parallel"),
    )(cvec, mod_w, mod_b.reshape(DEPTH, 1, n6))


PROJ_TM = 256


def _proj_kernel(x_ref, shift_ref, scale_ref, w_ref, *o_refs, widths):
    h = (x_ref[...] * (1.0 + scale_ref[0]) + shift_ref[0]).astype(BF16)
    off = 0
    for o_ref, wd in zip(o_refs, widths):
        o_ref[...] = jnp.dot(h, w_ref[:, off:off + wd], preferred_element_type=F32).astype(o_ref.dtype)
        off += wd


def _modulated_proj(x, mod3, shift_blk, w_bf16, widths, dtypes):
    n = x.shape[0]
    tm = PROJ_TM
    row = lambda i: _mod_row_of_tile(i, tm)
    return pl.pallas_call(
        functools.partial(_proj_kernel, widths=tuple(widths)),
        grid=(n // tm,),
        in_specs=[pl.BlockSpec((tm, D_MODEL), lambda i: (i, 0)),
                  pl.BlockSpec((1, 1, D_MODEL), lambda i: (row(i), 0, shift_blk)),
                  pl.BlockSpec((1, 1, D_MODEL), lambda i: (row(i), 0, shift_blk + 1)),
                  pl.BlockSpec((D_MODEL, sum(widths)), lambda i: (0, 0))],
        out_specs=[pl.BlockSpec((tm, wd), lambda i: (i, 0)) for wd in widths],
        out_shape=[jax.ShapeDtypeStruct((n, wd), dt) for wd, dt in zip(widths, dtypes)],
        compiler_params=_cparams("parallel"),
    )(x, mod3, mod3, w_bf16)


OUT_TM = 512


def _layer_norm_rows(r, g, b):
    mu = jnp.mean(r, axis=-1, keepdims=True)
    d = r - mu
    var = jnp.mean(d * d, axis=-1, keepdims=True)
    return d * lax.rsqrt(var + EPS) * g + b


def _outproj_kernel(*refs, n_a):
    a_refs = refs[:n_a]
    w_ref, x_ref, gate_ref, g_ref, b_ref, o_ref = refs[n_a:]
    off = 0
    acc = None
    for a_ref in a_refs:
        wd = a_ref.shape[1]
        part = jnp.dot(a_ref[...].astype(BF16), w_ref[off:off + wd, :], preferred_element_type=F32)
        acc = part if acc is None else acc + part
        off += wd
    r = DEEPNORM_ALPHA * x_ref[...] + gate_ref[0] * acc
    o_ref[...] = _layer_norm_rows(r, g_ref[...], b_ref[...])


def _outproj_ln(a_list, w_bf16, x, mod3, gate_blk, ln_g, ln_b):
    n = x.shape[0]
    tm = OUT_TM
    row = lambda i: _mod_row_of_tile(i, tm)
    kdim = w_bf16.shape[0]
    return pl.pallas_call(
        functools.partial(_outproj_kernel, n_a=len(a_list)),
        grid=(n // tm,),
        in_specs=[pl.BlockSpec((tm, a.shape[1]), lambda i: (i, 0)) for a in a_list] + [
            pl.BlockSpec((kdim, D_MODEL), lambda i: (0, 0)),
            pl.BlockSpec((tm, D_MODEL), lambda i: (i, 0)),
            pl.BlockSpec((1, 1, D_MODEL), lambda i: (row(i), 0, gate_blk)),
            pl.BlockSpec((1, D_MODEL), lambda i: (0, 0)),
            pl.BlockSpec((1, D_MODEL), lambda i: (0, 0))],
        out_specs=pl.BlockSpec((tm, D_MODEL), lambda i: (i, 0)),
        out_shape=jax.ShapeDtypeStruct((n, D_MODEL), F32),
        compiler_params=_cparams("parallel"),
    )(*a_list, w_bf16, x, mod3, ln_g.reshape(1, D_MODEL), ln_b.reshape(1, D_MODEL))


PREP_TM = 256
ROPE_SEG = HD_A // 4


def _head_rms(x, gain, ones_bd):
    parts = _split3(x * x)
    ss = functools.reduce(lambda a, b: a + b,
                          [jnp.dot(p, ones_bd, preferred_element_type=F32) for p in reversed(parts)])
    return x * lax.rsqrt(ss * (1.0 / HD_A) + EPS) * gain


def _rope_lanes(x, cos, sin):
    lane = lax.broadcasted_iota(jnp.int32, (1, LANES), 1)
    first = (lane % (2 * ROPE_SEG)) < ROPE_SEG
    cols = []
    for b in range(x.shape[1] // LANES):
        xb = x[:, b * LANES:(b + 1) * LANES]
        partner = jnp.where(first, pltpu.roll(xb, LANES - ROPE_SEG, axis=1), pltpu.roll(xb, ROPE_SEG, axis=1))
        cols.append(xb * cos + partner * sin)
    return cols[0] if len(cols) == 1 else jnp.concatenate(cols, axis=1)


def _attn_prep_kernel(q_ref, kv_ref, cos_ref, sin_ref, qg_ref, kg_ref, bd_ref, qo_ref, ko_ref):
    cos = cos_ref[...]
    sin = sin_ref[...]
    q = _rope_lanes(_head_rms(q_ref[...], qg_ref[...], bd_ref[...]), cos, sin)
    qo_ref[...] = (q * (HD_A ** -0.5)).astype(BF16)
    k = _head_rms(kv_ref[:, :A_KV], kg_ref[...], bd_ref[:A_KV, :A_KV])
    ko_ref[...] = _rope_lanes(k, cos, sin)


def _rope_tables():
    rows = DEC_SEQ // GRID_W
    row = jnp.repeat(jnp.arange(rows), GRID_W).astype(F32)
    colp = jnp.tile(jnp.arange(GRID_W), rows).astype(F32)
    inv_freq = ROPE_THETA ** (-jnp.arange(ROPE_SEG, dtype=F32) / ROPE_SEG)
    ar = row[:, None] * inv_freq[None, :]
    ac = colp[:, None] * inv_freq[None, :]
    cos = jnp.concatenate([jnp.cos(ar), jnp.cos(ar), jnp.cos(ac), jnp.cos(ac)], axis=-1)
    sin = jnp.concatenate([-jnp.sin(ar), jnp.sin(ar), -jnp.sin(ac), jnp.sin(ac)], axis=-1)
    cos = jnp.concatenate([jnp.ones((PREP_TM, HD_A), F32), cos], axis=0)
    sin = jnp.concatenate([jnp.zeros((PREP_TM, HD_A), F32), sin], axis=0)
    return jnp.tile(cos, (1, LANES // HD_A)), jnp.tile(sin, (1, LANES // HD_A))


def _attn_prep(p, q_norm, k_norm):
    tm = PREP_TM
    tiles_p = N_P // tm
    tiles_seq = DEC_SEQ // tm
    tab = lambda i: (jnp.where(i < tiles_p, 0, 1 + (i - tiles_p) % tiles_seq), 0)
    cos, sin = _rope_tables()
    head_id = jnp.arange(A_Q) // HD_A
    ones_bd = (head_id[:, None] == head_id[None, :]).astype(BF16)
    return pl.pallas_call(
        _attn_prep_kernel,
        grid=(N_TOK // tm,),
        in_specs=[pl.BlockSpec((tm, A_Q), lambda i: (i, 0)),
                  pl.BlockSpec((tm, 2 * A_KV), lambda i: (i, A_Q // (2 * A_KV))),
                  pl.BlockSpec((tm, LANES), tab), pl.BlockSpec((tm, LANES), tab),
                  pl.BlockSpec((1, A_Q), lambda i: (0, 0)), pl.BlockSpec((1, A_KV), lambda i: (0, 0)),
                  pl.BlockSpec((A_Q, A_Q), lambda i: (0, 0))],
        out_specs=[pl.BlockSpec((tm, A_Q), lambda i: (i, 0)), pl.BlockSpec((tm, A_KV), lambda i: (i, 0))],
        out_shape=[jax.ShapeDtypeStruct((N_TOK, A_Q), BF16), jax.ShapeDtypeStruct((N_TOK, A_KV), F32)],
        compiler_params=_cparams("parallel"),
    )(p, p, cos, sin, jnp.tile(q_norm, N_HEADS_A).reshape(1, A_Q), jnp.tile(k_norm, N_KV_A).reshape(1, A_KV),
      ones_bd)


ATT_TQ = 256
GROUP_A = N_HEADS_A // N_KV_A


def _attn_kernel(q_ref, kt_ref, kts_ref, v_ref, vs_ref, o_ref):
    low = lax.broadcasted_iota(jnp.int32, (1, LANES), 1) < HD_A
    for j in range(N_HEADS_A // 2):
        qb = q_ref[:, j * LANES:(j + 1) * LANES]
        outs = []
        for half in range(2):
            kv_head = (2 * j + half) // GROUP_A
            qh = jnp.where(low if half == 0 else jnp.logical_not(low), qb, jnp.zeros_like(qb))
            kt, v = (kt_ref[0], v_ref[0]) if half == kv_head else (kts_ref[0], vs_ref[0])
            s = jnp.dot(qh, kt, preferred_element_type=F32)
            m = jnp.max(s, axis=-1, keepdims=True)
            p = jnp.exp(s - m)
            l = jnp.sum(p, axis=-1, keepdims=True)
            pv = jnp.dot(p.astype(BF16), v, preferred_element_type=F32)
            outs.append(pv * (1.0 / l))
        o_ref[:, j * LANES:(j + 1) * LANES] = jnp.where(low, outs[0], outs[1]).astype(o_ref.dtype)


def _attention(q, k, v, row_blk0, batch, t):
    tk = v.shape[1]
    tq = min(ATT_TQ, t)
    nq = t // tq
    swap = lambda x: jnp.roll(x, HD_A, axis=-1)
    kt = jnp.swapaxes(k, 1, 2).astype(BF16)
    kts = jnp.swapaxes(swap(k), 1, 2).astype(BF16)
    kspec = pl.BlockSpec((1, LANES, tk), lambda b, i: (b, 0, 0))
    vspec = pl.BlockSpec((1, tk, LANES), lambda b, i: (b, 0, 0))
    return pl.pallas_call(
        _attn_kernel,
        grid=(batch, nq),
        in_specs=[pl.BlockSpec((tq, A_Q), lambda b, i: ((row_blk0 + b) * nq + i, 0)), kspec, kspec, vspec, vspec],
        out_specs=pl.BlockSpec((tq, A_Q), lambda b, i: (b * nq + i, 0)),
        out_shape=jax.ShapeDtypeStruct((batch * t, A_Q), BF16),
        compiler_params=_cparams("parallel", "parallel"),
    )(q, kt, kts, v.astype(BF16), swap(v).astype(BF16))


RC = RET_CHUNK


def _ret_kernel(q_ref, k_ref, v_ref, g_ref, lg_ref, s0_ref, gn_ref, o_ref, s_out_ref,
                ob_ref, st_ref, *, t):
    nc = t // RC
    hp = pl.program_id(1)
    ii = lax.broadcasted_iota(jnp.int32, (RC, RC), 0).astype(F32)
    jj = lax.broadcasted_iota(jnp.int32, (RC, RC), 1).astype(F32)
    col_i = lax.broadcasted_iota(jnp.int32, (RC, 1), 0).astype(F32)
    lane = lax.broadcasted_iota(jnp.int32, (1, LANES), 1)
    masks = [(lane >= hh * DK_B) & (lane < (hh + 1) * DK_B) for hh in range(2)]

    consts = []
    for d in range(2):
        for hh in range(2):
            lg = lg_ref[pl.ds(d * H_B + hp * 2 + hh, 1), :][:, :1]
            if d == 0:
                diff = ii - jj
                qe, ke = col_i + 1.0, (RC - 1.0) - col_i
            else:
                diff = jj - ii
                qe, ke = RC - col_i, col_i
            intra = jnp.where(diff >= 0, jnp.exp(jnp.maximum(diff, 0.0) * lg), 0.0)
            consts.append((intra, jnp.exp(qe * lg), jnp.exp(ke * lg), jnp.exp(RC * lg)))
            s0 = s0_ref[0, d, hh]
            z = jnp.zeros((DK_B, DV_B), F32)
            st_ref[d * 2 + hh] = jnp.concatenate([s0, z] if hh == 0 else [z, s0], axis=0)

    def body(c, carry):
        combos = [(d, hh) for d in range(2) for hh in range(2)]
        rows = [pl.ds(pl.multiple_of(cc * RC, RC), RC) for cc in (c, nc - 1 - c)]
        qcs = [q_ref[rows[d], :] for d in range(2)]
        kcs = [k_ref[rows[d], :] * (DK_B ** -0.5) for d in range(2)]
        qhs = [jnp.where(masks[hh], qcs[d], 0.0) for d, hh in combos]
        khs = [jnp.where(masks[hh], kcs[d], 0.0) for d, hh in combos]
        vcs = [v_ref[rows[d], hh * DV_B:(hh + 1) * DV_B].astype(BF16) for d, hh in combos]
        sts = [st_ref[d * 2 + hh] for d, hh in combos]
        scores = [_bdot_nt(qhs[i], khs[i]) * consts[i][0] for i in range(4)]
        kvs = [_bdot_t(khs[i] * consts[i][2], vcs[i]) for i in range(4)]
        outs = [jnp.dot(jnp.concatenate([scores[i], qhs[i] * consts[i][1]], axis=1).astype(BF16),
                        jnp.concatenate([vcs[i], sts[i].astype(BF16)], axis=0),
                        preferred_element_type=F32) for i in range(4)]
        for i, (d, hh) in enumerate(combos):
            st_ref[d * 2 + hh] = sts[i] * consts[i][3] + kvs[i]
        o_ref[rows[0], :] = jnp.concatenate(outs[0:2], axis=1)
        ob_ref[rows[1], :] = jnp.concatenate(outs[2:4], axis=1)
        return carry

    lax.fori_loop(0, nc, body, 0)

    for d in range(2):
        for hh in range(2):
            s_out_ref[0, d, hh] = st_ref[d * 2 + hh][hh * DK_B:(hh + 1) * DK_B, :]

    def finish(c, carry):
        r0 = pl.multiple_of(c * RC, RC)
        o = o_ref[pl.ds(r0, RC), :] + ob_ref[pl.ds(r0, RC), :]
        gate = _silu(g_ref[pl.ds(r0, RC), :])
        ys = []
        for hh in range(2):
            oh = o[:, hh * DV_B:(hh + 1) * DV_B]
            mu = jnp.mean(oh, axis=-1, keepdims=True)
            dlt = oh - mu
            var = jnp.mean(dlt * dlt, axis=-1, keepdims=True)
            ys.append(dlt * lax.rsqrt(var + EPS))
        o_ref[pl.ds(r0, RC), :] = jnp.concatenate(ys, axis=1) * gn_ref[...] * gate
        return carry

    lax.fori_loop(0, nc, finish, 0)


def _retention(p, row_blk0, batch, t, lg_rows, s0, gn_g):
    qb, kb, vb, gb = (A_Q + 2 * A_KV) // LANES, (A_Q + 2 * A_KV + B_QK) // LANES, \
        (A_Q + 2 * A_KV + 2 * B_QK) // (2 * DV_B), (A_Q + 2 * A_KV + 2 * B_QK + B_V) // (2 * DV_B)
    return pl.pallas_call(
        functools.partial(_ret_kernel, t=t),
        grid=(batch, H_B // 2),
        in_specs=[pl.BlockSpec((t, LANES), lambda b, h: (row_blk0 + b, qb + h)),
                  pl.BlockSpec((t, LANES), lambda b, h: (row_blk0 + b, kb + h)),
                  pl.BlockSpec((t, 2 * DV_B), lambda b, h: (row_blk0 + b, vb + h)),
                  pl.BlockSpec((t, 2 * DV_B), lambda b, h: (row_blk0 + b, gb + h)),
                  pl.BlockSpec((2 * H_B, LANES), lambda b, h: (0, 0)),
                  pl.BlockSpec((1, 2, 2, DK_B, DV_B), lambda b, h: (b, 0, h, 0, 0)),
                  pl.BlockSpec((1, 2 * DV_B), lambda b, h: (0, h))],
        out_specs=[pl.BlockSpec((t, 2 * DV_B), lambda b, h: (b, h)),
                   pl.BlockSpec((1, 2, 2, DK_B, DV_B), lambda b, h: (b, 0, h, 0, 0))],
        out_shape=[jax.ShapeDtypeStruct((batch * t, B_V), F32),
                   jax.ShapeDtypeStruct((batch, 2, H_B, DK_B, DV_B), F32)],
        scratch_shapes=[pltpu.VMEM((t, 2 * DV_B), F32), pltpu.VMEM((4, LANES, DV_B), F32)],
        compiler_params=_cparams("parallel", "parallel"),
    )(p, p, p, p, lg_rows, s0, gn_g.reshape(1, B_V))


GC = 256


def _unit_tri_inverses(mats):
    ii = lax.broadcasted_iota(jnp.int32, (GC, GC), 0)
    jj = lax.broadcasted_iota(jnp.int32, (GC, GC), 1)
    block_dist = ii ^ jj
    eye = (ii == jj).astype(F32)
    ms = [jnp.where((block_dist >> 3) == 0, -a, 0.0) for a in mats]
    invs = [eye + m for m in ms]
    for _ in range(2):
        ms = [_bdot(m, m) for m in ms]
        invs = [inv + _bdot(inv, m) for inv, m in zip(invs, ms)]
    for shift in range(3, int(math.log2(GC))):
        ls = [jnp.where((block_dist >> shift) == 1, a, 0.0) for a in mats]
        ts = [_bdot(inv, l) for inv, l in zip(invs, ls)]
        invs = [inv - _bdot(t, inv) for inv, t in zip(invs, ts)]
    return invs


def _gdn_kernel(q_ref, k_ref, v_ref, z_ref, gb_ref, gr_ref, s0_ref, ng_ref, o_ref, s_out_ref,
                ob_ref, st_ref, wq_s, u_s, a_s, kg_s, *, t):
    nc = t // GC
    h = pl.program_id(1)
    ii = lax.broadcasted_iota(jnp.int32, (GC, GC), 0)
    jj = lax.broadcasted_iota(jnp.int32, (GC, GC), 1)
    incl = [ii >= jj, jj >= ii]
    strict = [ii > jj, jj > ii]
    lane32 = lax.broadcasted_iota(jnp.int32, (1, 4 * H_C), 1)
    for d in range(2):
        st_ref[d] = s0_ref[0, d, 0]

    def col(x, idx):
        return jnp.sum(jnp.where(lane32 == idx, x, 0.0), axis=1, keepdims=True)

    def gate_last(c, d):
        grow = gr_ref[0, d, 0, pl.ds(c, 1), :]
        return grow, (grow[:, GC - 1:GC] if d == 0 else grow[:, 0:1])

    def prep(cb, carry):
        mats, rhs, dst = [], [], []
        for ci in range(prep_chunks):
            c = cb * prep_chunks + ci
            rows = pl.ds(pl.multiple_of(c * GC, GC), GC)
            qc = q_ref[rows, :] * (DK_C ** -0.5)
            kc = k_ref[rows, :]
            vc = v_ref[rows, :]
            gb = gb_ref[rows, :]
            kk = _bdot_nt(kc, kc)
            qk = _bdot_nt(qc, kc)
            r2 = pl.multiple_of(c * 2 * GC, 2 * GC)
            for d in range(2):
                beta = col(gb, d * H_C + h)
                gcol = col(gb, 2 * H_C + d * H_C + h)
                grow, glast = gate_last(c, d)
                decay = jnp.exp(jnp.where(incl[d], gcol - grow, -jnp.inf))
                eg = jnp.exp(gcol)
                mats.append(jnp.where(strict[d], kk * beta * decay, 0.0))
                rhs.append(jnp.concatenate([kc * (beta * eg), vc * beta], axis=1).astype(BF16))
                dst.append((d, rows, r2))
                wq_s[d, pl.ds(r2 + GC, GC), :] = (qc * eg).astype(BF16)
                a_s[d, rows, :] = (qk * decay).astype(BF16)
                kg_s[d, rows, :] = (kc * jnp.exp(glast - gcol)).astype(BF16)
        wus = [jnp.dot(tinv.astype(BF16), r, preferred_element_type=F32)
               for tinv, r in zip(_unit_tri_inverses(mats), rhs)]
        for wu, (d, rows, r2) in zip(wus, dst):
            wq_s[d, pl.ds(r2, GC), :] = wu[:, :DK_C].astype(BF16)
            u_s[d, rows, :] = wu[:, DK_C:]
        return carry

    prep_chunks = 2 if nc % 2 == 0 else 1
    lax.fori_loop(0, nc // prep_chunks, prep, 0)

    def step(c, carry):
        ccs = [c, nc - 1 - c]
        rows = [pl.ds(pl.multiple_of(cc * GC, GC), GC) for cc in ccs]
        ss = [st_ref[d] for d in range(2)]
        sbs = [s.astype(BF16) for s in ss]
        wss = [jnp.dot(wq_s[d, pl.ds(pl.multiple_of(ccs[d] * 2 * GC, 2 * GC), 2 * GC), :], sbs[d],
                       preferred_element_type=F32) for d in range(2)]
        vnbs = [(u_s[d, rows[d], :] - wss[d][:GC]).astype(BF16) for d in range(2)]
        os_ = [wss[d][GC:] + jnp.dot(a_s[d, rows[d], :], vnbs[d], preferred_element_type=F32) for d in range(2)]
        kvs = [_bdot_t(kg_s[d, rows[d], :], vnbs[d]) for d in range(2)]
        for d in range(2):
            _, glast = gate_last(ccs[d], d)
            st_ref[d] = ss[d] * jnp.exp(glast) + kvs[d]
        o_ref[rows[0], :] = os_[0]
        ob_ref[rows[1], :] = os_[1]
        return carry

    lax.fori_loop(0, nc, step, 0)

    for d in range(2):
        s_out_ref[0, d, 0] = st_ref[d]

    def finish(c, carry):
        r0 = pl.multiple_of(c * GC, GC)
        o = o_ref[pl.ds(r0, GC), :] + ob_ref[pl.ds(r0, GC), :]
        y = o * lax.rsqrt(jnp.mean(o * o, axis=-1, keepdims=True) + EPS) * ng_ref[...]
        o_ref[pl.ds(r0, GC), :] = y * _silu(z_ref[pl.ds(r0, GC), :])
        return carry

    lax.fori_loop(0, nc, finish, 0)


GDN_PREP_BLOCK_ELEMS = 512 * 1024


def _gdn_prep_kernel(x_ref, w_ref, o_ref, *, t, heads):
    x = x_ref[...]
    row = lax.broadcasted_iota(jnp.int32, (t, 1), 0)
    centre = (CONV_K - 1) // 2
    y = x * w_ref[centre:centre + 1, :]
    for i in range(CONV_K):
        s = i - centre
        if s == 0:
            continue
        shifted = pltpu.roll(x, (-s) % t, axis=0)
        inside = jnp.logical_and(row + s >= 0, row + s < t)
        y = y + jnp.where(inside, shifted, 0.0) * w_ref[i:i + 1, :]
    y = _silu(y)
    for b in range(heads):
        yb = y[:, b * LANES:(b + 1) * LANES]
        inv_norm = lax.rsqrt(jnp.sum(yb * yb, axis=-1, keepdims=True) + EPS)
        is_qk = pl.program_id(1) * heads + b < 2 * H_C
        o_ref[:, b * LANES:(b + 1) * LANES] = yb * jnp.where(is_qk, inv_norm, 1.0)


def _gdn_prep(pm, conv_w, row_blk0, batch, t):
    heads = max(1, min(H_C, GDN_PREP_BLOCK_ELEMS // (t * LANES)))
    n_col = (2 * C_QK + C_V) // (heads * LANES)
    return pl.pallas_call(
        functools.partial(_gdn_prep_kernel, t=t, heads=heads),
        grid=(batch, n_col),
        in_specs=[pl.BlockSpec((t, heads * LANES), lambda b, j: (row_blk0 + b, j)),
                  pl.BlockSpec((CONV_K, heads * LANES), lambda b, j: (0, j))],
        out_specs=pl.BlockSpec((t, heads * LANES), lambda b, j: (b, j)),
        out_shape=jax.ShapeDtypeStruct((batch * t, 2 * C_QK + C_V), F32),
        compiler_params=_cparams("parallel", "parallel"),
    )(pm, conv_w)


def _gdn(qkv, pm, gbeta, grow, row_blk0, batch, t, s0, norm_g):
    nc = t // GC
    return pl.pallas_call(
        functools.partial(_gdn_kernel, t=t),
        grid=(batch, H_C),
        in_specs=[pl.BlockSpec((t, DK_C), lambda b, h: (b, h)),
                  pl.BlockSpec((t, DK_C), lambda b, h: (b, H_C + h)),
                  pl.BlockSpec((t, DV_C), lambda b, h: (b, 2 * H_C + h)),
                  pl.BlockSpec((t, DV_C), lambda b, h: (row_blk0 + b, 3 * H_C + h)),
                  pl.BlockSpec((t, 4 * H_C), lambda b, h: (row_blk0 + b, 0)),
                  pl.BlockSpec((1, 2, 1, nc, GC), lambda b, h: (b, 0, h, 0, 0)),
                  pl.BlockSpec((1, 2, 1, DK_C, DV_C), lambda b, h: (b, 0, h, 0, 0)),
                  pl.BlockSpec((1, DV_C), lambda b, h: (0, 0))],
        out_specs=[pl.BlockSpec((t, DV_C), lambda b, h: (b, h)),
                   pl.BlockSpec((1, 2, 1, DK_C, DV_C), lambda b, h: (b, 0, h, 0, 0))],
        out_shape=[jax.ShapeDtypeStruct((batch * t, C_V), F32),
                   jax.ShapeDtypeStruct((batch, 2, H_C, DK_C, DV_C), F32)],
        scratch_shapes=[pltpu.VMEM((t, DV_C), F32), pltpu.VMEM((2, DK_C, DV_C), F32),
                        pltpu.VMEM((2, 2 * t, DK_C), BF16), pltpu.VMEM((2, t, DV_C), F32),
                        pltpu.VMEM((2, t, GC), BF16), pltpu.VMEM((2, t, DK_C), BF16)],
        compiler_params=_cparams("parallel", "parallel"),
    )(qkv, qkv, qkv, pm, gbeta, grow, s0, norm_g.reshape(1, DV_C))


FFN_TM = 512


def _ffn_pre_kernel(x_ref, shift_ref, scale_ref, rw_ref, h_ref, logit_ref):
    h = x_ref[...] * (1.0 + scale_ref[0]) + shift_ref[0]
    h_ref[...] = _pack_bf16_pairs(h)
    logit_ref[...] = _dot_hp(h, rw_ref[...])


def _ffn_pre(x, mod3, router_w_pad):
    n = x.shape[0]
    tm = FFN_TM
    row = lambda i: _mod_row_of_tile(i, tm)
    return pl.pallas_call(
        _ffn_pre_kernel,
        grid=(n // tm,),
        in_specs=[pl.BlockSpec((tm, D_MODEL), lambda i: (i, 0)),
                  pl.BlockSpec((1, 1, D_MODEL), lambda i: (row(i), 0, 3)),
                  pl.BlockSpec((1, 1, D_MODEL), lambda i: (row(i), 0, 4)),
                  pl.BlockSpec((D_MODEL, LANES), lambda i: (0, 0))],
        out_specs=[pl.BlockSpec((tm, D_PACK), lambda i: (i, 0)),
                   pl.BlockSpec((tm, LANES), lambda i: (i, 0))],
        out_shape=[jax.ShapeDtypeStruct((n, D_PACK), jnp.int32),
                   jax.ShapeDtypeStruct((n, LANES), F32)],
        compiler_params=_cparams("parallel"),
    )(x, mod3, mod3, router_w_pad)


def _shared_kernel(h_ref, w13_ref, w2_ref, sh_ref):
    hb = _unpack_bf16_pairs(h_ref[...]).astype(BF16)
    up = jnp.dot(hb, w13_ref[...], preferred_element_type=F32)
    hid = _silu(up[:, :D_SHARED]) * up[:, D_SHARED:]
    sh_ref[...] = jnp.dot(hid.astype(BF16), w2_ref[...], preferred_element_type=F32).astype(sh_ref.dtype)


def _shared_expert(hp, ws13, ws2):
    n = hp.shape[0]
    tm = FFN_TM
    return pl.pallas_call(
        _shared_kernel,
        grid=(n // tm,),
        in_specs=[pl.BlockSpec((tm, D_PACK), lambda i: (i, 0)),
                  pl.BlockSpec((D_MODEL, 2 * D_SHARED), lambda i: (0, 0)),
                  pl.BlockSpec((D_SHARED, D_MODEL), lambda i: (0, 0))],
        out_specs=pl.BlockSpec((tm, D_MODEL), lambda i: (i, 0)),
        out_shape=jax.ShapeDtypeStruct((n, D_MODEL), BF16),
        compiler_params=_cparams("parallel"),
    )(hp, ws13, ws2)


ROUTE_T = 512
PER_GROUP = N_EXPERTS // N_GROUPS
NEG_INF = float("-inf")


def _first_max(x, iota, n):
    m = jnp.max(x, axis=0, keepdims=True)
    first = jnp.min(jnp.where(x == m, iota, n), axis=0, keepdims=True)
    return m, iota == first


def _route_kernel(lg_ref, bias_ref, tri_ref, idx_ref, w_ref, rank_ref, cnt_ref, carry_ref):
    @pl.when(pl.program_id(0) == 0)
    def _():
        carry_ref[...] = jnp.zeros_like(carry_ref)

    t = lg_ref.shape[0]
    logits = jnp.transpose(lg_ref[...])[:N_EXPERTS]
    scores = 1.0 / (1.0 + jnp.exp(-logits))
    sel = scores + bias_ref[...]
    sub_g = lax.broadcasted_iota(jnp.int32, (PER_GROUP, t), 0)
    sub_e = lax.broadcasted_iota(jnp.int32, (N_EXPERTS, t), 0)
    grp_rows = []
    for g in range(N_GROUPS):
        x = sel[g * PER_GROUP:(g + 1) * PER_GROUP]
        m1, hit = _first_max(x, sub_g, PER_GROUP)
        m2 = jnp.max(jnp.where(hit, NEG_INF, x), axis=0, keepdims=True)
        grp_rows.append(m1 + m2)
    cur = jnp.concatenate(grp_rows, axis=0)
    sub_grp = lax.broadcasted_iota(jnp.int32, (N_GROUPS, t), 0)
    grp_on = jnp.zeros((N_GROUPS, t), F32)
    for _ in range(TOPK_GROUPS):
        _, hit = _first_max(cur, sub_grp, N_GROUPS)
        grp_on = jnp.where(hit, 1.0, grp_on)
        cur = jnp.where(hit, NEG_INF, cur)
    exp_on = jnp.concatenate([jnp.broadcast_to(grp_on[g:g + 1], (PER_GROUP, t)) for g in range(N_GROUPS)], axis=0)
    cur = jnp.where(exp_on > 0.0, sel, NEG_INF)
    hits, idx_rows = [], []
    for _ in range(TOP_K):
        _, hit = _first_max(cur, sub_e, N_EXPERTS)
        hits.append(hit)
        idx_rows.append(jnp.sum(jnp.where(hit, sub_e, 0), axis=0, keepdims=True))
        cur = jnp.where(hit, NEG_INF, cur)
    w_rows = [jnp.sum(jnp.where(hit, scores, 0.0), axis=0, keepdims=True) for hit in hits]
    total = functools.reduce(lambda a, b: a + b, w_rows)
    w_ref[...] = jnp.concatenate([w / total * ROUTED_SCALE for w in w_rows], axis=0)
    idx_ref[...] = jnp.concatenate(idx_rows, axis=0)
    chosen = functools.reduce(lambda a, b: a + b, [jnp.where(hit, 1.0, 0.0) for hit in hits])
    incl = jnp.dot(chosen.astype(BF16), tri_ref[...], preferred_element_type=F32)
    before = incl - chosen + carry_ref[:, :1]
    rank_ref[...] = jnp.concatenate(
        [jnp.sum(jnp.where(hit, before, 0.0), axis=0, keepdims=True) for hit in hits], axis=0).astype(jnp.int32)
    carry_ref[...] = carry_ref[...] + incl[:, t - 1:t]
    cnt_ref[...] = carry_ref[...]


def _route(logits, router_bias):
    n = logits.shape[0]
    t = ROUTE_T
    tri = (jnp.arange(t)[:, None] <= jnp.arange(t)[None, :]).astype(BF16)
    bias_b = jnp.broadcast_to(router_bias.astype(F32)[:, None], (N_EXPERTS, t))
    slot = pl.BlockSpec((TOP_K, t), lambda i: (0, i))
    idx, w, rank, cnt = pl.pallas_call(
        _route_kernel,
        grid=(n // t,),
        in_specs=[pl.BlockSpec((t, LANES), lambda i: (i, 0)),
                  pl.BlockSpec((N_EXPERTS, t), lambda i: (0, 0)),
                  pl.BlockSpec((t, t), lambda i: (0, 0))],
        out_specs=[slot, slot, slot, pl.BlockSpec((N_EXPERTS, LANES), lambda i: (0, 0))],
        out_shape=[jax.ShapeDtypeStruct((TOP_K, n), jnp.int32), jax.ShapeDtypeStruct((TOP_K, n), F32),
                   jax.ShapeDtypeStruct((TOP_K, n), jnp.int32), jax.ShapeDtypeStruct((N_EXPERTS, LANES), F32)],
        scratch_shapes=[pltpu.VMEM((N_EXPERTS, LANES), F32)],
        compiler_params=_cparams("arbitrary"),
    )(logits, bias_b, tri)
    return idx, w, rank, cnt[:, 0].astype(jnp.int32)


def _experts_kernel(be_ref, bv_ref, x_ref, w1_ref, w3_ref, w2_ref, o_ref, w1b, w3b, w2b):
    i = pl.program_id(0)
    changed = jnp.logical_or(i == 0, be_ref[i] != be_ref[jnp.maximum(i - 1, 0)])

    @pl.when(changed)
    def _():
        w1b[...] = w1_ref[0, 0].astype(BF16)
        w3b[...] = w3_ref[0, 0].astype(BF16)
        w2b[...] = w2_ref[0, 0].astype(BF16)

    live = lax.broadcasted_iota(jnp.int32, (MOE_BLOCK, 1), 0) < bv_ref[i]
    xb = _unpack_bf16_pairs(jnp.where(live, x_ref[...], 0)).astype(BF16)
    hid = _silu(jnp.dot(xb, w1b[...], preferred_element_type=F32)) * \
        jnp.dot(xb, w3b[...], preferred_element_type=F32)
    o_ref[...] = _pack_bf16_pairs(jnp.dot(hid.astype(BF16), w2b[...], preferred_element_type=F32))


def _grouped_experts(xs, block_expert, block_valid, w1, w3, w2, layer):
    rows = xs.shape[0]
    n_blocks = rows // MOE_BLOCK
    grid_spec = pltpu.PrefetchScalarGridSpec(
        num_scalar_prefetch=2,
        grid=(n_blocks,),
        in_specs=[pl.BlockSpec((MOE_BLOCK, D_PACK), lambda i, be, bv: (i, 0)),
                  pl.BlockSpec((1, 1, D_MODEL, D_EXPERT), lambda i, be, bv: (layer, be[i], 0, 0)),
                  pl.BlockSpec((1, 1, D_MODEL, D_EXPERT), lambda i, be, bv: (layer, be[i], 0, 0)),
                  pl.BlockSpec((1, 1, D_EXPERT, D_MODEL), lambda i, be, bv: (layer, be[i], 0, 0))],
        out_specs=pl.BlockSpec((MOE_BLOCK, D_PACK), lambda i, be, bv: (i, 0)),
        scratch_shapes=[pltpu.VMEM((D_MODEL, D_EXPERT), BF16), pltpu.VMEM((D_MODEL, D_EXPERT), BF16),
                        pltpu.VMEM((D_EXPERT, D_MODEL), BF16)])
    return pl.pallas_call(
        _experts_kernel,
        grid_spec=grid_spec,
        out_shape=jax.ShapeDtypeStruct((rows, D_PACK), jnp.int32),
        compiler_params=_cparams("arbitrary"),
    )(block_expert, block_valid, xs, w1, w3, w2)


SC_CORES = 2
SC_SUBCORES = 16
SC_WORKERS = SC_CORES * SC_SUBCORES
SC_ROWS = 64


def _sc_mesh():
    return plsc.VectorSubcoreMesh(core_axis_name="c", subcore_axis_name="s",
                                  num_cores=SC_CORES, num_subcores=SC_SUBCORES)


def _sc_dispatch(table, dest, rows):
    n, d = table.shape
    kk = dest.shape[0]
    per_worker = n // SC_WORKERS
    n_chunks = per_worker // SC_ROWS
    assert per_worker * SC_WORKERS == n and n_chunks * SC_ROWS == per_worker
    idx = dest.reshape(kk, SC_WORKERS, n_chunks, SC_ROWS).transpose(1, 2, 0, 3).reshape(
        SC_WORKERS, n_chunks * kk, SC_ROWS)

    @functools.partial(
        pl.kernel, mesh=_sc_mesh(),
        out_type=jax.ShapeDtypeStruct((rows, d), table.dtype),
        scratch_types=[pltpu.VMEM((n_chunks * kk, SC_ROWS), jnp.int32),
                       pltpu.VMEM((SC_ROWS, d), table.dtype),
                       pltpu.SemaphoreType.DMA])
    def dispatch(table_hbm, idx_hbm, out_hbm, idx_v, rows_v, sem):
        wid = lax.axis_index("s") * SC_CORES + lax.axis_index("c")
        base = wid * per_worker
        pltpu.sync_copy(idx_hbm.at[wid], idx_v)

        @pl.loop(0, n_chunks)
        def _(j):
            pltpu.sync_copy(table_hbm.at[pl.ds(base + j * SC_ROWS, SC_ROWS)], rows_v)
            copies = [pltpu.async_copy(rows_v, out_hbm.at[idx_v.at[j * kk + k]], sem) for k in range(kk)]
            for cp in copies:
                cp.wait()

    return dispatch(table, idx)


def _sc_gather(table, idx):
    b = idx.shape[0]
    d = table.shape[1]
    per_worker = b // SC_WORKERS
    n_chunks = per_worker // SC_ROWS
    assert per_worker * SC_WORKERS == b and n_chunks * SC_ROWS == per_worker

    @functools.partial(
        pl.kernel, mesh=_sc_mesh(),
        out_type=jax.ShapeDtypeStruct((b, d), table.dtype),
        scratch_types=[pltpu.VMEM((n_chunks, SC_ROWS), jnp.int32),
                       pltpu.VMEM((SC_ROWS, d), table.dtype),
                       pltpu.SemaphoreType.DMA])
    def gather(table_hbm, idx_hbm, out_hbm, idx_v, rows_v, sem):
        wid = lax.axis_index("s") * SC_CORES + lax.axis_index("c")
        base = wid * per_worker
        pltpu.sync_copy(idx_hbm.at[wid], idx_v)

        @pl.loop(0, n_chunks)
        def _(j):
            pltpu.async_copy(table_hbm.at[idx_v.at[j]], rows_v, sem).wait()
            pltpu.sync_copy(rows_v, out_hbm.at[pl.ds(base + j * SC_ROWS, SC_ROWS)])

    return gather(table, idx.reshape(SC_WORKERS, n_chunks, SC_ROWS))


POST_TM = 256


def _ffn_post_kernel(x_ref, yg_ref, w_ref, s_ref, gate_ref, g_ref, b_ref, o_ref):
    w = w_ref[...]
    routed = _unpack_bf16_pairs(yg_ref[0]) * w[:, 0:1]
    for k in range(1, TOP_K):
        routed = routed + _unpack_bf16_pairs(yg_ref[k]) * w[:, k:k + 1]
    r = DEEPNORM_ALPHA * x_ref[...] + gate_ref[0] * (routed + s_ref[...])
    o_ref[...] = _layer_norm_rows(r, g_ref[...], b_ref[...])


def _ffn_post(x, yg, w_tok, shared, mod3, ln_g, ln_b):
    n = x.shape[0]
    tm = POST_TM
    row = lambda i: _mod_row_of_tile(i, tm)
    tile = pl.BlockSpec((tm, D_MODEL), lambda i: (i, 0))
    vec = pl.BlockSpec((1, D_MODEL), lambda i: (0, 0))
    return pl.pallas_call(
        _ffn_post_kernel,
        grid=(n // tm,),
        in_specs=[tile, pl.BlockSpec((TOP_K, tm, D_PACK), lambda i: (0, i, 0)),
                  pl.BlockSpec((tm, TOP_K), lambda i: (i, 0)), tile,
                  pl.BlockSpec((1, 1, D_MODEL), lambda i: (row(i), 0, 5)), vec, vec],
        out_specs=tile,
        out_shape=jax.ShapeDtypeStruct((n, D_MODEL), F32),
        compiler_params=_cparams("parallel"),
    )(x, yg, w_tok, shared, mod3, ln_g.reshape(1, D_MODEL), ln_b.reshape(1, D_MODEL))


def _moe_routed(h, idx, rank, counts, w1, w3, w2, layer):
    n = h.shape[0]
    padded = (counts + MOE_BLOCK - 1) // MOE_BLOCK * MOE_BLOCK
    pad_end = jnp.cumsum(padded)
    pad_start = pad_end - padded
    experts = jnp.arange(N_EXPERTS, dtype=jnp.int32)
    dest = jnp.sum(jnp.where(idx[:, :, None] == experts, pad_start, 0), axis=-1) + rank
    n_blocks = n * TOP_K // MOE_BLOCK + N_EXPERTS
    rows = n_blocks * MOE_BLOCK
    block_start = jnp.arange(n_blocks, dtype=jnp.int32) * MOE_BLOCK
    block_expert = jnp.minimum(jnp.sum(pad_end[None, :] <= block_start[:, None], axis=1), N_EXPERTS - 1).astype(jnp.int32)
    of_block = block_expert[:, None] == experts
    used = jnp.sum(jnp.where(of_block, counts, 0), axis=1) - (block_start - jnp.sum(jnp.where(of_block, pad_start, 0), axis=1))
    block_valid = jnp.clip(used, 0, MOE_BLOCK).astype(jnp.int32)
    xs = _sc_dispatch(h, dest, rows)
    ys = _grouped_experts(xs, block_expert, block_valid, w1, w3, w2, layer)
    return _sc_gather(ys, dest.reshape(-1)).reshape(TOP_K, n, D_PACK)


def _even_layer(x, mod3, j, w_in, w_out, q_norm, k_norm, log_decay, gn_g,
                cache_k, cache_v, state_ret):
    (p,) = _modulated_proj(x, mod3, 0, w_in.astype(BF16), [EVEN_IN], [F32])
    q, k = _attn_prep(p, q_norm, k_norm)
    v = p[:, A_Q + A_KV:A_Q + 2 * A_KV]
    k_p = k[:N_P].reshape(BATCH, SEQ, A_KV)
    v_p = v[:N_P].reshape(BATCH, SEQ, A_KV)
    o_p = _attention(q, k_p, v_p, 0, BATCH, SEQ)
    k_all = jnp.concatenate([k[N_P:].reshape(DEC_BATCH, DEC_SEQ, A_KV),
                             cache_k.reshape(DEC_BATCH, PAST_LEN, A_KV)], axis=1)
    v_all = jnp.concatenate([v[N_P:].reshape(DEC_BATCH, DEC_SEQ, A_KV),
                             cache_v.reshape(DEC_BATCH, PAST_LEN, A_KV)], axis=1)
    o_s = _attention(q, k_all, v_all, N_P // DEC_SEQ, DEC_BATCH, DEC_SEQ)
    o_attn = jnp.concatenate([o_p, o_s], axis=0)
    lg_rows = jnp.broadcast_to(log_decay.reshape(2 * H_B, 1), (2 * H_B, LANES))
    zeros_s = jnp.zeros((BATCH, 2, H_B, DK_B, DV_B), F32)
    r_p, s_p = _retention(p, 0, BATCH, SEQ, lg_rows, zeros_s, gn_g)
    r_s, _ = _retention(p, N_P // DEC_SEQ, DEC_BATCH, DEC_SEQ, lg_rows, state_ret, gn_g)
    o_ret = jnp.concatenate([r_p, r_s], axis=0)
    return ([o_attn, o_ret], w_out.astype(BF16), k_p.reshape(BATCH, SEQ, N_KV_A, HD_A),
            v_p.reshape(BATCH, SEQ, N_KV_A, HD_A), s_p)


def _chunk_cumsum(g, batch, t):
    gc = g.reshape(batch, t // GC, GC, 2, H_C)
    f = jnp.cumsum(gc[:, :, :, 0], axis=2)
    b = jnp.cumsum(gc[:, :, ::-1, 1], axis=2)[:, :, ::-1]
    return jnp.stack([f, b], axis=3)


def _odd_layer(x, mod3, j, w_in, conv_w, a_log, dt_bias, norm_g, w_out, state_gdn):
    w_main = w_in[:, :2 * C_QK + 2 * C_V].astype(BF16)
    w_ab = jnp.pad(w_in[:, 2 * C_QK + 2 * C_V:], ((0, 0), (0, LANES - 4 * H_C))).astype(BF16)
    w_cat = jnp.concatenate([w_main, w_ab], axis=1)
    pm, pab = _modulated_proj(x, mod3, 0, w_cat, [2 * C_QK + 2 * C_V, LANES], [F32, F32])
    qkv_p = _gdn_prep(pm, conv_w, 0, BATCH, SEQ)
    qkv_s = _gdn_prep(pm, conv_w, N_P // DEC_SEQ, DEC_BATCH, DEC_SEQ)
    ab = pab[:, :4 * H_C].reshape(N_TOK, 2, 2, H_C)
    beta = jax.nn.sigmoid(ab[:, 0])
    g = -jnp.exp(a_log.astype(F32)) * jax.nn.softplus(ab[:, 1] + dt_bias.astype(F32))

    gcs_p = _chunk_cumsum(g[:N_P], BATCH, SEQ)
    gcs_s = _chunk_cumsum(g[N_P:], DEC_BATCH, DEC_SEQ)
    gcol = jnp.concatenate([gcs_p.reshape(N_P, 2 * H_C), gcs_s.reshape(N_S, 2 * H_C)], axis=0)
    gbeta = jnp.concatenate([beta.reshape(N_TOK, 2 * H_C), gcol], axis=1)
    zeros_s = jnp.zeros((BATCH, 2, H_C, DK_C, DV_C), F32)
    o_p, s_p = _gdn(qkv_p, pm, gbeta, gcs_p.transpose(0, 3, 4, 1, 2), 0, BATCH, SEQ, zeros_s, norm_g)
    o_s, _ = _gdn(qkv_s, pm, gbeta, gcs_s.transpose(0, 3, 4, 1, 2), N_P // DEC_SEQ, DEC_BATCH, DEC_SEQ,
                  state_gdn, norm_g)
    return [jnp.concatenate([o_p, o_s], axis=0)], w_out.astype(BF16), s_p


def kernel(x_prompt, x_sample, cache_attn_k, cache_attn_v, state_ret, state_gdn, c, c_ctx, mod_w, mod_b, ln_g, ln_b, even_w_in, even_w_out, attn_q_norm, attn_k_norm, ret_log_decay, ret_norm_g, odd_w_in, gdn_conv_w, gdn_a_log, gdn_dt_bias, gdn_norm_g, odd_w_out, router_w, router_bias, expert_w1, expert_w3, expert_w2, shared_w1, shared_w3, shared_w2):
    x = jnp.concatenate([x_prompt.reshape(N_P, D_MODEL), x_sample.reshape(N_S, D_MODEL)], axis=0)
    cvec = jnp.concatenate([c_ctx[None, :], c, jnp.zeros((MOD_ROWS - N_MOD, D_MODEL), F32)], axis=0)
    mod_all = _mod_vectors(cvec, mod_w, mod_b)
    new_k, new_v, new_ret, new_gdn = [], [], [], []
    for l in range(DEPTH):
        j = l // 2
        mod3 = mod_all[l].reshape(MOD_ROWS, 1, 6 * D_MODEL)
        if l % 2 == 0:
            a_list, w_o, k_p, v_p, s_p = _even_layer(
                x, mod3, j, even_w_in[j], even_w_out[j], attn_q_norm[j], attn_k_norm[j],
                ret_log_decay[j], ret_norm_g[j], cache_attn_k[:, j], cache_attn_v[:, j], state_ret[:, j])
            new_k.append(k_p)
            new_v.append(v_p)
            new_ret.append(s_p)
        else:
            a_list, w_o, s_p = _odd_layer(
                x, mod3, j, odd_w_in[j], gdn_conv_w[j], gdn_a_log[j], gdn_dt_bias[j], gdn_norm_g[j],
                odd_w_out[j], state_gdn[:, j])
            new_gdn.append(s_p)
        x = _outproj_ln(a_list, w_o, x, mod3, 2, ln_g[l, 0], ln_b[l, 0])
        rw = jnp.pad(router_w[l], ((0, 0), (0, LANES - N_EXPERTS)))
        ws13 = jnp.concatenate([shared_w1[l], shared_w3[l]], axis=1).astype(BF16)
        h, logits = _ffn_pre(x, mod3, rw)
        idx, w, rank, counts = _route(logits, router_bias[l])
        yg = _moe_routed(h, idx, rank, counts, expert_w1, expert_w3, expert_w2, l)
        shared = _shared_expert(h, ws13, shared_w2[l].astype(BF16))
        x = _ffn_post(x, yg, w.T, shared, mod3, ln_g[l, 1], ln_b[l, 1])
    return (x[:N_P].reshape(BATCH, SEQ, D_MODEL), x[N_P:].reshape(DEC_BATCH, DEC_SEQ, D_MODEL),
            jnp.stack(new_k, axis=1), jnp.stack(new_v, axis=1),
            jnp.stack(new_ret, axis=1), jnp.stack(new_gdn, axis=1))
```

```python
import functools
import math

import jax
import jax.numpy as jnp
from jax import lax
from jax.experimental import pallas as pl
from jax.experimental.pallas import tpu as pltpu
from jax.experimental.pallas import tpu_sc as plsc

D_MODEL = 1024
BATCH = 16
SEQ = 256
DEPTH = 4
DEC_BATCH = 4
DEC_SEQ = 4096
PAST_LEN = 512
GRID_W = 64
N_HEADS_A = 8
N_KV_A = 2
HD_A = 64
ROPE_THETA = 10000.0
H_B = 4
DK_B = 64
DV_B = 128
RET_CHUNK = 128
H_C = 8
DK_C = 128
DV_C = 128
CONV_K = 5
GDN_CHUNK = 64
N_EXPERTS = 64
TOP_K = 8
N_GROUPS = 8
TOPK_GROUPS = 4
D_EXPERT = 256
D_SHARED = 256
ROUTED_SCALE = 2.5
MOE_BLOCK = 512
A_Q = N_HEADS_A * HD_A
A_KV = N_KV_A * HD_A
B_QK = H_B * DK_B
B_V = H_B * DV_B
EVEN_IN = A_Q + 2 * A_KV + 2 * B_QK + 2 * B_V
C_QK = H_C * DK_C
C_V = H_C * DV_C
DEEPNORM_ALPHA = (2 * DEPTH) ** 0.25
EPS = 1e-6

N_P = BATCH * SEQ
N_S = DEC_BATCH * DEC_SEQ
N_TOK = N_P + N_S
N_MOD = 1 + DEC_BATCH
MOD_ROWS = 8

LANES = 128
VMEM_LIMIT = 56 * 1024 * 1024

F32 = jnp.float32
BF16 = jnp.bfloat16


def _cparams(*sem):
    return pltpu.CompilerParams(dimension_semantics=sem, vmem_limit_bytes=VMEM_LIMIT)


def _bdot(a, b):
    return jnp.dot(a.astype(BF16), b.astype(BF16), preferred_element_type=F32)


def _bdot_t(a, b):
    return lax.dot_general(a.astype(BF16), b.astype(BF16), (((0,), (0,)), ((), ())),
                           preferred_element_type=F32)


def _bdot_nt(a, b):
    return lax.dot_general(a.astype(BF16), b.astype(BF16), (((1,), (1,)), ((), ())),
                           preferred_element_type=F32)


def _split3(a):
    hi = a.astype(BF16)
    r = a - hi.astype(F32)
    mid = r.astype(BF16)
    lo = (r - mid.astype(F32)).astype(BF16)
    return hi, mid, lo


def _dot_hp(a, b):
    a0, a1, a2 = _split3(a)
    b0, b1, b2 = _split3(b)
    d = lambda x, y: jnp.dot(x, y, preferred_element_type=F32)
    small = d(a0, b2) + d(a2, b0) + d(a1, b1)
    return (d(a0, b1) + d(a1, b0)) + small + d(a0, b0)


def _silu(x):
    return x * (1.0 / (1.0 + jnp.exp(-x)))


HI16 = 0xFFFF0000
D_PACK = D_MODEL // 2


def _pack_bf16_pairs(x):
    c = x.shape[1] // 2
    lo = pltpu.bitcast(x[:, :c].astype(BF16).astype(F32), jnp.uint32) >> 16
    hi = pltpu.bitcast(x[:, c:].astype(BF16).astype(F32), jnp.uint32) & jnp.uint32(HI16)
    return pltpu.bitcast(lo | hi, jnp.int32)


def _unpack_bf16_pairs(p):
    u = pltpu.bitcast(p, jnp.uint32)
    return jnp.concatenate([pltpu.bitcast(u << 16, F32), pltpu.bitcast(u & jnp.uint32(HI16), F32)], axis=1)


def _mod_row_of_tile(i, tile):
    tiles_p = N_P // tile
    tiles_per_b = DEC_SEQ // tile
    return jnp.where(i < tiles_p, 0, 1 + (i - tiles_p) // tiles_per_b)


MOD_TN = 1536


def _mod_kernel(c_ref, w_ref, b_ref, o_ref):
    a = _silu(c_ref[...])
    o_ref[0] = _bdot(a, w_ref[0]) + b_ref[0]


def _mod_vectors(cvec, mod_w, mod_b):
    n6 = 6 * D_MODEL
    return pl.pallas_call(
        _mod_kernel,
        grid=(DEPTH, n6 // MOD_TN),
        in_specs=[pl.BlockSpec((MOD_ROWS, D_MODEL), lambda l, j: (0, 0)),
                  pl.BlockSpec((1, D_MODEL, MOD_TN), lambda l, j: (l, 0, j)),
                  pl.BlockSpec((1, 1, MOD_TN), lambda l, j: (l, 0, j))],
        out_specs=pl.BlockSpec((1, MOD_ROWS, MOD_TN), lambda l, j: (l, 0, j)),
        out_shape=jax.ShapeDtypeStruct((DEPTH, MOD_ROWS, n6), F32),
        compiler_params=_cparams("parallel", "parallel"),
    )(cvec, mod_w, mod_b.reshape(DEPTH, 1, n6))


PROJ_TM = 512


def _proj_kernel(x_ref, shift_ref, scale_ref, w_ref, *o_refs, widths):
    h = (x_ref[...] * (1.0 + scale_ref[0]) + shift_ref[0]).astype(BF16)
    off = 0
    for o_ref, wd in zip(o_refs, widths):
        o_ref[...] = jnp.dot(h, w_ref[:, off:off + wd], preferred_element_type=F32).astype(o_ref.dtype)
        off += wd


def _modulated_proj(x, mod3, shift_blk, w_bf16, widths, dtypes):
    n = x.shape[0]
    tm = PROJ_TM
    row = lambda i: _mod_row_of_tile(i, tm)
    return pl.pallas_call(
        functools.partial(_proj_kernel, widths=tuple(widths)),
        grid=(n // tm,),
        in_specs=[pl.BlockSpec((tm, D_MODEL), lambda i: (i, 0)),
                  pl.BlockSpec((1, 1, D_MODEL), lambda i: (row(i), 0, shift_blk)),
                  pl.BlockSpec((1, 1, D_MODEL), lambda i: (row(i), 0, shift_blk + 1)),
                  pl.BlockSpec((D_MODEL, sum(widths)), lambda i: (0, 0))],
        out_specs=[pl.BlockSpec((tm, wd), lambda i: (i, 0)) for wd in widths],
        out_shape=[jax.ShapeDtypeStruct((n, wd), dt) for wd, dt in zip(widths, dtypes)],
        compiler_params=_cparams("parallel"),
    )(x, mod3, mod3, w_bf16)


OUT_TM = 512


def _layer_norm_rows(r, g, b):
    mu = jnp.mean(r, axis=-1, keepdims=True)
    d = r - mu
    var = jnp.mean(d * d, axis=-1, keepdims=True)
    return d * lax.rsqrt(var + EPS) * g + b


def _outproj_kernel(*refs, n_a):
    a_refs = refs[:n_a]
    w_ref, x_ref, gate_ref, g_ref, b_ref, o_ref = refs[n_a:]
    off = 0
    acc = None
    for a_ref in a_refs:
        wd = a_ref.shape[1]
        part = jnp.dot(a_ref[...].astype(BF16), w_ref[off:off + wd, :], preferred_element_type=F32)
        acc = part if acc is None else acc + part
        off += wd
    r = DEEPNORM_ALPHA * x_ref[...] + gate_ref[0] * acc
    o_ref[...] = _layer_norm_rows(r, g_ref[...], b_ref[...])


def _outproj_ln(a_list, w_bf16, x, mod3, gate_blk, ln_g, ln_b):
    n = x.shape[0]
    tm = OUT_TM
    row = lambda i: _mod_row_of_tile(i, tm)
    kdim = w_bf16.shape[0]
    return pl.pallas_call(
        functools.partial(_outproj_kernel, n_a=len(a_list)),
        grid=(n // tm,),
        in_specs=[pl.BlockSpec((tm, a.shape[1]), lambda i: (i, 0)) for a in a_list] + [
            pl.BlockSpec((kdim, D_MODEL), lambda i: (0, 0)),
            pl.BlockSpec((tm, D_MODEL), lambda i: (i, 0)),
            pl.BlockSpec((1, 1, D_MODEL), lambda i: (row(i), 0, gate_blk)),
            pl.BlockSpec((1, D_MODEL), lambda i: (0, 0)),
            pl.BlockSpec((1, D_MODEL), lambda i: (0, 0))],
        out_specs=pl.BlockSpec((tm, D_MODEL), lambda i: (i, 0)),
        out_shape=jax.ShapeDtypeStruct((n, D_MODEL), F32),
        compiler_params=_cparams("parallel"),
    )(*a_list, w_bf16, x, mod3, ln_g.reshape(1, D_MODEL), ln_b.reshape(1, D_MODEL))


PREP_TM = 512
ROPE_SEG = HD_A // 4


def _head_rms(x, gain, ones_bd):
    parts = _split3(x * x)
    ss = functools.reduce(lambda a, b: a + b,
                          [jnp.dot(p, ones_bd, preferred_element_type=F32) for p in reversed(parts)])
    return x * lax.rsqrt(ss * (1.0 / HD_A) + EPS) * gain


def _rope_lanes(x, cos, sin):
    lane = lax.broadcasted_iota(jnp.int32, (1, LANES), 1)
    first = (lane % (2 * ROPE_SEG)) < ROPE_SEG
    cols = []
    for b in range(x.shape[1] // LANES):
        xb = x[:, b * LANES:(b + 1) * LANES]
        partner = jnp.where(first, pltpu.roll(xb, LANES - ROPE_SEG, axis=1), pltpu.roll(xb, ROPE_SEG, axis=1))
        cols.append(xb * cos + partner * sin)
    return cols[0] if len(cols) == 1 else jnp.concatenate(cols, axis=1)


def _attn_prep_kernel(q_ref, kv_ref, cos_ref, sin_ref, qg_ref, kg_ref, bd_ref, qo_ref, ko_ref):
    cos = cos_ref[...]
    sin = sin_ref[...]
    q = _rope_lanes(_head_rms(q_ref[...], qg_ref[...], bd_ref[...]), cos, sin)
    qo_ref[...] = (q * (HD_A ** -0.5)).astype(BF16)
    k = _head_rms(kv_ref[:, :A_KV], kg_ref[...], bd_ref[:A_KV, :A_KV])
    ko_ref[...] = _rope_lanes(k, cos, sin)


def _rope_tables():
    rows = DEC_SEQ // GRID_W
    row = jnp.repeat(jnp.arange(rows), GRID_W).astype(F32)
    colp = jnp.tile(jnp.arange(GRID_W), rows).astype(F32)
    inv_freq = ROPE_THETA ** (-jnp.arange(ROPE_SEG, dtype=F32) / ROPE_SEG)
    ar = row[:, None] * inv_freq[None, :]
    ac = colp[:, None] * inv_freq[None, :]
    cos = jnp.concatenate([jnp.cos(ar), jnp.cos(ar), jnp.cos(ac), jnp.cos(ac)], axis=-1)
    sin = jnp.concatenate([-jnp.sin(ar), jnp.sin(ar), -jnp.sin(ac), jnp.sin(ac)], axis=-1)
    cos = jnp.concatenate([jnp.ones((PREP_TM, HD_A), F32), cos], axis=0)
    sin = jnp.concatenate([jnp.zeros((PREP_TM, HD_A), F32), sin], axis=0)
    return jnp.tile(cos, (1, LANES // HD_A)), jnp.tile(sin, (1, LANES // HD_A))


def _attn_prep(p, q_norm, k_norm):
    tm = PREP_TM
    tiles_p = N_P // tm
    tiles_seq = DEC_SEQ // tm
    tab = lambda i: (jnp.where(i < tiles_p, 0, 1 + (i - tiles_p) % tiles_seq), 0)
    cos, sin = _rope_tables()
    head_id = jnp.arange(A_Q) // HD_A
    ones_bd = (head_id[:, None] == head_id[None, :]).astype(BF16)
    return pl.pallas_call(
        _attn_prep_kernel,
        grid=(N_TOK // tm,),
        in_specs=[pl.BlockSpec((tm, A_Q), lambda i: (i, 0)),
                  pl.BlockSpec((tm, 2 * A_KV), lambda i: (i, A_Q // (2 * A_KV))),
                  pl.BlockSpec((tm, LANES), tab), pl.BlockSpec((tm, LANES), tab),
                  pl.BlockSpec((1, A_Q), lambda i: (0, 0)), pl.BlockSpec((1, A_KV), lambda i: (0, 0)),
                  pl.BlockSpec((A_Q, A_Q), lambda i: (0, 0))],
        out_specs=[pl.BlockSpec((tm, A_Q), lambda i: (i, 0)), pl.BlockSpec((tm, A_KV), lambda i: (i, 0))],
        out_shape=[jax.ShapeDtypeStruct((N_TOK, A_Q), BF16), jax.ShapeDtypeStruct((N_TOK, A_KV), F32)],
        compiler_params=_cparams("parallel"),
    )(p, p, cos, sin, jnp.tile(q_norm, N_HEADS_A).reshape(1, A_Q), jnp.tile(k_norm, N_KV_A).reshape(1, A_KV),
      ones_bd)


ATT_TQ = 256
GROUP_A = N_HEADS_A // N_KV_A


def _attn_kernel(q_ref, kt_ref, kts_ref, v_ref, vs_ref, o_ref):
    low = lax.broadcasted_iota(jnp.int32, (1, LANES), 1) < HD_A
    for j in range(N_HEADS_A // 2):
        qb = q_ref[:, j * LANES:(j + 1) * LANES]
        outs = []
        for half in range(2):
            kv_head = (2 * j + half) // GROUP_A
            qh = jnp.where(low if half == 0 else jnp.logical_not(low), qb, jnp.zeros_like(qb))
            kt, v = (kt_ref[0], v_ref[0]) if half == kv_head else (kts_ref[0], vs_ref[0])
            s = jnp.dot(qh, kt, preferred_element_type=F32)
            m = jnp.max(s, axis=-1, keepdims=True)
            p = jnp.exp(s - m)
            l = jnp.sum(p, axis=-1, keepdims=True)
            pv = jnp.dot(p.astype(BF16), v, preferred_element_type=F32)
            outs.append(pv * (1.0 / l))
        o_ref[:, j * LANES:(j + 1) * LANES] = jnp.where(low, outs[0], outs[1]).astype(o_ref.dtype)


def _attention(q, k, v, row_blk0, batch, t):
    tk = v.shape[1]
    tq = min(ATT_TQ, t)
    nq = t // tq
    swap = lambda x: jnp.roll(x, HD_A, axis=-1)
    kt = jnp.swapaxes(k, 1, 2).astype(BF16)
    kts = jnp.swapaxes(swap(k), 1, 2).astype(BF16)
    kspec = pl.BlockSpec((1, LANES, tk), lambda b, i: (b, 0, 0))
    vspec = pl.BlockSpec((1, tk, LANES), lambda b, i: (b, 0, 0))
    return pl.pallas_call(
        _attn_kernel,
        grid=(batch, nq),
        in_specs=[pl.BlockSpec((tq, A_Q), lambda b, i: ((row_blk0 + b) * nq + i, 0)), kspec, kspec, vspec, vspec],
        out_specs=pl.BlockSpec((tq, A_Q), lambda b, i: (b * nq + i, 0)),
        out_shape=jax.ShapeDtypeStruct((batch * t, A_Q), BF16),
        compiler_params=_cparams("parallel", "parallel"),
    )(q, kt, kts, v.astype(BF16), swap(v).astype(BF16))


RC = RET_CHUNK


def _ret_kernel(q_ref, k_ref, v_ref, g_ref, lg_ref, s0_ref, gn_ref, o_ref, s_out_ref,
                ob_ref, st_ref, *, t):
    nc = t // RC
    hp = pl.program_id(1)
    ii = lax.broadcasted_iota(jnp.int32, (RC, RC), 0).astype(F32)
    jj = lax.broadcasted_iota(jnp.int32, (RC, RC), 1).astype(F32)
    col_i = lax.broadcasted_iota(jnp.int32, (RC, 1), 0).astype(F32)
    lane = lax.broadcasted_iota(jnp.int32, (1, LANES), 1)
    masks = [(lane >= hh * DK_B) & (lane < (hh + 1) * DK_B) for hh in range(2)]

    consts = []
    for d in range(2):
        for hh in range(2):
            lg = lg_ref[pl.ds(d * H_B + hp * 2 + hh, 1), :][:, :1]
            if d == 0:
                diff = ii - jj
                qe, ke = col_i + 1.0, (RC - 1.0) - col_i
            else:
                diff = jj - ii
                qe, ke = RC - col_i, col_i
            intra = jnp.where(diff >= 0, jnp.exp(jnp.maximum(diff, 0.0) * lg), 0.0)
            consts.append((intra, jnp.exp(qe * lg), jnp.exp(ke * lg), jnp.exp(RC * lg)))
            s0 = s0_ref[0, d, hh]
            z = jnp.zeros((DK_B, DV_B), F32)
            st_ref[d * 2 + hh] = jnp.concatenate([s0, z] if hh == 0 else [z, s0], axis=0)

    def body(c, carry):
        combos = [(d, hh) for d in range(2) for hh in range(2)]
        rows = [pl.ds(pl.multiple_of(cc * RC, RC), RC) for cc in (c, nc - 1 - c)]
        qcs = [q_ref[rows[d], :] for d in range(2)]
        kcs = [k_ref[rows[d], :] * (DK_B ** -0.5) for d in range(2)]
        qhs = [jnp.where(masks[hh], qcs[d], 0.0) for d, hh in combos]
        khs = [jnp.where(masks[hh], kcs[d], 0.0) for d, hh in combos]
        vcs = [v_ref[rows[d], hh * DV_B:(hh + 1) * DV_B].astype(BF16) for d, hh in combos]
        sts = [st_ref[d * 2 + hh] for d, hh in combos]
        scores = [_bdot_nt(qhs[i], khs[i]) * consts[i][0] for i in range(4)]
        kvs = [_bdot_t(khs[i] * consts[i][2], vcs[i]) for i in range(4)]
        outs = [jnp.dot(jnp.concatenate([scores[i], qhs[i] * consts[i][1]], axis=1).astype(BF16),
                        jnp.concatenate([vcs[i], sts[i].astype(BF16)], axis=0),
                        preferred_element_type=F32) for i in range(4)]
        for i, (d, hh) in enumerate(combos):
            st_ref[d * 2 + hh] = sts[i] * consts[i][3] + kvs[i]
        o_ref[rows[0], :] = jnp.concatenate(outs[0:2], axis=1)
        ob_ref[rows[1], :] = jnp.concatenate(outs[2:4], axis=1)
        return carry

    lax.fori_loop(0, nc, body, 0)

    for d in range(2):
        for hh in range(2):
            s_out_ref[0, d, hh] = st_ref[d * 2 + hh][hh * DK_B:(hh + 1) * DK_B, :]

    def finish(c, carry):
        r0 = pl.multiple_of(c * RC, RC)
        o = o_ref[pl.ds(r0, RC), :] + ob_ref[pl.ds(r0, RC), :]
        gate = _silu(g_ref[pl.ds(r0, RC), :])
        ys = []
        for hh in range(2):
            oh = o[:, hh * DV_B:(hh + 1) * DV_B]
            mu = jnp.mean(oh, axis=-1, keepdims=True)
            dlt = oh - mu
            var = jnp.mean(dlt * dlt, axis=-1, keepdims=True)
            ys.append(dlt * lax.rsqrt(var + EPS))
        o_ref[pl.ds(r0, RC), :] = jnp.concatenate(ys, axis=1) * gn_ref[...] * gate
        return carry

    lax.fori_loop(0, nc, finish, 0)


def _retention(p, row_blk0, batch, t, lg_rows, s0, gn_g):
    qb, kb, vb, gb = (A_Q + 2 * A_KV) // LANES, (A_Q + 2 * A_KV + B_QK) // LANES, \
        (A_Q + 2 * A_KV + 2 * B_QK) // (2 * DV_B), (A_Q + 2 * A_KV + 2 * B_QK + B_V) // (2 * DV_B)
    return pl.pallas_call(
        functools.partial(_ret_kernel, t=t),
        grid=(batch, H_B // 2),
        in_specs=[pl.BlockSpec((t, LANES), lambda b, h: (row_blk0 + b, qb + h)),
                  pl.BlockSpec((t, LANES), lambda b, h: (row_blk0 + b, kb + h)),
                  pl.BlockSpec((t, 2 * DV_B), lambda b, h: (row_blk0 + b, vb + h)),
                  pl.BlockSpec((t, 2 * DV_B), lambda b, h: (row_blk0 + b, gb + h)),
                  pl.BlockSpec((2 * H_B, LANES), lambda b, h: (0, 0)),
                  pl.BlockSpec((1, 2, 2, DK_B, DV_B), lambda b, h: (b, 0, h, 0, 0)),
                  pl.BlockSpec((1, 2 * DV_B), lambda b, h: (0, h))],
        out_specs=[pl.BlockSpec((t, 2 * DV_B), lambda b, h: (b, h)),
                   pl.BlockSpec((1, 2, 2, DK_B, DV_B), lambda b, h: (b, 0, h, 0, 0))],
        out_shape=[jax.ShapeDtypeStruct((batch * t, B_V), F32),
                   jax.ShapeDtypeStruct((batch, 2, H_B, DK_B, DV_B), F32)],
        scratch_shapes=[pltpu.VMEM((t, 2 * DV_B), F32), pltpu.VMEM((4, LANES, DV_B), F32)],
        compiler_params=_cparams("parallel", "parallel"),
    )(p, p, p, p, lg_rows, s0, gn_g.reshape(1, B_V))


GC = 256


def _unit_tri_inverses(mats):
    ii = lax.broadcasted_iota(jnp.int32, (GC, GC), 0)
    jj = lax.broadcasted_iota(jnp.int32, (GC, GC), 1)
    block_dist = ii ^ jj
    eye = (ii == jj).astype(F32)
    ms = [jnp.where((block_dist >> 3) == 0, -a, 0.0) for a in mats]
    invs = [eye + m for m in ms]
    for _ in range(2):
        ms = [_bdot(m, m) for m in ms]
        invs = [inv + _bdot(inv, m) for inv, m in zip(invs, ms)]
    for shift in range(3, int(math.log2(GC))):
        ls = [jnp.where((block_dist >> shift) == 1, a, 0.0) for a in mats]
        ts = [_bdot(inv, l) for inv, l in zip(invs, ls)]
        invs = [inv - _bdot(t, inv) for inv, t in zip(invs, ts)]
    return invs


def _gdn_kernel(q_ref, k_ref, v_ref, z_ref, gb_ref, gr_ref, s0_ref, ng_ref, o_ref, s_out_ref,
                ob_ref, st_ref, wq_s, u_s, a_s, kg_s, *, t):
    nc = t // GC
    h = pl.program_id(1)
    ii = lax.broadcasted_iota(jnp.int32, (GC, GC), 0)
    jj = lax.broadcasted_iota(jnp.int32, (GC, GC), 1)
    incl = [ii >= jj, jj >= ii]
    strict = [ii > jj, jj > ii]
    lane32 = lax.broadcasted_iota(jnp.int32, (1, 4 * H_C), 1)
    for d in range(2):
        st_ref[d] = s0_ref[0, d, 0]

    def col(x, idx):
        return jnp.sum(jnp.where(lane32 == idx, x, 0.0), axis=1, keepdims=True)

    def gate_last(c, d):
        grow = gr_ref[0, d, 0, pl.ds(c, 1), :]
        return grow, (grow[:, GC - 1:GC] if d == 0 else grow[:, 0:1])

    def prep(cb, carry):
        mats, rhs, dst = [], [], []
        for ci in range(prep_chunks):
            c = cb * prep_chunks + ci
            rows = pl.ds(pl.multiple_of(c * GC, GC), GC)
            qc = q_ref[rows, :] * (DK_C ** -0.5)
            kc = k_ref[rows, :]
            vc = v_ref[rows, :]
            gb = gb_ref[rows, :]
            kk = _bdot_nt(kc, kc)
            qk = _bdot_nt(qc, kc)
            r2 = pl.multiple_of(c * 2 * GC, 2 * GC)
            for d in range(2):
                beta = col(gb, d * H_C + h)
                gcol = col(gb, 2 * H_C + d * H_C + h)
                grow, glast = gate_last(c, d)
                decay = jnp.exp(jnp.where(incl[d], gcol - grow, -jnp.inf))
                eg = jnp.exp(gcol)
                mats.append(jnp.where(strict[d], kk * beta * decay, 0.0))
                rhs.append(jnp.concatenate([kc * (beta * eg), vc * beta], axis=1).astype(BF16))
                dst.append((d, rows, r2))
                wq_s[d, pl.ds(r2 + GC, GC), :] = (qc * eg).astype(BF16)
                a_s[d, rows, :] = (qk * decay).astype(BF16)
                kg_s[d, rows, :] = (kc * jnp.exp(glast - gcol)).astype(BF16)
        wus = [jnp.dot(tinv.astype(BF16), r, preferred_element_type=F32)
               for tinv, r in zip(_unit_tri_inverses(mats), rhs)]
        for wu, (d, rows, r2) in zip(wus, dst):
            wq_s[d, pl.ds(r2, GC), :] = wu[:, :DK_C].astype(BF16)
            u_s[d, rows, :] = wu[:, DK_C:]
        return carry

    prep_chunks = 2 if nc % 2 == 0 else 1
    lax.fori_loop(0, nc // prep_chunks, prep, 0)

    def step(c, carry):
        ccs = [c, nc - 1 - c]
        rows = [pl.ds(pl.multiple_of(cc * GC, GC), GC) for cc in ccs]
        ss = [st_ref[d] for d in range(2)]
        sbs = [s.astype(BF16) for s in ss]
        wss = [jnp.dot(wq_s[d, pl.ds(pl.multiple_of(ccs[d] * 2 * GC, 2 * GC), 2 * GC), :], sbs[d],
                       preferred_element_type=F32) for d in range(2)]
        vnbs = [(u_s[d, rows[d], :] - wss[d][:GC]).astype(BF16) for d in range(2)]
        os_ = [wss[d][GC:] + jnp.dot(a_s[d, rows[d], :], vnbs[d], preferred_element_type=F32) for d in range(2)]
        kvs = [_bdot_t(kg_s[d, rows[d], :], vnbs[d]) for d in range(2)]
        for d in range(2):
            _, glast = gate_last(ccs[d], d)
            st_ref[d] = ss[d] * jnp.exp(glast) + kvs[d]
        o_ref[rows[0], :] = os_[0]
        ob_ref[rows[1], :] = os_[1]
        return carry

    lax.fori_loop(0, nc, step, 0)

    for d in range(2):
        s_out_ref[0, d, 0] = st_ref[d]

    def finish(c, carry):
        r0 = pl.multiple_of(c * GC, GC)
        o = o_ref[pl.ds(r0, GC), :] + ob_ref[pl.ds(r0, GC), :]
        y = o * lax.rsqrt(jnp.mean(o * o, axis=-1, keepdims=True) + EPS) * ng_ref[...]
        o_ref[pl.ds(r0, GC), :] = y * _silu(z_ref[pl.ds(r0, GC), :])
        return carry

    lax.fori_loop(0, nc, finish, 0)


GDN_PREP_BLOCK_ELEMS = 512 * 1024


def _gdn_prep_kernel(x_ref, w_ref, o_ref, *, t, heads):
    x = x_ref[...]
    row = lax.broadcasted_iota(jnp.int32, (t, 1), 0)
    centre = (CONV_K - 1) // 2
    y = x * w_ref[centre:centre + 1, :]
    for i in range(CONV_K):
        s = i - centre
        if s == 0:
            continue
        shifted = pltpu.roll(x, (-s) % t, axis=0)
        inside = jnp.logical_and(row + s >= 0, row + s < t)
        y = y + jnp.where(inside, shifted, 0.0) * w_ref[i:i + 1, :]
    y = _silu(y)
    for b in range(heads):
        yb = y[:, b * LANES:(b + 1) * LANES]
        inv_norm = lax.rsqrt(jnp.sum(yb * yb, axis=-1, keepdims=True) + EPS)
        is_qk = pl.program_id(1) * heads + b < 2 * H_C
        o_ref[:, b * LANES:(b + 1) * LANES] = yb * jnp.where(is_qk, inv_norm, 1.0)


def _gdn_prep(pm, conv_w, row_blk0, batch, t):
    heads = max(1, min(H_C, GDN_PREP_BLOCK_ELEMS // (t * LANES)))
    n_col = (2 * C_QK + C_V) // (heads * LANES)
    return pl.pallas_call(
        functools.partial(_gdn_prep_kernel, t=t, heads=heads),
        grid=(batch, n_col),
        in_specs=[pl.BlockSpec((t, heads * LANES), lambda b, j: (row_blk0 + b, j)),
                  pl.BlockSpec((CONV_K, heads * LANES), lambda b, j: (0, j))],
        out_specs=pl.BlockSpec((t, heads * LANES), lambda b, j: (b, j)),
        out_shape=jax.ShapeDtypeStruct((batch * t, 2 * C_QK + C_V), F32),
        compiler_params=_cparams("parallel", "parallel"),
    )(pm, conv_w)


def _gdn(qkv, pm, gbeta, grow, row_blk0, batch, t, s0, norm_g):
    nc = t // GC
    return pl.pallas_call(
        functools.partial(_gdn_kernel, t=t),
        grid=(batch, H_C),
        in_specs=[pl.BlockSpec((t, DK_C), lambda b, h: (b, h)),
                  pl.BlockSpec((t, DK_C), lambda b, h: (b, H_C + h)),
                  pl.BlockSpec((t, DV_C), lambda b, h: (b, 2 * H_C + h)),
                  pl.BlockSpec((t, DV_C), lambda b, h: (row_blk0 + b, 3 * H_C + h)),
                  pl.BlockSpec((t, 4 * H_C), lambda b, h: (row_blk0 + b, 0)),
                  pl.BlockSpec((1, 2, 1, nc, GC), lambda b, h: (b, 0, h, 0, 0)),
                  pl.BlockSpec((1, 2, 1, DK_C, DV_C), lambda b, h: (b, 0, h, 0, 0)),
                  pl.BlockSpec((1, DV_C), lambda b, h: (0, 0))],
        out_specs=[pl.BlockSpec((t, DV_C), lambda b, h: (b, h)),
                   pl.BlockSpec((1, 2, 1, DK_C, DV_C), lambda b, h: (b, 0, h, 0, 0))],
        out_shape=[jax.ShapeDtypeStruct((batch * t, C_V), F32),
                   jax.ShapeDtypeStruct((batch, 2, H_C, DK_C, DV_C), F32)],
        scratch_shapes=[pltpu.VMEM((t, DV_C), F32), pltpu.VMEM((2, DK_C, DV_C), F32),
                        pltpu.VMEM((2, 2 * t, DK_C), BF16), pltpu.VMEM((2, t, DV_C), F32),
                        pltpu.VMEM((2, t, GC), BF16), pltpu.VMEM((2, t, DK_C), BF16)],
        compiler_params=_cparams("parallel", "parallel"),
    )(qkv, qkv, qkv, pm, gbeta, grow, s0, norm_g.reshape(1, DV_C))


FFN_TM = 512


def _ffn_pre_kernel(x_ref, shift_ref, scale_ref, rw_ref, h_ref, logit_ref):
    h = x_ref[...] * (1.0 + scale_ref[0]) + shift_ref[0]
    h_ref[...] = _pack_bf16_pairs(h)
    logit_ref[...] = _dot_hp(h, rw_ref[...])


def _ffn_pre(x, mod3, router_w_pad):
    n = x.shape[0]
    tm = FFN_TM
    row = lambda i: _mod_row_of_tile(i, tm)
    return pl.pallas_call(
        _ffn_pre_kernel,
        grid=(n // tm,),
        in_specs=[pl.BlockSpec((tm, D_MODEL), lambda i: (i, 0)),
                  pl.BlockSpec((1, 1, D_MODEL), lambda i: (row(i), 0, 3)),
                  pl.BlockSpec((1, 1, D_MODEL), lambda i: (row(i), 0, 4)),
                  pl.BlockSpec((D_MODEL, LANES), lambda i: (0, 0))],
        out_specs=[pl.BlockSpec((tm, D_PACK), lambda i: (i, 0)),
                   pl.BlockSpec((tm, LANES), lambda i: (i, 0))],
        out_shape=[jax.ShapeDtypeStruct((n, D_PACK), jnp.int32),
                   jax.ShapeDtypeStruct((n, LANES), F32)],
        compiler_params=_cparams("parallel"),
    )(x, mod3, mod3, router_w_pad)


def _shared_kernel(h_ref, w13_ref, w2_ref, sh_ref):
    hb = _unpack_bf16_pairs(h_ref[...]).astype(BF16)
    up = jnp.dot(hb, w13_ref[...], preferred_element_type=F32)
    hid = _silu(up[:, :D_SHARED]) * up[:, D_SHARED:]
    sh_ref[...] = jnp.dot(hid.astype(BF16), w2_ref[...], preferred_element_type=F32).astype(sh_ref.dtype)


def _shared_expert(hp, ws13, ws2):
    n = hp.shape[0]
    tm = FFN_TM
    return pl.pallas_call(
        _shared_kernel,
        grid=(n // tm,),
        in_specs=[pl.BlockSpec((tm, D_PACK), lambda i: (i, 0)),
                  pl.BlockSpec((D_MODEL, 2 * D_SHARED), lambda i: (0, 0)),
                  pl.BlockSpec((D_SHARED, D_MODEL), lambda i: (0, 0))],
        out_specs=pl.BlockSpec((tm, D_MODEL), lambda i: (i, 0)),
        out_shape=jax.ShapeDtypeStruct((n, D_MODEL), BF16),
        compiler_params=_cparams("parallel"),
    )(hp, ws13, ws2)


ROUTE_T = 512
PER_GROUP = N_EXPERTS // N_GROUPS
NEG_INF = float("-inf")


def _first_max(x, iota, n):
    m = jnp.max(x, axis=0, keepdims=True)
    first = jnp.min(jnp.where(x == m, iota, n), axis=0, keepdims=True)
    return m, iota == first


def _route_kernel(lg_ref, bias_ref, tri_ref, idx_ref, w_ref, rank_ref, cnt_ref, carry_ref):
    @pl.when(pl.program_id(0) == 0)
    def _():
        carry_ref[...] = jnp.zeros_like(carry_ref)

    t = lg_ref.shape[0]
    logits = jnp.transpose(lg_ref[...])[:N_EXPERTS]
    scores = 1.0 / (1.0 + jnp.exp(-logits))
    sel = scores + bias_ref[...]
    sub_g = lax.broadcasted_iota(jnp.int32, (PER_GROUP, t), 0)
    sub_e = lax.broadcasted_iota(jnp.int32, (N_EXPERTS, t), 0)
    grp_rows = []
    for g in range(N_GROUPS):
        x = sel[g * PER_GROUP:(g + 1) * PER_GROUP]
        m1, hit = _first_max(x, sub_g, PER_GROUP)
        m2 = jnp.max(jnp.where(hit, NEG_INF, x), axis=0, keepdims=True)
        grp_rows.append(m1 + m2)
    cur = jnp.concatenate(grp_rows, axis=0)
    sub_grp = lax.broadcasted_iota(jnp.int32, (N_GROUPS, t), 0)
    grp_on = jnp.zeros((N_GROUPS, t), F32)
    for _ in range(TOPK_GROUPS):
        _, hit = _first_max(cur, sub_grp, N_GROUPS)
        grp_on = jnp.where(hit, 1.0, grp_on)
        cur = jnp.where(hit, NEG_INF, cur)
    exp_on = jnp.concatenate([jnp.broadcast_to(grp_on[g:g + 1], (PER_GROUP, t)) for g in range(N_GROUPS)], axis=0)
    cur = jnp.where(exp_on > 0.0, sel, NEG_INF)
    hits, idx_rows = [], []
    for _ in range(TOP_K):
        _, hit = _first_max(cur, sub_e, N_EXPERTS)
        hits.append(hit)
        idx_rows.append(jnp.sum(jnp.where(hit, sub_e, 0), axis=0, keepdims=True))
        cur = jnp.where(hit, NEG_INF, cur)
    w_rows = [jnp.sum(jnp.where(hit, scores, 0.0), axis=0, keepdims=True) for hit in hits]
    total = functools.reduce(lambda a, b: a + b, w_rows)
    w_ref[...] = jnp.concatenate([w / total * ROUTED_SCALE for w in w_rows], axis=0)
    idx_ref[...] = jnp.concatenate(idx_rows, axis=0)
    chosen = functools.reduce(lambda a, b: a + b, [jnp.where(hit, 1.0, 0.0) for hit in hits])
    incl = jnp.dot(chosen.astype(BF16), tri_ref[...], preferred_element_type=F32)
    before = incl - chosen + carry_ref[:, :1]
    rank_ref[...] = jnp.concatenate(
        [jnp.sum(jnp.where(hit, before, 0.0), axis=0, keepdims=True) for hit in hits], axis=0).astype(jnp.int32)
    carry_ref[...] = carry_ref[...] + incl[:, t - 1:t]
    cnt_ref[...] = carry_ref[...]


def _route(logits, router_bias):
    n = logits.shape[0]
    t = ROUTE_T
    tri = (jnp.arange(t)[:, None] <= jnp.arange(t)[None, :]).astype(BF16)
    bias_b = jnp.broadcast_to(router_bias.astype(F32)[:, None], (N_EXPERTS, t))
    slot = pl.BlockSpec((TOP_K, t), lambda i: (0, i))
    idx, w, rank, cnt = pl.pallas_call(
        _route_kernel,
        grid=(n // t,),
        in_specs=[pl.BlockSpec((t, LANES), lambda i: (i, 0)),
                  pl.BlockSpec((N_EXPERTS, t), lambda i: (0, 0)),
                  pl.BlockSpec((t, t), lambda i: (0, 0))],
        out_specs=[slot, slot, slot, pl.BlockSpec((N_EXPERTS, LANES), lambda i: (0, 0))],
        out_shape=[jax.ShapeDtypeStruct((TOP_K, n), jnp.int32), jax.ShapeDtypeStruct((TOP_K, n), F32),
                   jax.ShapeDtypeStruct((TOP_K, n), jnp.int32), jax.ShapeDtypeStruct((N_EXPERTS, LANES), F32)],
        scratch_shapes=[pltpu.VMEM((N_EXPERTS, LANES), F32)],
        compiler_params=_cparams("arbitrary"),
    )(logits, bias_b, tri)
    return idx, w, rank, cnt[:, 0].astype(jnp.int32)


def _experts_kernel(be_ref, bv_ref, x_ref, w1_ref, w3_ref, w2_ref, o_ref, w1b, w3b, w2b):
    i = pl.program_id(0)
    changed = jnp.logical_or(i == 0, be_ref[i] != be_ref[jnp.maximum(i - 1, 0)])

    @pl.when(changed)
    def _():
        w1b[...] = w1_ref[0, 0].astype(BF16)
        w3b[...] = w3_ref[0, 0].astype(BF16)
        w2b[...] = w2_ref[0, 0].astype(BF16)

    @pl.when(bv_ref[i] > 0)
    def _():
        live = lax.broadcasted_iota(jnp.int32, (MOE_BLOCK, 1), 0) < bv_ref[i]
        xb = _unpack_bf16_pairs(jnp.where(live, x_ref[...], 0)).astype(BF16)
        hid = _silu(jnp.dot(xb, w1b[...], preferred_element_type=F32)) * \
            jnp.dot(xb, w3b[...], preferred_element_type=F32)
        o_ref[...] = _pack_bf16_pairs(jnp.dot(hid.astype(BF16), w2b[...], preferred_element_type=F32))

    @pl.when(bv_ref[i] == 0)
    def _():
        o_ref[...] = jnp.zeros_like(o_ref)


def _grouped_experts(xs, block_expert, block_valid, w1, w3, w2, layer):
    rows = xs.shape[0]
    n_blocks = rows // MOE_BLOCK
    grid_spec = pltpu.PrefetchScalarGridSpec(
        num_scalar_prefetch=2,
        grid=(n_blocks,),
        in_specs=[pl.BlockSpec((MOE_BLOCK, D_PACK), lambda i, be, bv: (i, 0)),
                  pl.BlockSpec((1, 1, D_MODEL, D_EXPERT), lambda i, be, bv: (layer, be[i], 0, 0)),
                  pl.BlockSpec((1, 1, D_MODEL, D_EXPERT), lambda i, be, bv: (layer, be[i], 0, 0)),
                  pl.BlockSpec((1, 1, D_EXPERT, D_MODEL), lambda i, be, bv: (layer, be[i], 0, 0))],
        out_specs=pl.BlockSpec((MOE_BLOCK, D_PACK), lambda i, be, bv: (i, 0)),
        scratch_shapes=[pltpu.VMEM((D_MODEL, D_EXPERT), BF16), pltpu.VMEM((D_MODEL, D_EXPERT), BF16),
                        pltpu.VMEM((D_EXPERT, D_MODEL), BF16)])
    return pl.pallas_call(
        _experts_kernel,
        grid_spec=grid_spec,
        out_shape=jax.ShapeDtypeStruct((rows, D_PACK), jnp.int32),
        compiler_params=_cparams("arbitrary"),
    )(block_expert, block_valid, xs, w1, w3, w2)


SC_CORES = 2
SC_SUBCORES = 16
SC_WORKERS = SC_CORES * SC_SUBCORES
SC_ROWS = 64


def _sc_mesh():
    return plsc.VectorSubcoreMesh(core_axis_name="c", subcore_axis_name="s",
                                  num_cores=SC_CORES, num_subcores=SC_SUBCORES)


def _sc_dispatch(table, dest, rows):
    n, d = table.shape
    kk = dest.shape[0]
    per_worker = n // SC_WORKERS
    n_chunks = per_worker // SC_ROWS
    assert per_worker * SC_WORKERS == n and n_chunks * SC_ROWS == per_worker
    idx = dest.reshape(kk, SC_WORKERS, n_chunks, SC_ROWS).transpose(1, 2, 0, 3).reshape(
        SC_WORKERS, n_chunks * kk, SC_ROWS)

    @functools.partial(
        pl.kernel, mesh=_sc_mesh(),
        out_type=jax.ShapeDtypeStruct((rows, d), table.dtype),
        scratch_types=[pltpu.VMEM((n_chunks * kk, SC_ROWS), jnp.int32),
                       pltpu.VMEM((SC_ROWS, d), table.dtype),
                       pltpu.SemaphoreType.DMA])
    def dispatch(table_hbm, idx_hbm, out_hbm, idx_v, rows_v, sem):
        wid = lax.axis_index("s") * SC_CORES + lax.axis_index("c")
        base = wid * per_worker
        pltpu.sync_copy(idx_hbm.at[wid], idx_v)

        @pl.loop(0, n_chunks)
        def _(j):
            pltpu.sync_copy(table_hbm.at[pl.ds(base + j * SC_ROWS, SC_ROWS)], rows_v)
            copies = [pltpu.async_copy(rows_v, out_hbm.at[idx_v.at[j * kk + k]], sem) for k in range(kk)]
            for cp in copies:
                cp.wait()

    return dispatch(table, idx)


def _sc_gather(table, idx):
    b = idx.shape[0]
    d = table.shape[1]
    per_worker = b // SC_WORKERS
    n_chunks = per_worker // SC_ROWS
    assert per_worker * SC_WORKERS == b and n_chunks * SC_ROWS == per_worker

    @functools.partial(
        pl.kernel, mesh=_sc_mesh(),
        out_type=jax.ShapeDtypeStruct((b, d), table.dtype),
        scratch_types=[pltpu.VMEM((n_chunks, SC_ROWS), jnp.int32),
                       pltpu.VMEM((SC_ROWS, d), table.dtype),
                       pltpu.SemaphoreType.DMA])
    def gather(table_hbm, idx_hbm, out_hbm, idx_v, rows_v, sem):
        wid = lax.axis_index("s") * SC_CORES + lax.axis_index("c")
        base = wid * per_worker
        pltpu.sync_copy(idx_hbm.at[wid], idx_v)

        @pl.loop(0, n_chunks)
        def _(j):
            pltpu.async_copy(table_hbm.at[idx_v.at[j]], rows_v, sem).wait()
            pltpu.sync_copy(rows_v, out_hbm.at[pl.ds(base + j * SC_ROWS, SC_ROWS)])

    return gather(table, idx.reshape(SC_WORKERS, n_chunks, SC_ROWS))


POST_TM = 512


def _ffn_post_kernel(x_ref, yg_ref, w_ref, s_ref, gate_ref, g_ref, b_ref, o_ref):
    w = w_ref[...]
    routed = _unpack_bf16_pairs(yg_ref[0]) * w[:, 0:1]
    for k in range(1, TOP_K):
        routed = routed + _unpack_bf16_pairs(yg_ref[k]) * w[:, k:k + 1]
    r = DEEPNORM_ALPHA * x_ref[...] + gate_ref[0] * (routed + s_ref[...])
    o_ref[...] = _layer_norm_rows(r, g_ref[...], b_ref[...])


def _ffn_post(x, yg, w_tok, shared, mod3, ln_g, ln_b):
    n = x.shape[0]
    tm = POST_TM
    row = lambda i: _mod_row_of_tile(i, tm)
    tile = pl.BlockSpec((tm, D_MODEL), lambda i: (i, 0))
    vec = pl.BlockSpec((1, D_MODEL), lambda i: (0, 0))
    return pl.pallas_call(
        _ffn_post_kernel,
        grid=(n // tm,),
        in_specs=[tile, pl.BlockSpec((TOP_K, tm, D_PACK), lambda i: (0, i, 0)),
                  pl.BlockSpec((tm, TOP_K), lambda i: (i, 0)), tile,
                  pl.BlockSpec((1, 1, D_MODEL), lambda i: (row(i), 0, 5)), vec, vec],
        out_specs=tile,
        out_shape=jax.ShapeDtypeStruct((n, D_MODEL), F32),
        compiler_params=_cparams("parallel"),
    )(x, yg, w_tok, shared, mod3, ln_g.reshape(1, D_MODEL), ln_b.reshape(1, D_MODEL))


def _moe_routed(h, idx, rank, counts, w1, w3, w2, layer):
    n = h.shape[0]
    padded = (counts + MOE_BLOCK - 1) // MOE_BLOCK * MOE_BLOCK
    pad_end = jnp.cumsum(padded)
    pad_start = pad_end - padded
    experts = jnp.arange(N_EXPERTS, dtype=jnp.int32)
    dest = jnp.sum(jnp.where(idx[:, :, None] == experts, pad_start, 0), axis=-1) + rank
    n_blocks = n * TOP_K // MOE_BLOCK + N_EXPERTS
    rows = n_blocks * MOE_BLOCK
    block_start = jnp.arange(n_blocks, dtype=jnp.int32) * MOE_BLOCK
    block_expert = jnp.minimum(jnp.sum(pad_end[None, :] <= block_start[:, None], axis=1), N_EXPERTS - 1).astype(jnp.int32)
    of_block = block_expert[:, None] == experts
    used = jnp.sum(jnp.where(of_block, counts, 0), axis=1) - (block_start - jnp.sum(jnp.where(of_block, pad_start, 0), axis=1))
    block_valid = jnp.clip(used, 0, MOE_BLOCK).astype(jnp.int32)
    xs = _sc_dispatch(h, dest, rows)
    ys = _grouped_experts(xs, block_expert, block_valid, w1, w3, w2, layer)
    return _sc_gather(ys, dest.reshape(-1)).reshape(TOP_K, n, D_PACK)


def _even_layer(x, mod3, j, w_in, w_out, q_norm, k_norm, log_decay, gn_g,
                cache_k, cache_v, state_ret):
    (p,) = _modulated_proj(x, mod3, 0, w_in.astype(BF16), [EVEN_IN], [F32])
    q, k = _attn_prep(p, q_norm, k_norm)
    v = p[:, A_Q + A_KV:A_Q + 2 * A_KV]
    k_p = k[:N_P].reshape(BATCH, SEQ, A_KV)
    v_p = v[:N_P].reshape(BATCH, SEQ, A_KV)
    o_p = _attention(q, k_p, v_p, 0, BATCH, SEQ)
    k_all = jnp.concatenate([k[N_P:].reshape(DEC_BATCH, DEC_SEQ, A_KV),
                             cache_k.reshape(DEC_BATCH, PAST_LEN, A_KV)], axis=1)
    v_all = jnp.concatenate([v[N_P:].reshape(DEC_BATCH, DEC_SEQ, A_KV),
                             cache_v.reshape(DEC_BATCH, PAST_LEN, A_KV)], axis=1)
    o_s = _attention(q, k_all, v_all, N_P // DEC_SEQ, DEC_BATCH, DEC_SEQ)
    o_attn = jnp.concatenate([o_p, o_s], axis=0)
    lg_rows = jnp.broadcast_to(log_decay.reshape(2 * H_B, 1), (2 * H_B, LANES))
    zeros_s = jnp.zeros((BATCH, 2, H_B, DK_B, DV_B), F32)
    r_p, s_p = _retention(p, 0, BATCH, SEQ, lg_rows, zeros_s, gn_g)
    r_s, _ = _retention(p, N_P // DEC_SEQ, DEC_BATCH, DEC_SEQ, lg_rows, state_ret, gn_g)
    o_ret = jnp.concatenate([r_p, r_s], axis=0)
    return ([o_attn, o_ret], w_out.astype(BF16), k_p.reshape(BATCH, SEQ, N_KV_A, HD_A),
            v_p.reshape(BATCH, SEQ, N_KV_A, HD_A), s_p)


def _chunk_cumsum(g, batch, t):
    gc = g.reshape(batch, t // GC, GC, 2, H_C)
    f = jnp.cumsum(gc[:, :, :, 0], axis=2)
    b = jnp.cumsum(gc[:, :, ::-1, 1], axis=2)[:, :, ::-1]
    return jnp.stack([f, b], axis=3)


def _odd_layer(x, mod3, j, w_in, conv_w, a_log, dt_bias, norm_g, w_out, state_gdn):
    w_main = w_in[:, :2 * C_QK + 2 * C_V].astype(BF16)
    w_ab = jnp.pad(w_in[:, 2 * C_QK + 2 * C_V:], ((0, 0), (0, LANES - 4 * H_C))).astype(BF16)
    w_cat = jnp.concatenate([w_main, w_ab], axis=1)
    pm, pab = _modulated_proj(x, mod3, 0, w_cat, [2 * C_QK + 2 * C_V, LANES], [F32, F32])
    qkv_p = _gdn_prep(pm, conv_w, 0, BATCH, SEQ)
    qkv_s = _gdn_prep(pm, conv_w, N_P // DEC_SEQ, DEC_BATCH, DEC_SEQ)
    ab = pab[:, :4 * H_C].reshape(N_TOK, 2, 2, H_C)
    beta = jax.nn.sigmoid(ab[:, 0])
    g = -jnp.exp(a_log.astype(F32)) * jax.nn.softplus(ab[:, 1] + dt_bias.astype(F32))

    gcs_p = _chunk_cumsum(g[:N_P], BATCH, SEQ)
    gcs_s = _chunk_cumsum(g[N_P:], DEC_BATCH, DEC_SEQ)
    gcol = jnp.concatenate([gcs_p.reshape(N_P, 2 * H_C), gcs_s.reshape(N_S, 2 * H_C)], axis=0)
    gbeta = jnp.concatenate([beta.reshape(N_TOK, 2 * H_C), gcol], axis=1)
    zeros_s = jnp.zeros((BATCH, 2, H_C, DK_C, DV_C), F32)
    o_p, s_p = _gdn(qkv_p, pm, gbeta, gcs_p.transpose(0, 3, 4, 1, 2), 0, BATCH, SEQ, zeros_s, norm_g)
    o_s, _ = _gdn(qkv_s, pm, gbeta, gcs_s.transpose(0, 3, 4, 1, 2), N_P // DEC_SEQ, DEC_BATCH, DEC_SEQ,
                  state_gdn, norm_g)
    return [jnp.concatenate([o_p, o_s], axis=0)], w_out.astype(BF16), s_p


def kernel(x_prompt, x_sample, cache_attn_k, cache_attn_v, state_ret, state_gdn, c, c_ctx, mod_w, mod_b, ln_g, ln_b, even_w_in, even_w_out, attn_q_norm, attn_k_norm, ret_log_decay, ret_norm_g, odd_w_in, gdn_conv_w, gdn_a_log, gdn_dt_bias, gdn_norm_g, odd_w_out, router_w, router_bias, expert_w1, expert_w3, expert_w2, shared_w1, shared_w3, shared_w2):
    x = jnp.concatenate([x_prompt.reshape(N_P, D_MODEL), x_sample.reshape(N_S, D_MODEL)], axis=0)
    cvec = jnp.concatenate([c_ctx[None, :], c, jnp.zeros((MOD_ROWS - N_MOD, D_MODEL), F32)], axis=0)
    mod_all = _mod_vectors(cvec, mod_w, mod_b)
    new_k, new_v, new_ret, new_gdn = [], [], [], []
    for l in range(DEPTH):
        j = l // 2
        mod3 = mod_all[l].reshape(MOD_ROWS, 1, 6 * D_MODEL)
        if l % 2 == 0:
            a_list, w_o, k_p, v_p, s_p = _even_layer(
                x, mod3, j, even_w_in[j], even_w_out[j], attn_q_norm[j], attn_k_norm[j],
                ret_log_decay[j], ret_norm_g[j], cache_attn_k[:, j], cache_attn_v[:, j], state_ret[:, j])
            new_k.append(k_p)
            new_v.append(v_p)
            new_ret.append(s_p)
        else:
            a_list, w_o, s_p = _odd_layer(
                x, mod3, j, odd_w_in[j], gdn_conv_w[j], gdn_a_log[j], gdn_dt_bias[j], gdn_norm_g[j],
                odd_w_out[j], state_gdn[:, j])
            new_gdn.append(s_p)
        x = _outproj_ln(a_list, w_o, x, mod3, 2, ln_g[l, 0], ln_b[l, 0])
        rw = jnp.pad(router_w[l], ((0, 0), (0, LANES - N_EXPERTS)))
        ws13 = jnp.concatenate([shared_w1[l], shared_w3[l]], axis=1).astype(BF16)
        h, logits = _ffn_pre(x, mod3, rw)
        idx, w, rank, counts = _route(logits, router_bias[l])
        yg = _moe_routed(h, idx, rank, counts, expert_w1, expert_w3, expert_w2, l)
        shared = _shared_expert(h, ws13, shared_w2[l].astype(BF16))
        x = _ffn_post(x, yg, w.T, shared, mod3, ln_g[l, 1], ln_b[l, 1])
    return (x[:N_P].reshape(BATCH, SEQ, D_MODEL), x[N_P:].reshape(DEC_BATCH, DEC_SEQ, D_MODEL),
            jnp.stack(new_k, axis=1), jnp.stack(new_v, axis=1),
            jnp.stack(new_ret, axis=1), jnp.stack(new_gdn, axis=1))
```

```python
import functools
import math

import jax
import jax.numpy as jnp
from jax import lax
from jax.experimental import pallas as pl
from jax.experimental.pallas import tpu as pltpu
from jax.experimental.pallas import tpu_sc as plsc

D_MODEL = 1024
BATCH = 16
SEQ = 256
DEPTH = 4
DEC_BATCH = 4
DEC_SEQ = 4096
PAST_LEN = 512
GRID_W = 64
N_HEADS_A = 8
N_KV_A = 2
HD_A = 64
ROPE_THETA = 10000.0
H_B = 4
DK_B = 64
DV_B = 128
RET_CHUNK = 128
H_C = 8
DK_C = 128
DV_C = 128
CONV_K = 5
GDN_CHUNK = 64
N_EXPERTS = 64
TOP_K = 8
N_GROUPS = 8
TOPK_GROUPS = 4
D_EXPERT = 256
D_SHARED = 256
ROUTED_SCALE = 2.5
MOE_BLOCK = 512
A_Q = N_HEADS_A * HD_A
A_KV = N_KV_A * HD_A
B_QK = H_B * DK_B
B_V = H_B * DV_B
EVEN_IN = A_Q + 2 * A_KV + 2 * B_QK + 2 * B_V
C_QK = H_C * DK_C
C_V = H_C * DV_C
DEEPNORM_ALPHA = (2 * DEPTH) ** 0.25
EPS = 1e-6

N_P = BATCH * SEQ
N_S = DEC_BATCH * DEC_SEQ
N_TOK = N_P + N_S
N_MOD = 1 + DEC_BATCH
MOD_ROWS = 8

LANES = 128
VMEM_LIMIT = 56 * 1024 * 1024

F32 = jnp.float32
BF16 = jnp.bfloat16


def _cparams(*sem):
    return pltpu.CompilerParams(dimension_semantics=sem, vmem_limit_bytes=VMEM_LIMIT)


def _bdot(a, b):
    return jnp.dot(a.astype(BF16), b.astype(BF16), preferred_element_type=F32)


def _bdot_t(a, b):
    return lax.dot_general(a.astype(BF16), b.astype(BF16), (((0,), (0,)), ((), ())),
                           preferred_element_type=F32)


def _bdot_nt(a, b):
    return lax.dot_general(a.astype(BF16), b.astype(BF16), (((1,), (1,)), ((), ())),
                           preferred_element_type=F32)


def _split3(a):
    hi = a.astype(BF16)
    r = a - hi.astype(F32)
    mid = r.astype(BF16)
    lo = (r - mid.astype(F32)).astype(BF16)
    return hi, mid, lo


def _dot_hp(a, b):
    a0, a1, a2 = _split3(a)
    b0, b1, b2 = _split3(b)
    d = lambda x, y: jnp.dot(x, y, preferred_element_type=F32)
    small = d(a0, b2) + d(a2, b0) + d(a1, b1)
    return (d(a0, b1) + d(a1, b0)) + small + d(a0, b0)


def _silu(x):
    return x * (1.0 / (1.0 + jnp.exp(-x)))


HI16 = 0xFFFF0000
D_PACK = D_MODEL // 2


def _pack_bf16_pairs(x):
    c = x.shape[1] // 2
    lo = pltpu.bitcast(x[:, :c].astype(BF16).astype(F32), jnp.uint32) >> 16
    hi = pltpu.bitcast(x[:, c:].astype(BF16).astype(F32), jnp.uint32) & jnp.uint32(HI16)
    return pltpu.bitcast(lo | hi, jnp.int32)


def _unpack_bf16_pairs(p):
    u = pltpu.bitcast(p, jnp.uint32)
    return jnp.concatenate([pltpu.bitcast(u << 16, F32), pltpu.bitcast(u & jnp.uint32(HI16), F32)], axis=1)


def _mod_row_of_tile(i, tile):
    tiles_p = N_P // tile
    tiles_per_b = DEC_SEQ // tile
    return jnp.where(i < tiles_p, 0, 1 + (i - tiles_p) // tiles_per_b)


MOD_TN = 1536


def _mod_kernel(c_ref, w_ref, b_ref, o_ref):
    a = _silu(c_ref[...])
    o_ref[0] = _bdot(a, w_ref[0]) + b_ref[0]


def _mod_vectors(cvec, mod_w, mod_b):
    n6 = 6 * D_MODEL
    return pl.pallas_call(
        _mod_kernel,
        grid=(DEPTH, n6 // MOD_TN),
        in_specs=[pl.BlockSpec((MOD_ROWS, D_MODEL), lambda l, j: (0, 0)),
                  pl.BlockSpec((1, D_MODEL, MOD_TN), lambda l, j: (l, 0, j)),
                  pl.BlockSpec((1, 1, MOD_TN), lambda l, j: (l, 0, j))],
        out_specs=pl.BlockSpec((1, MOD_ROWS, MOD_TN), lambda l, j: (l, 0, j)),
        out_shape=jax.ShapeDtypeStruct((DEPTH, MOD_ROWS, n6), F32),
        compiler_params=_cparams("parallel", "parallel"),
    )(cvec, mod_w, mod_b.reshape(DEPTH, 1, n6))


PROJ_TM = 512


def _proj_kernel(x_ref, shift_ref, scale_ref, w_ref, *o_refs, widths):
    h = (x_ref[...] * (1.0 + scale_ref[0]) + shift_ref[0]).astype(BF16)
    off = 0
    for o_ref, wd in zip(o_refs, widths):
        o_ref[...] = jnp.dot(h, w_ref[:, off:off + wd], preferred_element_type=F32).astype(o_ref.dtype)
        off += wd


def _modulated_proj(x, mod3, shift_blk, w_bf16, widths, dtypes):
    n = x.shape[0]
    tm = PROJ_TM
    row = lambda i: _mod_row_of_tile(i, tm)
    return pl.pallas_call(
        functools.partial(_proj_kernel, widths=tuple(widths)),
        grid=(n // tm,),
        in_specs=[pl.BlockSpec((tm, D_MODEL), lambda i: (i, 0)),
                  pl.BlockSpec((1, 1, D_MODEL), lambda i: (row(i), 0, shift_blk)),
                  pl.BlockSpec((1, 1, D_MODEL), lambda i: (row(i), 0, shift_blk + 1)),
                  pl.BlockSpec((D_MODEL, sum(widths)), lambda i: (0, 0))],
        out_specs=[pl.BlockSpec((tm, wd), lambda i: (i, 0)) for wd in widths],
        out_shape=[jax.ShapeDtypeStruct((n, wd), dt) for wd, dt in zip(widths, dtypes)],
        compiler_params=_cparams("parallel"),
    )(x, mod3, mod3, w_bf16)


OUT_TM = 512


def _layer_norm_rows(r, g, b):
    mu = jnp.mean(r, axis=-1, keepdims=True)
    d = r - mu
    var = jnp.mean(d * d, axis=-1, keepdims=True)
    return d * lax.rsqrt(var + EPS) * g + b


def _outproj_kernel(*refs, n_a):
    a_refs = refs[:n_a]
    w_ref, x_ref, gate_ref, g_ref, b_ref, o_ref = refs[n_a:]
    off = 0
    acc = None
    for a_ref in a_refs:
        wd = a_ref.shape[1]
        part = jnp.dot(a_ref[...].astype(BF16), w_ref[off:off + wd, :], preferred_element_type=F32)
        acc = part if acc is None else acc + part
        off += wd
    r = DEEPNORM_ALPHA * x_ref[...] + gate_ref[0] * acc
    o_ref[...] = _layer_norm_rows(r, g_ref[...], b_ref[...])


def _outproj_ln(a_list, w_bf16, x, mod3, gate_blk, ln_g, ln_b):
    n = x.shape[0]
    tm = OUT_TM
    row = lambda i: _mod_row_of_tile(i, tm)
    kdim = w_bf16.shape[0]
    return pl.pallas_call(
        functools.partial(_outproj_kernel, n_a=len(a_list)),
        grid=(n // tm,),
        in_specs=[pl.BlockSpec((tm, a.shape[1]), lambda i: (i, 0)) for a in a_list] + [
            pl.BlockSpec((kdim, D_MODEL), lambda i: (0, 0)),
            pl.BlockSpec((tm, D_MODEL), lambda i: (i, 0)),
            pl.BlockSpec((1, 1, D_MODEL), lambda i: (row(i), 0, gate_blk)),
            pl.BlockSpec((1, D_MODEL), lambda i: (0, 0)),
            pl.BlockSpec((1, D_MODEL), lambda i: (0, 0))],
        out_specs=pl.BlockSpec((tm, D_MODEL), lambda i: (i, 0)),
        out_shape=jax.ShapeDtypeStruct((n, D_MODEL), F32),
        compiler_params=_cparams("parallel"),
    )(*a_list, w_bf16, x, mod3, ln_g.reshape(1, D_MODEL), ln_b.reshape(1, D_MODEL))


PREP_TM = 512
ROPE_SEG = HD_A // 4


def _head_rms(x, gain, ones_bd):
    parts = _split3(x * x)
    ss = functools.reduce(lambda a, b: a + b,
                          [jnp.dot(p, ones_bd, preferred_element_type=F32) for p in reversed(parts)])
    return x * lax.rsqrt(ss * (1.0 / HD_A) + EPS) * gain


def _rope_lanes(x, cos, sin):
    lane = lax.broadcasted_iota(jnp.int32, (1, LANES), 1)
    first = (lane % (2 * ROPE_SEG)) < ROPE_SEG
    cols = []
    for b in range(x.shape[1] // LANES):
        xb = x[:, b * LANES:(b + 1) * LANES]
        partner = jnp.where(first, pltpu.roll(xb, LANES - ROPE_SEG, axis=1), pltpu.roll(xb, ROPE_SEG, axis=1))
        cols.append(xb * cos + partner * sin)
    return cols[0] if len(cols) == 1 else jnp.concatenate(cols, axis=1)


def _attn_prep_kernel(q_ref, kv_ref, cos_ref, sin_ref, qg_ref, kg_ref, bd_ref, qo_ref, ko_ref):
    cos = cos_ref[...]
    sin = sin_ref[...]
    q = _rope_lanes(_head_rms(q_ref[...], qg_ref[...], bd_ref[...]), cos, sin)
    qo_ref[...] = (q * (HD_A ** -0.5)).astype(BF16)
    k = _head_rms(kv_ref[:, :A_KV], kg_ref[...], bd_ref[:A_KV, :A_KV])
    ko_ref[...] = _rope_lanes(k, cos, sin)


def _rope_tables():
    rows = DEC_SEQ // GRID_W
    row = jnp.repeat(jnp.arange(rows), GRID_W).astype(F32)
    colp = jnp.tile(jnp.arange(GRID_W), rows).astype(F32)
    inv_freq = ROPE_THETA ** (-jnp.arange(ROPE_SEG, dtype=F32) / ROPE_SEG)
    ar = row[:, None] * inv_freq[None, :]
    ac = colp[:, None] * inv_freq[None, :]
    cos = jnp.concatenate([jnp.cos(ar), jnp.cos(ar), jnp.cos(ac), jnp.cos(ac)], axis=-1)
    sin = jnp.concatenate([-jnp.sin(ar), jnp.sin(ar), -jnp.sin(ac), jnp.sin(ac)], axis=-1)
    cos = jnp.concatenate([jnp.ones((PREP_TM, HD_A), F32), cos], axis=0)
    sin = jnp.concatenate([jnp.zeros((PREP_TM, HD_A), F32), sin], axis=0)
    return jnp.tile(cos, (1, LANES // HD_A)), jnp.tile(sin, (1, LANES // HD_A))


def _attn_prep(p, q_norm, k_norm):
    tm = PREP_TM
    tiles_p = N_P // tm
    tiles_seq = DEC_SEQ // tm
    tab = lambda i: (jnp.where(i < tiles_p, 0, 1 + (i - tiles_p) % tiles_seq), 0)
    cos, sin = _rope_tables()
    head_id = jnp.arange(A_Q) // HD_A
    ones_bd = (head_id[:, None] == head_id[None, :]).astype(BF16)
    return pl.pallas_call(
        _attn_prep_kernel,
        grid=(N_TOK // tm,),
        in_specs=[pl.BlockSpec((tm, A_Q), lambda i: (i, 0)),
                  pl.BlockSpec((tm, 2 * A_KV), lambda i: (i, A_Q // (2 * A_KV))),
                  pl.BlockSpec((tm, LANES), tab), pl.BlockSpec((tm, LANES), tab),
                  pl.BlockSpec((1, A_Q), lambda i: (0, 0)), pl.BlockSpec((1, A_KV), lambda i: (0, 0)),
                  pl.BlockSpec((A_Q, A_Q), lambda i: (0, 0))],
        out_specs=[pl.BlockSpec((tm, A_Q), lambda i: (i, 0)), pl.BlockSpec((tm, A_KV), lambda i: (i, 0))],
        out_shape=[jax.ShapeDtypeStruct((N_TOK, A_Q), BF16), jax.ShapeDtypeStruct((N_TOK, A_KV), F32)],
        compiler_params=_cparams("parallel"),
    )(p, p, cos, sin, jnp.tile(q_norm, N_HEADS_A).reshape(1, A_Q), jnp.tile(k_norm, N_KV_A).reshape(1, A_KV),
      ones_bd)


ATT_TQ = 256
GROUP_A = N_HEADS_A // N_KV_A


def _attn_kernel(q_ref, kt_ref, kts_ref, v_ref, vs_ref, o_ref):
    low = lax.broadcasted_iota(jnp.int32, (1, LANES), 1) < HD_A
    for j in range(N_HEADS_A // 2):
        qb = q_ref[:, j * LANES:(j + 1) * LANES]
        outs = []
        for half in range(2):
            kv_head = (2 * j + half) // GROUP_A
            qh = jnp.where(low if half == 0 else jnp.logical_not(low), qb, jnp.zeros_like(qb))
            kt, v = (kt_ref[0], v_ref[0]) if half == kv_head else (kts_ref[0], vs_ref[0])
            s = jnp.dot(qh, kt, preferred_element_type=F32)
            m = jnp.max(s, axis=-1, keepdims=True)
            p = jnp.exp(s - m)
            l = jnp.sum(p, axis=-1, keepdims=True)
            pv = jnp.dot(p.astype(BF16), v, preferred_element_type=F32)
            outs.append(pv * (1.0 / l))
        o_ref[:, j * LANES:(j + 1) * LANES] = jnp.where(low, outs[0], outs[1]).astype(o_ref.dtype)


def _attention(q, k, v, row_blk0, batch, t):
    tk = v.shape[1]
    tq = min(ATT_TQ, t)
    nq = t // tq
    swap = lambda x: jnp.roll(x, HD_A, axis=-1)
    kt = jnp.swapaxes(k, 1, 2).astype(BF16)
    kts = jnp.swapaxes(swap(k), 1, 2).astype(BF16)
    kspec = pl.BlockSpec((1, LANES, tk), lambda b, i: (b, 0, 0))
    vspec = pl.BlockSpec((1, tk, LANES), lambda b, i: (b, 0, 0))
    return pl.pallas_call(
        _attn_kernel,
        grid=(batch, nq),
        in_specs=[pl.BlockSpec((tq, A_Q), lambda b, i: ((row_blk0 + b) * nq + i, 0)), kspec, kspec, vspec, vspec],
        out_specs=pl.BlockSpec((tq, A_Q), lambda b, i: (b * nq + i, 0)),
        out_shape=jax.ShapeDtypeStruct((batch * t, A_Q), BF16),
        compiler_params=_cparams("parallel", "parallel"),
    )(q, kt, kts, v.astype(BF16), swap(v).astype(BF16))


RC = RET_CHUNK


def _ret_kernel(q_ref, k_ref, v_ref, g_ref, lg_ref, s0_ref, gn_ref, o_ref, s_out_ref,
                ob_ref, st_ref, *, t):
    nc = t // RC
    hp = pl.program_id(1)
    ii = lax.broadcasted_iota(jnp.int32, (RC, RC), 0).astype(F32)
    jj = lax.broadcasted_iota(jnp.int32, (RC, RC), 1).astype(F32)
    col_i = lax.broadcasted_iota(jnp.int32, (RC, 1), 0).astype(F32)
    lane = lax.broadcasted_iota(jnp.int32, (1, LANES), 1)
    masks = [(lane >= hh * DK_B) & (lane < (hh + 1) * DK_B) for hh in range(2)]

    consts = []
    for d in range(2):
        for hh in range(2):
            lg = lg_ref[pl.ds(d * H_B + hp * 2 + hh, 1), :][:, :1]
            if d == 0:
                diff = ii - jj
                qe, ke = col_i + 1.0, (RC - 1.0) - col_i
            else:
                diff = jj - ii
                qe, ke = RC - col_i, col_i
            intra = jnp.where(diff >= 0, jnp.exp(jnp.maximum(diff, 0.0) * lg), 0.0)
            consts.append((intra, jnp.exp(qe * lg), jnp.exp(ke * lg), jnp.exp(RC * lg)))
            s0 = s0_ref[0, d, hh]
            z = jnp.zeros((DK_B, DV_B), F32)
            st_ref[d * 2 + hh] = jnp.concatenate([s0, z] if hh == 0 else [z, s0], axis=0)

    def body(c, carry):
        combos = [(d, hh) for d in range(2) for hh in range(2)]
        rows = [pl.ds(pl.multiple_of(cc * RC, RC), RC) for cc in (c, nc - 1 - c)]
        qcs = [q_ref[rows[d], :] for d in range(2)]
        kcs = [k_ref[rows[d], :] * (DK_B ** -0.5) for d in range(2)]
        qhs = [jnp.where(masks[hh], qcs[d], 0.0) for d, hh in combos]
        khs = [jnp.where(masks[hh], kcs[d], 0.0) for d, hh in combos]
        vcs = [v_ref[rows[d], hh * DV_B:(hh + 1) * DV_B].astype(BF16) for d, hh in combos]
        sts = [st_ref[d * 2 + hh] for d, hh in combos]
        scores = [_bdot_nt(qhs[i], khs[i]) * consts[i][0] for i in range(4)]
        kvs = [_bdot_t(khs[i] * consts[i][2], vcs[i]) for i in range(4)]
        outs = [jnp.dot(jnp.concatenate([scores[i], qhs[i] * consts[i][1]], axis=1).astype(BF16),
                        jnp.concatenate([vcs[i], sts[i].astype(BF16)], axis=0),
                        preferred_element_type=F32) for i in range(4)]
        for i, (d, hh) in enumerate(combos):
            st_ref[d * 2 + hh] = sts[i] * consts[i][3] + kvs[i]
        o_ref[rows[0], :] = jnp.concatenate(outs[0:2], axis=1)
        ob_ref[rows[1], :] = jnp.concatenate(outs[2:4], axis=1)
        return carry

    lax.fori_loop(0, nc, body, 0)

    for d in range(2):
        for hh in range(2):
            s_out_ref[0, d, hh] = st_ref[d * 2 + hh][hh * DK_B:(hh + 1) * DK_B, :]

    def finish(c, carry):
        r0 = pl.multiple_of(c * RC, RC)
        o = o_ref[pl.ds(r0, RC), :] + ob_ref[pl.ds(r0, RC), :]
        gate = _silu(g_ref[pl.ds(r0, RC), :])
        ys = []
        for hh in range(2):
            oh = o[:, hh * DV_B:(hh + 1) * DV_B]
            mu = jnp.mean(oh, axis=-1, keepdims=True)
            dlt = oh - mu
            var = jnp.mean(dlt * dlt, axis=-1, keepdims=True)
            ys.append(dlt * lax.rsqrt(var + EPS))
        o_ref[pl.ds(r0, RC), :] = jnp.concatenate(ys, axis=1) * gn_ref[...] * gate
        return carry

    lax.fori_loop(0, nc, finish, 0)


def _retention(p, row_blk0, batch, t, lg_rows, s0, gn_g):
    qb, kb, vb, gb = (A_Q + 2 * A_KV) // LANES, (A_Q + 2 * A_KV + B_QK) // LANES, \
        (A_Q + 2 * A_KV + 2 * B_QK) // (2 * DV_B), (A_Q + 2 * A_KV + 2 * B_QK + B_V) // (2 * DV_B)
    return pl.pallas_call(
        functools.partial(_ret_kernel, t=t),
        grid=(batch, H_B // 2),
        in_specs=[pl.BlockSpec((t, LANES), lambda b, h: (row_blk0 + b, qb + h)),
                  pl.BlockSpec((t, LANES), lambda b, h: (row_blk0 + b, kb + h)),
                  pl.BlockSpec((t, 2 * DV_B), lambda b, h: (row_blk0 + b, vb + h)),
                  pl.BlockSpec((t, 2 * DV_B), lambda b, h: (row_blk0 + b, gb + h)),
                  pl.BlockSpec((2 * H_B, LANES), lambda b, h: (0, 0)),
                  pl.BlockSpec((1, 2, 2, DK_B, DV_B), lambda b, h: (b, 0, h, 0, 0)),
                  pl.BlockSpec((1, 2 * DV_B), lambda b, h: (0, h))],
        out_specs=[pl.BlockSpec((t, 2 * DV_B), lambda b, h: (b, h)),
                   pl.BlockSpec((1, 2, 2, DK_B, DV_B), lambda b, h: (b, 0, h, 0, 0))],
        out_shape=[jax.ShapeDtypeStruct((batch * t, B_V), F32),
                   jax.ShapeDtypeStruct((batch, 2, H_B, DK_B, DV_B), F32)],
        scratch_shapes=[pltpu.VMEM((t, 2 * DV_B), F32), pltpu.VMEM((4, LANES, DV_B), F32)],
        compiler_params=_cparams("parallel", "parallel"),
    )(p, p, p, p, lg_rows, s0, gn_g.reshape(1, B_V))


GC = 256


def _unit_tri_inverses(mats):
    ii = lax.broadcasted_iota(jnp.int32, (GC, GC), 0)
    jj = lax.broadcasted_iota(jnp.int32, (GC, GC), 1)
    block_dist = ii ^ jj
    eye = (ii == jj).astype(F32)
    ms = [jnp.where((block_dist >> 3) == 0, -a, 0.0) for a in mats]
    invs = [eye + m for m in ms]
    for _ in range(2):
        ms = [_bdot(m, m) for m in ms]
        invs = [inv + _bdot(inv, m) for inv, m in zip(invs, ms)]
    for shift in range(3, int(math.log2(GC))):
        ls = [jnp.where((block_dist >> shift) == 1, a, 0.0) for a in mats]
        ts = [_bdot(inv, l) for inv, l in zip(invs, ls)]
        invs = [inv - _bdot(t, inv) for inv, t in zip(invs, ts)]
    return invs


def _gdn_kernel(q_ref, k_ref, v_ref, z_ref, gb_ref, gr_ref, s0_ref, ng_ref, o_ref, s_out_ref,
                ob_ref, st_ref, wq_s, u_s, a_s, kg_s, *, t, seqs):
    nc_seq = t // GC
    nc = seqs * nc_seq
    h = pl.program_id(1)
    ii = lax.broadcasted_iota(jnp.int32, (GC, GC), 0)
    jj = lax.broadcasted_iota(jnp.int32, (GC, GC), 1)
    incl = [ii >= jj, jj >= ii]
    strict = [ii > jj, jj > ii]
    lane32 = lax.broadcasted_iota(jnp.int32, (1, 4 * H_C), 1)
    st_ref[...] = jnp.zeros_like(st_ref)

    def col(x, idx):
        return jnp.sum(jnp.where(lane32 == idx, x, 0.0), axis=1, keepdims=True)

    def gate_last(c, d):
        grow = gr_ref[0, d, 0, pl.ds(c, 1), :]
        return grow, (grow[:, GC - 1:GC] if d == 0 else grow[:, 0:1])

    def prep(cb, carry):
        mats, rhs, dst = [], [], []
        for ci in range(prep_chunks):
            c = cb * prep_chunks + ci
            rows = pl.ds(pl.multiple_of(c * GC, GC), GC)
            qc = q_ref[rows, :] * (DK_C ** -0.5)
            kc = k_ref[rows, :]
            vc = v_ref[rows, :]
            gb = gb_ref[rows, :]
            kk = _bdot_nt(kc, kc)
            qk = _bdot_nt(qc, kc)
            r2 = pl.multiple_of(c * 2 * GC, 2 * GC)
            for d in range(2):
                beta = col(gb, d * H_C + h)
                gcol = col(gb, 2 * H_C + d * H_C + h)
                grow, glast = gate_last(c, d)
                decay = jnp.exp(jnp.where(incl[d], gcol - grow, -jnp.inf))
                eg = jnp.exp(gcol)
                mats.append(jnp.where(strict[d], kk * beta * decay, 0.0))
                rhs.append(jnp.concatenate([kc * (beta * eg), vc * beta], axis=1).astype(BF16))
                dst.append((d, rows, r2))
                wq_s[d, pl.ds(r2 + GC, GC), :] = (qc * eg).astype(BF16)
                a_s[d, rows, :] = (qk * decay).astype(BF16)
                kg_s[d, rows, :] = (kc * jnp.exp(glast - gcol)).astype(BF16)
        wus = [jnp.dot(tinv.astype(BF16), r, preferred_element_type=F32)
               for tinv, r in zip(_unit_tri_inverses(mats), rhs)]
        for wu, (d, rows, r2) in zip(wus, dst):
            wq_s[d, pl.ds(r2, GC), :] = wu[:, :DK_C].astype(BF16)
            u_s[d, rows, :] = wu[:, DK_C:]
        return carry

    prep_chunks = 2 if nc % 2 == 0 else 1
    lax.fori_loop(0, nc // prep_chunks, prep, 0)

    def step(c, carry):
        ccs = [c, nc - 1 - c]
        rows = [pl.ds(pl.multiple_of(cc * GC, GC), GC) for cc in ccs]
        sqs = [cc // nc_seq for cc in ccs]
        first = [ccs[0] % nc_seq == 0, ccs[1] % nc_seq == nc_seq - 1]
        ss = [jnp.where(first[d], s0_ref[sqs[d], d, 0], st_ref[d]) for d in range(2)]
        sbs = [s.astype(BF16) for s in ss]
        wss = [jnp.dot(wq_s[d, pl.ds(pl.multiple_of(ccs[d] * 2 * GC, 2 * GC), 2 * GC), :], sbs[d],
                       preferred_element_type=F32) for d in range(2)]
        vnbs = [(u_s[d, rows[d], :] - wss[d][:GC]).astype(BF16) for d in range(2)]
        os_ = [wss[d][GC:] + jnp.dot(a_s[d, rows[d], :], vnbs[d], preferred_element_type=F32) for d in range(2)]
        kvs = [_bdot_t(kg_s[d, rows[d], :], vnbs[d]) for d in range(2)]
        for d in range(2):
            _, glast = gate_last(ccs[d], d)
            s_new = ss[d] * jnp.exp(glast) + kvs[d]
            st_ref[d] = s_new
            s_out_ref[sqs[d], d, 0] = s_new
        o_ref[rows[0], :] = os_[0]
        ob_ref[rows[1], :] = os_[1]
        return carry

    lax.fori_loop(0, nc, step, 0)

    def finish(c, carry):
        r0 = pl.multiple_of(c * GC, GC)
        o = o_ref[pl.ds(r0, GC), :] + ob_ref[pl.ds(r0, GC), :]
        y = o * lax.rsqrt(jnp.mean(o * o, axis=-1, keepdims=True) + EPS) * ng_ref[...]
        o_ref[pl.ds(r0, GC), :] = y * _silu(z_ref[pl.ds(r0, GC), :])
        return carry

    lax.fori_loop(0, nc, finish, 0)


GDN_PREP_BLOCK_ELEMS = 512 * 1024


def _gdn_prep_kernel(x_ref, w_ref, o_ref, *, t, heads):
    x = x_ref[...]
    row = lax.broadcasted_iota(jnp.int32, (t, 1), 0)
    centre = (CONV_K - 1) // 2
    y = x * w_ref[centre:centre + 1, :]
    for i in range(CONV_K):
        s = i - centre
        if s == 0:
            continue
        shifted = pltpu.roll(x, (-s) % t, axis=0)
        inside = jnp.logical_and(row + s >= 0, row + s < t)
        y = y + jnp.where(inside, shifted, 0.0) * w_ref[i:i + 1, :]
    y = _silu(y)
    for b in range(heads):
        yb = y[:, b * LANES:(b + 1) * LANES]
        inv_norm = lax.rsqrt(jnp.sum(yb * yb, axis=-1, keepdims=True) + EPS)
        is_qk = pl.program_id(1) * heads + b < 2 * H_C
        o_ref[:, b * LANES:(b + 1) * LANES] = yb * jnp.where(is_qk, inv_norm, 1.0)


def _gdn_prep(pm, conv_w, row_blk0, batch, t):
    heads = max(1, min(H_C, GDN_PREP_BLOCK_ELEMS // (t * LANES)))
    n_col = (2 * C_QK + C_V) // (heads * LANES)
    return pl.pallas_call(
        functools.partial(_gdn_prep_kernel, t=t, heads=heads),
        grid=(batch, n_col),
        in_specs=[pl.BlockSpec((t, heads * LANES), lambda b, j: (row_blk0 + b, j)),
                  pl.BlockSpec((CONV_K, heads * LANES), lambda b, j: (0, j))],
        out_specs=pl.BlockSpec((t, heads * LANES), lambda b, j: (b, j)),
        out_shape=jax.ShapeDtypeStruct((batch * t, 2 * C_QK + C_V), F32),
        compiler_params=_cparams("parallel", "parallel"),
    )(pm, conv_w)


def _gdn(qkv, pm, gbeta, grow, row_blk0, batch, t, s0, norm_g):
    nc = t // GC
    seqs = 2 if (nc == 1 and batch % 2 == 0 and row_blk0 % 2 == 0) else 1
    tb = seqs * t
    groups = batch // seqs
    blk0 = row_blk0 // seqs
    grow = grow.reshape(groups, seqs, 2, H_C, nc, GC).transpose(0, 2, 3, 1, 4, 5).reshape(groups, 2, H_C, seqs * nc, GC)
    return pl.pallas_call(
        functools.partial(_gdn_kernel, t=t, seqs=seqs),
        grid=(groups, H_C),
        in_specs=[pl.BlockSpec((tb, DK_C), lambda b, h: (b, h)),
                  pl.BlockSpec((tb, DK_C), lambda b, h: (b, H_C + h)),
                  pl.BlockSpec((tb, DV_C), lambda b, h: (b, 2 * H_C + h)),
                  pl.BlockSpec((tb, DV_C), lambda b, h: (blk0 + b, 3 * H_C + h)),
                  pl.BlockSpec((tb, 4 * H_C), lambda b, h: (blk0 + b, 0)),
                  pl.BlockSpec((1, 2, 1, seqs * nc, GC), lambda b, h: (b, 0, h, 0, 0)),
                  pl.BlockSpec((seqs, 2, 1, DK_C, DV_C), lambda b, h: (b, 0, h, 0, 0)),
                  pl.BlockSpec((1, DV_C), lambda b, h: (0, 0))],
        out_specs=[pl.BlockSpec((tb, DV_C), lambda b, h: (b, h)),
                   pl.BlockSpec((seqs, 2, 1, DK_C, DV_C), lambda b, h: (b, 0, h, 0, 0))],
        out_shape=[jax.ShapeDtypeStruct((batch * t, C_V), F32),
                   jax.ShapeDtypeStruct((batch, 2, H_C, DK_C, DV_C), F32)],
        scratch_shapes=[pltpu.VMEM((tb, DV_C), F32), pltpu.VMEM((2, DK_C, DV_C), F32),
                        pltpu.VMEM((2, 2 * tb, DK_C), BF16), pltpu.VMEM((2, tb, DV_C), F32),
                        pltpu.VMEM((2, tb, GC), BF16), pltpu.VMEM((2, tb, DK_C), BF16)],
        compiler_params=_cparams("parallel", "parallel"),
    )(qkv, qkv, qkv, pm, gbeta, grow, s0, norm_g.reshape(1, DV_C))


FFN_TM = 512


def _ffn_pre_kernel(x_ref, shift_ref, scale_ref, rw_ref, h_ref, logit_ref):
    h = x_ref[...] * (1.0 + scale_ref[0]) + shift_ref[0]
    h_ref[...] = _pack_bf16_pairs(h)
    logit_ref[...] = _dot_hp(h, rw_ref[...])


def _ffn_pre(x, mod3, router_w_pad):
    n = x.shape[0]
    tm = FFN_TM
    row = lambda i: _mod_row_of_tile(i, tm)
    return pl.pallas_call(
        _ffn_pre_kernel,
        grid=(n // tm,),
        in_specs=[pl.BlockSpec((tm, D_MODEL), lambda i: (i, 0)),
                  pl.BlockSpec((1, 1, D_MODEL), lambda i: (row(i), 0, 3)),
                  pl.BlockSpec((1, 1, D_MODEL), lambda i: (row(i), 0, 4)),
                  pl.BlockSpec((D_MODEL, LANES), lambda i: (0, 0))],
        out_specs=[pl.BlockSpec((tm, D_PACK), lambda i: (i, 0)),
                   pl.BlockSpec((tm, LANES), lambda i: (i, 0))],
        out_shape=[jax.ShapeDtypeStruct((n, D_PACK), jnp.int32),
                   jax.ShapeDtypeStruct((n, LANES), F32)],
        compiler_params=_cparams("parallel"),
    )(x, mod3, mod3, router_w_pad)


def _shared_kernel(h_ref, w13_ref, w2_ref, sh_ref):
    hb = _unpack_bf16_pairs(h_ref[...]).astype(BF16)
    up = jnp.dot(hb, w13_ref[...], preferred_element_type=F32)
    hid = _silu(up[:, :D_SHARED]) * up[:, D_SHARED:]
    sh_ref[...] = jnp.dot(hid.astype(BF16), w2_ref[...], preferred_element_type=F32).astype(sh_ref.dtype)


def _shared_expert(hp, ws13, ws2):
    n = hp.shape[0]
    tm = FFN_TM
    return pl.pallas_call(
        _shared_kernel,
        grid=(n // tm,),
        in_specs=[pl.BlockSpec((tm, D_PACK), lambda i: (i, 0)),
                  pl.BlockSpec((D_MODEL, 2 * D_SHARED), lambda i: (0, 0)),
                  pl.BlockSpec((D_SHARED, D_MODEL), lambda i: (0, 0))],
        out_specs=pl.BlockSpec((tm, D_MODEL), lambda i: (i, 0)),
        out_shape=jax.ShapeDtypeStruct((n, D_MODEL), BF16),
        compiler_params=_cparams("parallel"),
    )(hp, ws13, ws2)


ROUTE_T = 512
PER_GROUP = N_EXPERTS // N_GROUPS
NEG_INF = float("-inf")


def _first_max(x, iota, n):
    m = jnp.max(x, axis=0, keepdims=True)
    first = jnp.min(jnp.where(x == m, iota, n), axis=0, keepdims=True)
    return m, iota == first


def _route_kernel(lg_ref, bias_ref, tri_ref, idx_ref, w_ref, rank_ref, cnt_ref, carry_ref):
    @pl.when(pl.program_id(0) == 0)
    def _():
        carry_ref[...] = jnp.zeros_like(carry_ref)

    t = lg_ref.shape[0]
    logits = jnp.transpose(lg_ref[...])[:N_EXPERTS]
    scores = 1.0 / (1.0 + jnp.exp(-logits))
    sel = scores + bias_ref[...]
    sub_g = lax.broadcasted_iota(jnp.int32, (PER_GROUP, t), 0)
    sub_e = lax.broadcasted_iota(jnp.int32, (N_EXPERTS, t), 0)
    grp_rows = []
    for g in range(N_GROUPS):
        x = sel[g * PER_GROUP:(g + 1) * PER_GROUP]
        m1, hit = _first_max(x, sub_g, PER_GROUP)
        m2 = jnp.max(jnp.where(hit, NEG_INF, x), axis=0, keepdims=True)
        grp_rows.append(m1 + m2)
    cur = jnp.concatenate(grp_rows, axis=0)
    sub_grp = lax.broadcasted_iota(jnp.int32, (N_GROUPS, t), 0)
    grp_on = jnp.zeros((N_GROUPS, t), F32)
    for _ in range(TOPK_GROUPS):
        _, hit = _first_max(cur, sub_grp, N_GROUPS)
        grp_on = jnp.where(hit, 1.0, grp_on)
        cur = jnp.where(hit, NEG_INF, cur)
    exp_on = jnp.concatenate([jnp.broadcast_to(grp_on[g:g + 1], (PER_GROUP, t)) for g in range(N_GROUPS)], axis=0)
    cur = jnp.where(exp_on > 0.0, sel, NEG_INF)
    hits, idx_rows = [], []
    for _ in range(TOP_K):
        _, hit = _first_max(cur, sub_e, N_EXPERTS)
        hits.append(hit)
        idx_rows.append(jnp.sum(jnp.where(hit, sub_e, 0), axis=0, keepdims=True))
        cur = jnp.where(hit, NEG_INF, cur)
    w_rows = [jnp.sum(jnp.where(hit, scores, 0.0), axis=0, keepdims=True) for hit in hits]
    total = functools.reduce(lambda a, b: a + b, w_rows)
    w_ref[...] = jnp.concatenate([w / total * ROUTED_SCALE for w in w_rows], axis=0)
    idx_ref[...] = jnp.concatenate(idx_rows, axis=0)
    chosen = functools.reduce(lambda a, b: a + b, [jnp.where(hit, 1.0, 0.0) for hit in hits])
    incl = jnp.dot(chosen.astype(BF16), tri_ref[...], preferred_element_type=F32)
    before = incl - chosen + carry_ref[:, :1]
    rank_ref[...] = jnp.concatenate(
        [jnp.sum(jnp.where(hit, before, 0.0), axis=0, keepdims=True) for hit in hits], axis=0).astype(jnp.int32)
    carry_ref[...] = carry_ref[...] + incl[:, t - 1:t]
    cnt_ref[...] = carry_ref[...]


def _route(logits, router_bias):
    n = logits.shape[0]
    t = ROUTE_T
    tri = (jnp.arange(t)[:, None] <= jnp.arange(t)[None, :]).astype(BF16)
    bias_b = jnp.broadcast_to(router_bias.astype(F32)[:, None], (N_EXPERTS, t))
    slot = pl.BlockSpec((TOP_K, t), lambda i: (0, i))
    idx, w, rank, cnt = pl.pallas_call(
        _route_kernel,
        grid=(n // t,),
        in_specs=[pl.BlockSpec((t, LANES), lambda i: (i, 0)),
                  pl.BlockSpec((N_EXPERTS, t), lambda i: (0, 0)),
                  pl.BlockSpec((t, t), lambda i: (0, 0))],
        out_specs=[slot, slot, slot, pl.BlockSpec((N_EXPERTS, LANES), lambda i: (0, 0))],
        out_shape=[jax.ShapeDtypeStruct((TOP_K, n), jnp.int32), jax.ShapeDtypeStruct((TOP_K, n), F32),
                   jax.ShapeDtypeStruct((TOP_K, n), jnp.int32), jax.ShapeDtypeStruct((N_EXPERTS, LANES), F32)],
        scratch_shapes=[pltpu.VMEM((N_EXPERTS, LANES), F32)],
        compiler_params=_cparams("arbitrary"),
    )(logits, bias_b, tri)
    return idx, w, rank, cnt[:, 0].astype(jnp.int32)


def _experts_kernel(be_ref, bv_ref, x_ref, w1_ref, w3_ref, w2_ref, o_ref, w1b, w3b, w2b):
    i = pl.program_id(0)
    changed = jnp.logical_or(i == 0, be_ref[i] != be_ref[jnp.maximum(i - 1, 0)])

    @pl.when(changed)
    def _():
        w1b[...] = w1_ref[0, 0].astype(BF16)
        w3b[...] = w3_ref[0, 0].astype(BF16)
        w2b[...] = w2_ref[0, 0].astype(BF16)

    @pl.when(bv_ref[i] > 0)
    def _():
        live = lax.broadcasted_iota(jnp.int32, (MOE_BLOCK, 1), 0) < bv_ref[i]
        xb = _unpack_bf16_pairs(jnp.where(live, x_ref[...], 0)).astype(BF16)
        hid = _silu(jnp.dot(xb, w1b[...], preferred_element_type=F32)) * \
            jnp.dot(xb, w3b[...], preferred_element_type=F32)
        o_ref[...] = _pack_bf16_pairs(jnp.dot(hid.astype(BF16), w2b[...], preferred_element_type=F32))

    @pl.when(bv_ref[i] == 0)
    def _():
        o_ref[...] = jnp.zeros_like(o_ref)


def _grouped_experts(xs, block_expert, block_valid, w1, w3, w2, layer):
    rows = xs.shape[0]
    n_blocks = rows // MOE_BLOCK
    grid_spec = pltpu.PrefetchScalarGridSpec(
        num_scalar_prefetch=2,
        grid=(n_blocks,),
        in_specs=[pl.BlockSpec((MOE_BLOCK, D_PACK), lambda i, be, bv: (i, 0)),
                  pl.BlockSpec((1, 1, D_MODEL, D_EXPERT), lambda i, be, bv: (layer, be[i], 0, 0)),
                  pl.BlockSpec((1, 1, D_MODEL, D_EXPERT), lambda i, be, bv: (layer, be[i], 0, 0)),
                  pl.BlockSpec((1, 1, D_EXPERT, D_MODEL), lambda i, be, bv: (layer, be[i], 0, 0))],
        out_specs=pl.BlockSpec((MOE_BLOCK, D_PACK), lambda i, be, bv: (i, 0)),
        scratch_shapes=[pltpu.VMEM((D_MODEL, D_EXPERT), BF16), pltpu.VMEM((D_MODEL, D_EXPERT), BF16),
                        pltpu.VMEM((D_EXPERT, D_MODEL), BF16)])
    return pl.pallas_call(
        _experts_kernel,
        grid_spec=grid_spec,
        out_shape=jax.ShapeDtypeStruct((rows, D_PACK), jnp.int32),
        compiler_params=_cparams("arbitrary"),
    )(block_expert, block_valid, xs, w1, w3, w2)


SC_CORES = 2
SC_SUBCORES = 16
SC_WORKERS = SC_CORES * SC_SUBCORES
SC_ROWS = 64


def _sc_mesh():
    return plsc.VectorSubcoreMesh(core_axis_name="c", subcore_axis_name="s",
                                  num_cores=SC_CORES, num_subcores=SC_SUBCORES)


def _sc_dispatch(table, dest, rows):
    n, d = table.shape
    kk = dest.shape[0]
    per_worker = n // SC_WORKERS
    n_chunks = per_worker // SC_ROWS
    assert per_worker * SC_WORKERS == n and n_chunks * SC_ROWS == per_worker
    idx = dest.reshape(kk, SC_WORKERS, n_chunks, SC_ROWS).transpose(1, 2, 0, 3).reshape(
        SC_WORKERS, n_chunks * kk, SC_ROWS)

    @functools.partial(
        pl.kernel, mesh=_sc_mesh(),
        out_type=jax.ShapeDtypeStruct((rows, d), table.dtype),
        scratch_types=[pltpu.VMEM((n_chunks * kk, SC_ROWS), jnp.int32),
                       pltpu.VMEM((SC_ROWS, d), table.dtype),
                       pltpu.SemaphoreType.DMA])
    def dispatch(table_hbm, idx_hbm, out_hbm, idx_v, rows_v, sem):
        wid = lax.axis_index("s") * SC_CORES + lax.axis_index("c")
        base = wid * per_worker
        pltpu.sync_copy(idx_hbm.at[wid], idx_v)

        @pl.loop(0, n_chunks)
        def _(j):
            pltpu.sync_copy(table_hbm.at[pl.ds(base + j * SC_ROWS, SC_ROWS)], rows_v)
            copies = [pltpu.async_copy(rows_v, out_hbm.at[idx_v.at[j * kk + k]], sem) for k in range(kk)]
            for cp in copies:
                cp.wait()

    return dispatch(table, idx)


def _sc_gather(table, idx):
    b = idx.shape[0]
    d = table.shape[1]
    per_worker = b // SC_WORKERS
    n_chunks = per_worker // SC_ROWS
    assert per_worker * SC_WORKERS == b and n_chunks * SC_ROWS == per_worker

    @functools.partial(
        pl.kernel, mesh=_sc_mesh(),
        out_type=jax.ShapeDtypeStruct((b, d), table.dtype),
        scratch_types=[pltpu.VMEM((n_chunks, SC_ROWS), jnp.int32),
                       pltpu.VMEM((SC_ROWS, d), table.dtype),
                       pltpu.SemaphoreType.DMA])
    def gather(table_hbm, idx_hbm, out_hbm, idx_v, rows_v, sem):
        wid = lax.axis_index("s") * SC_CORES + lax.axis_index("c")
        base = wid * per_worker
        pltpu.sync_copy(idx_hbm.at[wid], idx_v)

        @pl.loop(0, n_chunks)
        def _(j):
            pltpu.async_copy(table_hbm.at[idx_v.at[j]], rows_v, sem).wait()
            pltpu.sync_copy(rows_v, out_hbm.at[pl.ds(base + j * SC_ROWS, SC_ROWS)])

    return gather(table, idx.reshape(SC_WORKERS, n_chunks, SC_ROWS))


POST_TM = 512


def _ffn_post_kernel(x_ref, yg_ref, w_ref, s_ref, gate_ref, g_ref, b_ref, o_ref):
    w = w_ref[...]
    routed = _unpack_bf16_pairs(yg_ref[0]) * w[:, 0:1]
    for k in range(1, TOP_K):
        routed = routed + _unpack_bf16_pairs(yg_ref[k]) * w[:, k:k + 1]
    r = DEEPNORM_ALPHA * x_ref[...] + gate_ref[0] * (routed + s_ref[...])
    o_ref[...] = _layer_norm_rows(r, g_ref[...], b_ref[...])


def _ffn_post(x, yg, w_tok, shared, mod3, ln_g, ln_b):
    n = x.shape[0]
    tm = POST_TM
    row = lambda i: _mod_row_of_tile(i, tm)
    tile = pl.BlockSpec((tm, D_MODEL), lambda i: (i, 0))
    vec = pl.BlockSpec((1, D_MODEL), lambda i: (0, 0))
    return pl.pallas_call(
        _ffn_post_kernel,
        grid=(n // tm,),
        in_specs=[tile, pl.BlockSpec((TOP_K, tm, D_PACK), lambda i: (0, i, 0)),
                  pl.BlockSpec((tm, TOP_K), lambda i: (i, 0)), tile,
                  pl.BlockSpec((1, 1, D_MODEL), lambda i: (row(i), 0, 5)), vec, vec],
        out_specs=tile,
        out_shape=jax.ShapeDtypeStruct((n, D_MODEL), F32),
        compiler_params=_cparams("parallel"),
    )(x, yg, w_tok, shared, mod3, ln_g.reshape(1, D_MODEL), ln_b.reshape(1, D_MODEL))


def _moe_routed(h, idx, rank, counts, w1, w3, w2, layer):
    n = h.shape[0]
    padded = (counts + MOE_BLOCK - 1) // MOE_BLOCK * MOE_BLOCK
    pad_end = jnp.cumsum(padded)
    pad_start = pad_end - padded
    experts = jnp.arange(N_EXPERTS, dtype=jnp.int32)
    dest = jnp.sum(jnp.where(idx[:, :, None] == experts, pad_start, 0), axis=-1) + rank
    n_blocks = n * TOP_K // MOE_BLOCK + N_EXPERTS
    rows = n_blocks * MOE_BLOCK
    block_start = jnp.arange(n_blocks, dtype=jnp.int32) * MOE_BLOCK
    block_expert = jnp.minimum(jnp.sum(pad_end[None, :] <= block_start[:, None], axis=1), N_EXPERTS - 1).astype(jnp.int32)
    of_block = block_expert[:, None] == experts
    used = jnp.sum(jnp.where(of_block, counts, 0), axis=1) - (block_start - jnp.sum(jnp.where(of_block, pad_start, 0), axis=1))
    block_valid = jnp.clip(used, 0, MOE_BLOCK).astype(jnp.int32)
    xs = _sc_dispatch(h, dest, rows)
    ys = _grouped_experts(xs, block_expert, block_valid, w1, w3, w2, layer)
    return _sc_gather(ys, dest.reshape(-1)).reshape(TOP_K, n, D_PACK)


def _even_layer(x, mod3, j, w_in, w_out, q_norm, k_norm, log_decay, gn_g,
                cache_k, cache_v, state_ret):
    (p,) = _modulated_proj(x, mod3, 0, w_in.astype(BF16), [EVEN_IN], [F32])
    q, k = _attn_prep(p, q_norm, k_norm)
    v = p[:, A_Q + A_KV:A_Q + 2 * A_KV]
    k_p = k[:N_P].reshape(BATCH, SEQ, A_KV)
    v_p = v[:N_P].reshape(BATCH, SEQ, A_KV)
    o_p = _attention(q, k_p, v_p, 0, BATCH, SEQ)
    k_all = jnp.concatenate([k[N_P:].reshape(DEC_BATCH, DEC_SEQ, A_KV),
                             cache_k.reshape(DEC_BATCH, PAST_LEN, A_KV)], axis=1)
    v_all = jnp.concatenate([v[N_P:].reshape(DEC_BATCH, DEC_SEQ, A_KV),
                             cache_v.reshape(DEC_BATCH, PAST_LEN, A_KV)], axis=1)
    o_s = _attention(q, k_all, v_all, N_P // DEC_SEQ, DEC_BATCH, DEC_SEQ)
    o_attn = jnp.concatenate([o_p, o_s], axis=0)
    lg_rows = jnp.broadcast_to(log_decay.reshape(2 * H_B, 1), (2 * H_B, LANES))
    zeros_s = jnp.zeros((BATCH, 2, H_B, DK_B, DV_B), F32)
    r_p, s_p = _retention(p, 0, BATCH, SEQ, lg_rows, zeros_s, gn_g)
    r_s, _ = _retention(p, N_P // DEC_SEQ, DEC_BATCH, DEC_SEQ, lg_rows, state_ret, gn_g)
    o_ret = jnp.concatenate([r_p, r_s], axis=0)
    return ([o_attn, o_ret], w_out.astype(BF16), k_p.reshape(BATCH, SEQ, N_KV_A, HD_A),
            v_p.reshape(BATCH, SEQ, N_KV_A, HD_A), s_p)


def _chunk_cumsum(g, batch, t):
    gc = g.reshape(batch, t // GC, GC, 2, H_C)
    f = jnp.cumsum(gc[:, :, :, 0], axis=2)
    b = jnp.cumsum(gc[:, :, ::-1, 1], axis=2)[:, :, ::-1]
    return jnp.stack([f, b], axis=3)


def _odd_layer(x, mod3, j, w_in, conv_w, a_log, dt_bias, norm_g, w_out, state_gdn):
    w_main = w_in[:, :2 * C_QK + 2 * C_V].astype(BF16)
    w_ab = jnp.pad(w_in[:, 2 * C_QK + 2 * C_V:], ((0, 0), (0, LANES - 4 * H_C))).astype(BF16)
    w_cat = jnp.concatenate([w_main, w_ab], axis=1)
    pm, pab = _modulated_proj(x, mod3, 0, w_cat, [2 * C_QK + 2 * C_V, LANES], [F32, F32])
    qkv_p = _gdn_prep(pm, conv_w, 0, BATCH, SEQ)
    qkv_s = _gdn_prep(pm, conv_w, N_P // DEC_SEQ, DEC_BATCH, DEC_SEQ)
    ab = pab[:, :4 * H_C].reshape(N_TOK, 2, 2, H_C)
    beta = jax.nn.sigmoid(ab[:, 0])
    g = -jnp.exp(a_log.astype(F32)) * jax.nn.softplus(ab[:, 1] + dt_bias.astype(F32))

    gcs_p = _chunk_cumsum(g[:N_P], BATCH, SEQ)
    gcs_s = _chunk_cumsum(g[N_P:], DEC_BATCH, DEC_SEQ)
    gcol = jnp.concatenate([gcs_p.reshape(N_P, 2 * H_C), gcs_s.reshape(N_S, 2 * H_C)], axis=0)
    gbeta = jnp.concatenate([beta.reshape(N_TOK, 2 * H_C), gcol], axis=1)
    zeros_s = jnp.zeros((BATCH, 2, H_C, DK_C, DV_C), F32)
    o_p, s_p = _gdn(qkv_p, pm, gbeta, gcs_p.transpose(0, 3, 4, 1, 2), 0, BATCH, SEQ, zeros_s, norm_g)
    o_s, _ = _gdn(qkv_s, pm, gbeta, gcs_s.transpose(0, 3, 4, 1, 2), N_P // DEC_SEQ, DEC_BATCH, DEC_SEQ,
                  state_gdn, norm_g)
    return [jnp.concatenate([o_p, o_s], axis=0)], w_out.astype(BF16), s_p


def kernel(x_prompt, x_sample, cache_attn_k, cache_attn_v, state_ret, state_gdn, c, c_ctx, mod_w, mod_b, ln_g, ln_b, even_w_in, even_w_out, attn_q_norm, attn_k_norm, ret_log_decay, ret_norm_g, odd_w_in, gdn_conv_w, gdn_a_log, gdn_dt_bias, gdn_norm_g, odd_w_out, router_w, router_bias, expert_w1, expert_w3, expert_w2, shared_w1, shared_w3, shared_w2):
    x = jnp.concatenate([x_prompt.reshape(N_P, D_MODEL), x_sample.reshape(N_S, D_MODEL)], axis=0)
    cvec = jnp.concatenate([c_ctx[None, :], c, jnp.zeros((MOD_ROWS - N_MOD, D_MODEL), F32)], axis=0)
    mod_all = _mod_vectors(cvec, mod_w, mod_b)
    new_k, new_v, new_ret, new_gdn = [], [], [], []
    for l in range(DEPTH):
        j = l // 2
        mod3 = mod_all[l].reshape(MOD_ROWS, 1, 6 * D_MODEL)
        if l % 2 == 0:
            a_list, w_o, k_p, v_p, s_p = _even_layer(
                x, mod3, j, even_w_in[j], even_w_out[j], attn_q_norm[j], attn_k_norm[j],
                ret_log_decay[j], ret_norm_g[j], cache_attn_k[:, j], cache_attn_v[:, j], state_ret[:, j])
            new_k.append(k_p)
            new_v.append(v_p)
            new_ret.append(s_p)
        else:
            a_list, w_o, s_p = _odd_layer(
                x, mod3, j, odd_w_in[j], gdn_conv_w[j], gdn_a_log[j], gdn_dt_bias[j], gdn_norm_g[j],
                odd_w_out[j], state_gdn[:, j])
            new_gdn.append(s_p)
        x = _outproj_ln(a_list, w_o, x, mod3, 2, ln_g[l, 0], ln_b[l, 0])
        rw = jnp.pad(router_w[l], ((0, 0), (0, LANES - N_EXPERTS)))
        ws13 = jnp.concatenate([shared_w1[l], shared_w3[l]], axis=1).astype(BF16)
        h, logits = _ffn_pre(x, mod3, rw)
        idx, w, rank, counts = _route(logits, router_bias[l])
        yg = _moe_routed(h, idx, rank, counts, expert_w1, expert_w3, expert_w2, l)
        shared = _shared_expert(h, ws13, shared_w2[l].astype(BF16))
        x = _ffn_post(x, yg, w.T, shared, mod3, ln_g[l, 1], ln_b[l, 1])
    return (x[:N_P].reshape(BATCH, SEQ, D_MODEL), x[N_P:].reshape(DEC_BATCH, DEC_SEQ, D_MODEL),
            jnp.stack(new_k, axis=1), jnp.stack(new_v, axis=1),
            jnp.stack(new_ret, axis=1), jnp.stack(new_gdn, axis=1))
```

```python
import functools
import math

import jax
import jax.numpy as jnp
from jax import lax
from jax.experimental import pallas as pl
from jax.experimental.pallas import tpu as pltpu
from jax.experimental.pallas import tpu_sc as plsc

D_MODEL = 1024
BATCH = 16
SEQ = 256
DEPTH = 4
DEC_BATCH = 4
DEC_SEQ = 4096
PAST_LEN = 512
GRID_W = 64
N_HEADS_A = 8
N_KV_A = 2
HD_A = 64
ROPE_THETA = 10000.0
H_B = 4
DK_B = 64
DV_B = 128
RET_CHUNK = 128
H_C = 8
DK_C = 128
DV_C = 128
CONV_K = 5
GDN_CHUNK = 64
N_EXPERTS = 64
TOP_K = 8
N_GROUPS = 8
TOPK_GROUPS = 4
D_EXPERT = 256
D_SHARED = 256
ROUTED_SCALE = 2.5
MOE_BLOCK = 512
A_Q = N_HEADS_A * HD_A
A_KV = N_KV_A * HD_A
B_QK = H_B * DK_B
B_V = H_B * DV_B
EVEN_IN = A_Q + 2 * A_KV + 2 * B_QK + 2 * B_V
C_QK = H_C * DK_C
C_V = H_C * DV_C
DEEPNORM_ALPHA = (2 * DEPTH) ** 0.25
EPS = 1e-6

N_P = BATCH * SEQ
N_S = DEC_BATCH * DEC_SEQ
N_TOK = N_P + N_S
N_MOD = 1 + DEC_BATCH
MOD_ROWS = 8

LANES = 128
VMEM_LIMIT = 56 * 1024 * 1024

F32 = jnp.float32
BF16 = jnp.bfloat16


def _cparams(*sem):
    return pltpu.CompilerParams(dimension_semantics=sem, vmem_limit_bytes=VMEM_LIMIT)


def _bdot(a, b):
    return jnp.dot(a.astype(BF16), b.astype(BF16), preferred_element_type=F32)


def _bdot_t(a, b):
    return lax.dot_general(a.astype(BF16), b.astype(BF16), (((0,), (0,)), ((), ())),
                           preferred_element_type=F32)


def _bdot_nt(a, b):
    return lax.dot_general(a.astype(BF16), b.astype(BF16), (((1,), (1,)), ((), ())),
                           preferred_element_type=F32)


def _split3(a):
    hi = a.astype(BF16)
    r = a - hi.astype(F32)
    mid = r.astype(BF16)
    lo = (r - mid.astype(F32)).astype(BF16)
    return hi, mid, lo


def _dot_hp(a, b):
    a0, a1, a2 = _split3(a)
    b0, b1, b2 = _split3(b)
    d = lambda x, y: jnp.dot(x, y, preferred_element_type=F32)
    small = d(a0, b2) + d(a2, b0) + d(a1, b1)
    return (d(a0, b1) + d(a1, b0)) + small + d(a0, b0)


def _silu(x):
    return x * (1.0 / (1.0 + jnp.exp(-x)))


HI16 = 0xFFFF0000
D_PACK = D_MODEL // 2


def _pack_bf16_pairs(x):
    c = x.shape[1] // 2
    lo = pltpu.bitcast(x[:, :c].astype(BF16).astype(F32), jnp.uint32) >> 16
    hi = pltpu.bitcast(x[:, c:].astype(BF16).astype(F32), jnp.uint32) & jnp.uint32(HI16)
    return pltpu.bitcast(lo | hi, jnp.int32)


def _unpack_bf16_pairs(p):
    u = pltpu.bitcast(p, jnp.uint32)
    return jnp.concatenate([pltpu.bitcast(u << 16, F32), pltpu.bitcast(u & jnp.uint32(HI16), F32)], axis=1)


def _mod_row_of_tile(i, tile):
    tiles_p = N_P // tile
    tiles_per_b = DEC_SEQ // tile
    return jnp.where(i < tiles_p, 0, 1 + (i - tiles_p) // tiles_per_b)


MOD_TN = 1536


def _mod_kernel(c_ref, w_ref, b_ref, o_ref):
    a = _silu(c_ref[...])
    o_ref[0] = _bdot(a, w_ref[0]) + b_ref[0]


def _mod_vectors(cvec, mod_w, mod_b):
    n6 = 6 * D_MODEL
    return pl.pallas_call(
        _mod_kernel,
        grid=(DEPTH, n6 // MOD_TN),
        in_specs=[pl.BlockSpec((MOD_ROWS, D_MODEL), lambda l, j: (0, 0)),
                  pl.BlockSpec((1, D_MODEL, MOD_TN), lambda l, j: (l, 0, j)),
                  pl.BlockSpec((1, 1, MOD_TN), lambda l, j: (l, 0, j))],
        out_specs=pl.BlockSpec((1, MOD_ROWS, MOD_TN), lambda l, j: (l, 0, j)),
        out_shape=jax.ShapeDtypeStruct((DEPTH, MOD_ROWS, n6), F32),
        compiler_params=_cparams("parallel", "parallel"),
    )(cvec, mod_w, mod_b.reshape(DEPTH, 1, n6))


PROJ_TM = 512


def _proj_kernel(x_ref, shift_ref, scale_ref, w_ref, *o_refs, widths):
    h = (x_ref[...] * (1.0 + scale_ref[0]) + shift_ref[0]).astype(BF16)
    off = 0
    for o_ref, wd in zip(o_refs, widths):
        o_ref[...] = jnp.dot(h, w_ref[:, off:off + wd], preferred_element_type=F32).astype(o_ref.dtype)
        off += wd


def _modulated_proj(x, mod3, shift_blk, w_bf16, widths, dtypes):
    n = x.shape[0]
    tm = PROJ_TM
    row = lambda i: _mod_row_of_tile(i, tm)
    return pl.pallas_call(
        functools.partial(_proj_kernel, widths=tuple(widths)),
        grid=(n // tm,),
        in_specs=[pl.BlockSpec((tm, D_MODEL), lambda i: (i, 0)),
                  pl.BlockSpec((1, 1, D_MODEL), lambda i: (row(i), 0, shift_blk)),
                  pl.BlockSpec((1, 1, D_MODEL), lambda i: (row(i), 0, shift_blk + 1)),
                  pl.BlockSpec((D_MODEL, sum(widths)), lambda i: (0, 0))],
        out_specs=[pl.BlockSpec((tm, wd), lambda i: (i, 0)) for wd in widths],
        out_shape=[jax.ShapeDtypeStruct((n, wd), dt) for wd, dt in zip(widths, dtypes)],
        compiler_params=_cparams("parallel"),
    )(x, mod3, mod3, w_bf16)


OUT_TM = 512


def _layer_norm_rows(r, g, b):
    mu = jnp.mean(r, axis=-1, keepdims=True)
    d = r - mu
    var = jnp.mean(d * d, axis=-1, keepdims=True)
    return d * lax.rsqrt(var + EPS) * g + b


def _outproj_kernel(*refs, n_a):
    a_refs = refs[:n_a]
    w_ref, x_ref, gate_ref, g_ref, b_ref, o_ref = refs[n_a:]
    off = 0
    acc = None
    for a_ref in a_refs:
        wd = a_ref.shape[1]
        part = jnp.dot(a_ref[...].astype(BF16), w_ref[off:off + wd, :], preferred_element_type=F32)
        acc = part if acc is None else acc + part
        off += wd
    r = DEEPNORM_ALPHA * x_ref[...] + gate_ref[0] * acc
    o_ref[...] = _layer_norm_rows(r, g_ref[...], b_ref[...])


def _outproj_ln(a_list, w_bf16, x, mod3, gate_blk, ln_g, ln_b):
    n = x.shape[0]
    tm = OUT_TM
    row = lambda i: _mod_row_of_tile(i, tm)
    kdim = w_bf16.shape[0]
    return pl.pallas_call(
        functools.partial(_outproj_kernel, n_a=len(a_list)),
        grid=(n // tm,),
        in_specs=[pl.BlockSpec((tm, a.shape[1]), lambda i: (i, 0)) for a in a_list] + [
            pl.BlockSpec((kdim, D_MODEL), lambda i: (0, 0)),
            pl.BlockSpec((tm, D_MODEL), lambda i: (i, 0)),
            pl.BlockSpec((1, 1, D_MODEL), lambda i: (row(i), 0, gate_blk)),
            pl.BlockSpec((1, D_MODEL), lambda i: (0, 0)),
            pl.BlockSpec((1, D_MODEL), lambda i: (0, 0))],
        out_specs=pl.BlockSpec((tm, D_MODEL), lambda i: (i, 0)),
        out_shape=jax.ShapeDtypeStruct((n, D_MODEL), F32),
        compiler_params=_cparams("parallel"),
    )(*a_list, w_bf16, x, mod3, ln_g.reshape(1, D_MODEL), ln_b.reshape(1, D_MODEL))


PREP_TM = 512
ROPE_SEG = HD_A // 4


def _head_rms(x, gain, ones_bd):
    parts = _split3(x * x)
    ss = functools.reduce(lambda a, b: a + b,
                          [jnp.dot(p, ones_bd, preferred_element_type=F32) for p in reversed(parts)])
    return x * lax.rsqrt(ss * (1.0 / HD_A) + EPS) * gain


def _rope_lanes(x, cos, sin):
    lane = lax.broadcasted_iota(jnp.int32, (1, LANES), 1)
    first = (lane % (2 * ROPE_SEG)) < ROPE_SEG
    cols = []
    for b in range(x.shape[1] // LANES):
        xb = x[:, b * LANES:(b + 1) * LANES]
        partner = jnp.where(first, pltpu.roll(xb, LANES - ROPE_SEG, axis=1), pltpu.roll(xb, ROPE_SEG, axis=1))
        cols.append(xb * cos + partner * sin)
    return cols[0] if len(cols) == 1 else jnp.concatenate(cols, axis=1)


def _attn_prep_kernel(q_ref, kv_ref, cos_ref, sin_ref, qg_ref, kg_ref, bd_ref, qo_ref, ko_ref):
    cos = cos_ref[...]
    sin = sin_ref[...]
    q = _rope_lanes(_head_rms(q_ref[...], qg_ref[...], bd_ref[...]), cos, sin)
    qo_ref[...] = (q * (HD_A ** -0.5)).astype(BF16)
    k = _head_rms(kv_ref[:, :A_KV], kg_ref[...], bd_ref[:A_KV, :A_KV])
    ko_ref[...] = _rope_lanes(k, cos, sin)


def _rope_tables():
    rows = DEC_SEQ // GRID_W
    row = jnp.repeat(jnp.arange(rows), GRID_W).astype(F32)
    colp = jnp.tile(jnp.arange(GRID_W), rows).astype(F32)
    inv_freq = ROPE_THETA ** (-jnp.arange(ROPE_SEG, dtype=F32) / ROPE_SEG)
    ar = row[:, None] * inv_freq[None, :]
    ac = colp[:, None] * inv_freq[None, :]
    cos = jnp.concatenate([jnp.cos(ar), jnp.cos(ar), jnp.cos(ac), jnp.cos(ac)], axis=-1)
    sin = jnp.concatenate([-jnp.sin(ar), jnp.sin(ar), -jnp.sin(ac), jnp.sin(ac)], axis=-1)
    cos = jnp.concatenate([jnp.ones((PREP_TM, HD_A), F32), cos], axis=0)
    sin = jnp.concatenate([jnp.zeros((PREP_TM, HD_A), F32), sin], axis=0)
    return jnp.tile(cos, (1, LANES // HD_A)), jnp.tile(sin, (1, LANES // HD_A))


def _attn_prep(p, q_norm, k_norm):
    tm = PREP_TM
    tiles_p = N_P // tm
    tiles_seq = DEC_SEQ // tm
    tab = lambda i: (jnp.where(i < tiles_p, 0, 1 + (i - tiles_p) % tiles_seq), 0)
    cos, sin = _rope_tables()
    head_id = jnp.arange(A_Q) // HD_A
    ones_bd = (head_id[:, None] == head_id[None, :]).astype(BF16)
    return pl.pallas_call(
        _attn_prep_kernel,
        grid=(N_TOK // tm,),
        in_specs=[pl.BlockSpec((tm, A_Q), lambda i: (i, 0)),
                  pl.BlockSpec((tm, 2 * A_KV), lambda i: (i, A_Q // (2 * A_KV))),
                  pl.BlockSpec((tm, LANES), tab), pl.BlockSpec((tm, LANES), tab),
                  pl.BlockSpec((1, A_Q), lambda i: (0, 0)), pl.BlockSpec((1, A_KV), lambda i: (0, 0)),
                  pl.BlockSpec((A_Q, A_Q), lambda i: (0, 0))],
        out_specs=[pl.BlockSpec((tm, A_Q), lambda i: (i, 0)), pl.BlockSpec((tm, A_KV), lambda i: (i, 0))],
        out_shape=[jax.ShapeDtypeStruct((N_TOK, A_Q), BF16), jax.ShapeDtypeStruct((N_TOK, A_KV), F32)],
        compiler_params=_cparams("parallel"),
    )(p, p, cos, sin, jnp.tile(q_norm, N_HEADS_A).reshape(1, A_Q), jnp.tile(k_norm, N_KV_A).reshape(1, A_KV),
      ones_bd)


ATT_TQ = 512
GROUP_A = N_HEADS_A // N_KV_A


def _attn_kernel(q_ref, kt_ref, kts_ref, v_ref, vs_ref, o_ref):
    low = lax.broadcasted_iota(jnp.int32, (1, LANES), 1) < HD_A
    for j in range(N_HEADS_A // 2):
        qb = q_ref[:, j * LANES:(j + 1) * LANES]
        outs = []
        for half in range(2):
            kv_head = (2 * j + half) // GROUP_A
            qh = jnp.where(low if half == 0 else jnp.logical_not(low), qb, jnp.zeros_like(qb))
            kt, v = (kt_ref[0], v_ref[0]) if half == kv_head else (kts_ref[0], vs_ref[0])
            s = jnp.dot(qh, kt, preferred_element_type=F32)
            m = jnp.max(s, axis=-1, keepdims=True)
            p = jnp.exp(s - m)
            l = jnp.sum(p, axis=-1, keepdims=True)
            pv = jnp.dot(p.astype(BF16), v, preferred_element_type=F32)
            outs.append(pv * (1.0 / l))
        o_ref[:, j * LANES:(j + 1) * LANES] = jnp.where(low, outs[0], outs[1]).astype(o_ref.dtype)


def _attention(q, k, v, row_blk0, batch, t):
    tk = v.shape[1]
    tq = min(ATT_TQ, t)
    nq = t // tq
    swap = lambda x: jnp.roll(x, HD_A, axis=-1)
    kt = jnp.swapaxes(k, 1, 2).astype(BF16)
    kts = jnp.swapaxes(swap(k), 1, 2).astype(BF16)
    kspec = pl.BlockSpec((1, LANES, tk), lambda b, i: (b, 0, 0))
    vspec = pl.BlockSpec((1, tk, LANES), lambda b, i: (b, 0, 0))
    return pl.pallas_call(
        _attn_kernel,
        grid=(batch, nq),
        in_specs=[pl.BlockSpec((tq, A_Q), lambda b, i: ((row_blk0 + b) * nq + i, 0)), kspec, kspec, vspec, vspec],
        out_specs=pl.BlockSpec((tq, A_Q), lambda b, i: (b * nq + i, 0)),
        out_shape=jax.ShapeDtypeStruct((batch * t, A_Q), BF16),
        compiler_params=_cparams("parallel", "parallel"),
    )(q, kt, kts, v.astype(BF16), swap(v).astype(BF16))


RC = RET_CHUNK


def _ret_kernel(q_ref, k_ref, v_ref, g_ref, lg_ref, s0_ref, gn_ref, o_ref, s_out_ref,
                ob_ref, st_ref, *, t):
    nc = t // RC
    hp = pl.program_id(1)
    ii = lax.broadcasted_iota(jnp.int32, (RC, RC), 0).astype(F32)
    jj = lax.broadcasted_iota(jnp.int32, (RC, RC), 1).astype(F32)
    col_i = lax.broadcasted_iota(jnp.int32, (RC, 1), 0).astype(F32)
    lane = lax.broadcasted_iota(jnp.int32, (1, LANES), 1)
    masks = [(lane >= hh * DK_B) & (lane < (hh + 1) * DK_B) for hh in range(2)]

    consts = []
    for d in range(2):
        for hh in range(2):
            lg = lg_ref[pl.ds(d * H_B + hp * 2 + hh, 1), :][:, :1]
            if d == 0:
                diff = ii - jj
                qe, ke = col_i + 1.0, (RC - 1.0) - col_i
            else:
                diff = jj - ii
                qe, ke = RC - col_i, col_i
            intra = jnp.where(diff >= 0, jnp.exp(jnp.maximum(diff, 0.0) * lg), 0.0)
            consts.append((intra, jnp.exp(qe * lg), jnp.exp(ke * lg), jnp.exp(RC * lg)))
            s0 = s0_ref[0, d, hh]
            z = jnp.zeros((DK_B, DV_B), F32)
            st_ref[d * 2 + hh] = jnp.concatenate([s0, z] if hh == 0 else [z, s0], axis=0)

    def body(c, carry):
        combos = [(d, hh) for d in range(2) for hh in range(2)]
        rows = [pl.ds(pl.multiple_of(cc * RC, RC), RC) for cc in (c, nc - 1 - c)]
        qcs = [q_ref[rows[d], :] for d in range(2)]
        kcs = [k_ref[rows[d], :] * (DK_B ** -0.5) for d in range(2)]
        qhs = [jnp.where(masks[hh], qcs[d], 0.0) for d, hh in combos]
        khs = [jnp.where(masks[hh], kcs[d], 0.0) for d, hh in combos]
        vcs = [v_ref[rows[d], hh * DV_B:(hh + 1) * DV_B].astype(BF16) for d, hh in combos]
        sts = [st_ref[d * 2 + hh] for d, hh in combos]
        scores = [_bdot_nt(qhs[i], khs[i]) * consts[i][0] for i in range(4)]
        kvs = [_bdot_t(khs[i] * consts[i][2], vcs[i]) for i in range(4)]
        outs = [jnp.dot(jnp.concatenate([scores[i], qhs[i] * consts[i][1]], axis=1).astype(BF16),
                        jnp.concatenate([vcs[i], sts[i].astype(BF16)], axis=0),
                        preferred_element_type=F32) for i in range(4)]
        for i, (d, hh) in enumerate(combos):
            st_ref[d * 2 + hh] = sts[i] * consts[i][3] + kvs[i]
        o_ref[rows[0], :] = jnp.concatenate(outs[0:2], axis=1)
        ob_ref[rows[1], :] = jnp.concatenate(outs[2:4], axis=1)
        return carry

    lax.fori_loop(0, nc, body, 0)

    for d in range(2):
        for hh in range(2):
            s_out_ref[0, d, hh] = st_ref[d * 2 + hh][hh * DK_B:(hh + 1) * DK_B, :]

    def finish(c, carry):
        r0 = pl.multiple_of(c * RC, RC)
        o = o_ref[pl.ds(r0, RC), :] + ob_ref[pl.ds(r0, RC), :]
        gate = _silu(g_ref[pl.ds(r0, RC), :])
        ys = []
        for hh in range(2):
            oh = o[:, hh * DV_B:(hh + 1) * DV_B]
            mu = jnp.mean(oh, axis=-1, keepdims=True)
            dlt = oh - mu
            var = jnp.mean(dlt * dlt, axis=-1, keepdims=True)
            ys.append(dlt * lax.rsqrt(var + EPS))
        o_ref[pl.ds(r0, RC), :] = jnp.concatenate(ys, axis=1) * gn_ref[...] * gate
        return carry

    lax.fori_loop(0, nc, finish, 0)


def _retention(p, row_blk0, batch, t, lg_rows, s0, gn_g):
    qb, kb, vb, gb = (A_Q + 2 * A_KV) // LANES, (A_Q + 2 * A_KV + B_QK) // LANES, \
        (A_Q + 2 * A_KV + 2 * B_QK) // (2 * DV_B), (A_Q + 2 * A_KV + 2 * B_QK + B_V) // (2 * DV_B)
    return pl.pallas_call(
        functools.partial(_ret_kernel, t=t),
        grid=(batch, H_B // 2),
        in_specs=[pl.BlockSpec((t, LANES), lambda b, h: (row_blk0 + b, qb + h)),
                  pl.BlockSpec((t, LANES), lambda b, h: (row_blk0 + b, kb + h)),
                  pl.BlockSpec((t, 2 * DV_B), lambda b, h: (row_blk0 + b, vb + h)),
                  pl.BlockSpec((t, 2 * DV_B), lambda b, h: (row_blk0 + b, gb + h)),
                  pl.BlockSpec((2 * H_B, LANES), lambda b, h: (0, 0)),
                  pl.BlockSpec((1, 2, 2, DK_B, DV_B), lambda b, h: (b, 0, h, 0, 0)),
                  pl.BlockSpec((1, 2 * DV_B), lambda b, h: (0, h))],
        out_specs=[pl.BlockSpec((t, 2 * DV_B), lambda b, h: (b, h)),
                   pl.BlockSpec((1, 2, 2, DK_B, DV_B), lambda b, h: (b, 0, h, 0, 0))],
        out_shape=[jax.ShapeDtypeStruct((batch * t, B_V), F32),
                   jax.ShapeDtypeStruct((batch, 2, H_B, DK_B, DV_B), F32)],
        scratch_shapes=[pltpu.VMEM((t, 2 * DV_B), F32), pltpu.VMEM((4, LANES, DV_B), F32)],
        compiler_params=_cparams("parallel", "parallel"),
    )(p, p, p, p, lg_rows, s0, gn_g.reshape(1, B_V))


GC = 256


def _unit_tri_inverses(mats):
    ii = lax.broadcasted_iota(jnp.int32, (GC, GC), 0)
    jj = lax.broadcasted_iota(jnp.int32, (GC, GC), 1)
    block_dist = ii ^ jj
    eye = (ii == jj).astype(F32)
    ms = [jnp.where((block_dist >> 3) == 0, -a, 0.0) for a in mats]
    invs = [eye + m for m in ms]
    for _ in range(2):
        ms = [_bdot(m, m) for m in ms]
        invs = [inv + _bdot(inv, m) for inv, m in zip(invs, ms)]
    for shift in range(3, int(math.log2(GC))):
        ls = [jnp.where((block_dist >> shift) == 1, a, 0.0) for a in mats]
        ts = [_bdot(inv, l) for inv, l in zip(invs, ls)]
        invs = [inv - _bdot(t, inv) for inv, t in zip(invs, ts)]
    return invs


def _gdn_kernel(q_ref, k_ref, v_ref, z_ref, gb_ref, gr_ref, s0_ref, ng_ref, o_ref, s_out_ref,
                ob_ref, st_ref, wq_s, u_s, a_s, kg_s, *, t, seqs):
    nc_seq = t // GC
    nc = seqs * nc_seq
    h = pl.program_id(1)
    ii = lax.broadcasted_iota(jnp.int32, (GC, GC), 0)
    jj = lax.broadcasted_iota(jnp.int32, (GC, GC), 1)
    incl = [ii >= jj, jj >= ii]
    strict = [ii > jj, jj > ii]
    lane32 = lax.broadcasted_iota(jnp.int32, (1, 4 * H_C), 1)
    st_ref[...] = jnp.zeros_like(st_ref)

    def col(x, idx):
        return jnp.sum(jnp.where(lane32 == idx, x, 0.0), axis=1, keepdims=True)

    def gate_last(c, d):
        grow = gr_ref[0, d, 0, pl.ds(c, 1), :]
        return grow, (grow[:, GC - 1:GC] if d == 0 else grow[:, 0:1])

    def prep(cb, carry):
        mats, rhs, dst = [], [], []
        for ci in range(prep_chunks):
            c = cb * prep_chunks + ci
            rows = pl.ds(pl.multiple_of(c * GC, GC), GC)
            qc = q_ref[rows, :] * (DK_C ** -0.5)
            kc = k_ref[rows, :]
            vc = v_ref[rows, :]
            gb = gb_ref[rows, :]
            kk = _bdot_nt(kc, kc)
            qk = _bdot_nt(qc, kc)
            r2 = pl.multiple_of(c * 2 * GC, 2 * GC)
            for d in range(2):
                beta = col(gb, d * H_C + h)
                gcol = col(gb, 2 * H_C + d * H_C + h)
                grow, glast = gate_last(c, d)
                decay = jnp.exp(jnp.where(incl[d], gcol - grow, -jnp.inf))
                eg = jnp.exp(gcol)
                mats.append(jnp.where(strict[d], kk * beta * decay, 0.0))
                rhs.append(jnp.concatenate([kc * (beta * eg), vc * beta], axis=1).astype(BF16))
                dst.append((d, rows, r2))
                wq_s[d, pl.ds(r2 + GC, GC), :] = (qc * eg).astype(BF16)
                a_s[d, rows, :] = (qk * decay).astype(BF16)
                kg_s[d, rows, :] = (kc * jnp.exp(glast - gcol)).astype(BF16)
        wus = [jnp.dot(tinv.astype(BF16), r, preferred_element_type=F32)
               for tinv, r in zip(_unit_tri_inverses(mats), rhs)]
        for wu, (d, rows, r2) in zip(wus, dst):
            wq_s[d, pl.ds(r2, GC), :] = wu[:, :DK_C].astype(BF16)
            u_s[d, rows, :] = wu[:, DK_C:]
        return carry

    prep_chunks = 2 if nc % 2 == 0 else 1
    lax.fori_loop(0, nc // prep_chunks, prep, 0)

    def step(c, carry):
        ccs = [c, nc - 1 - c]
        rows = [pl.ds(pl.multiple_of(cc * GC, GC), GC) for cc in ccs]
        sqs = [cc // nc_seq for cc in ccs]
        first = [ccs[0] % nc_seq == 0, ccs[1] % nc_seq == nc_seq - 1]
        ss = [jnp.where(first[d], s0_ref[sqs[d], d, 0], st_ref[d]) for d in range(2)]
        sbs = [s.astype(BF16) for s in ss]
        wss = [jnp.dot(wq_s[d, pl.ds(pl.multiple_of(ccs[d] * 2 * GC, 2 * GC), 2 * GC), :], sbs[d],
                       preferred_element_type=F32) for d in range(2)]
        vnbs = [(u_s[d, rows[d], :] - wss[d][:GC]).astype(BF16) for d in range(2)]
        os_ = [wss[d][GC:] + jnp.dot(a_s[d, rows[d], :], vnbs[d], preferred_element_type=F32) for d in range(2)]
        kvs = [_bdot_t(kg_s[d, rows[d], :], vnbs[d]) for d in range(2)]
        for d in range(2):
            _, glast = gate_last(ccs[d], d)
            s_new = ss[d] * jnp.exp(glast) + kvs[d]
            st_ref[d] = s_new
            s_out_ref[sqs[d], d, 0] = s_new
        o_ref[rows[0], :] = os_[0]
        ob_ref[rows[1], :] = os_[1]
        return carry

    lax.fori_loop(0, nc, step, 0)

    def finish(c, carry):
        r0 = pl.multiple_of(c * GC, GC)
        o = o_ref[pl.ds(r0, GC), :] + ob_ref[pl.ds(r0, GC), :]
        y = o * lax.rsqrt(jnp.mean(o * o, axis=-1, keepdims=True) + EPS) * ng_ref[...]
        o_ref[pl.ds(r0, GC), :] = y * _silu(z_ref[pl.ds(r0, GC), :])
        return carry

    lax.fori_loop(0, nc, finish, 0)


GDN_PREP_BLOCK_ELEMS = 512 * 1024


def _gdn_prep_kernel(x_ref, w_ref, o_ref, *, t, heads):
    x = x_ref[...]
    row = lax.broadcasted_iota(jnp.int32, (t, 1), 0)
    centre = (CONV_K - 1) // 2
    y = x * w_ref[centre:centre + 1, :]
    for i in range(CONV_K):
        s = i - centre
        if s == 0:
            continue
        shifted = pltpu.roll(x, (-s) % t, axis=0)
        inside = jnp.logical_and(row + s >= 0, row + s < t)
        y = y + jnp.where(inside, shifted, 0.0) * w_ref[i:i + 1, :]
    y = _silu(y)
    for b in range(heads):
        yb = y[:, b * LANES:(b + 1) * LANES]
        inv_norm = lax.rsqrt(jnp.sum(yb * yb, axis=-1, keepdims=True) + EPS)
        is_qk = pl.program_id(1) * heads + b < 2 * H_C
        o_ref[:, b * LANES:(b + 1) * LANES] = yb * jnp.where(is_qk, inv_norm, 1.0)


def _gdn_prep(pm, conv_w, row_blk0, batch, t):
    heads = max(1, min(H_C, GDN_PREP_BLOCK_ELEMS // (t * LANES)))
    n_col = (2 * C_QK + C_V) // (heads * LANES)
    return pl.pallas_call(
        functools.partial(_gdn_prep_kernel, t=t, heads=heads),
        grid=(batch, n_col),
        in_specs=[pl.BlockSpec((t, heads * LANES), lambda b, j: (row_blk0 + b, j)),
                  pl.BlockSpec((CONV_K, heads * LANES), lambda b, j: (0, j))],
        out_specs=pl.BlockSpec((t, heads * LANES), lambda b, j: (b, j)),
        out_shape=jax.ShapeDtypeStruct((batch * t, 2 * C_QK + C_V), F32),
        compiler_params=_cparams("parallel", "parallel"),
    )(pm, conv_w)


def _gdn(qkv, pm, gbeta, grow, row_blk0, batch, t, s0, norm_g):
    nc = t // GC
    seqs = 2 if (nc == 1 and batch % 2 == 0 and row_blk0 % 2 == 0) else 1
    tb = seqs * t
    groups = batch // seqs
    blk0 = row_blk0 // seqs
    grow = grow.reshape(groups, seqs, 2, H_C, nc, GC).transpose(0, 2, 3, 1, 4, 5).reshape(groups, 2, H_C, seqs * nc, GC)
    return pl.pallas_call(
        functools.partial(_gdn_kernel, t=t, seqs=seqs),
        grid=(groups, H_C),
        in_specs=[pl.BlockSpec((tb, DK_C), lambda b, h: (b, h)),
                  pl.BlockSpec((tb, DK_C), lambda b, h: (b, H_C + h)),
                  pl.BlockSpec((tb, DV_C), lambda b, h: (b, 2 * H_C + h)),
                  pl.BlockSpec((tb, DV_C), lambda b, h: (blk0 + b, 3 * H_C + h)),
                  pl.BlockSpec((tb, 4 * H_C), lambda b, h: (blk0 + b, 0)),
                  pl.BlockSpec((1, 2, 1, seqs * nc, GC), lambda b, h: (b, 0, h, 0, 0)),
                  pl.BlockSpec((seqs, 2, 1, DK_C, DV_C), lambda b, h: (b, 0, h, 0, 0)),
                  pl.BlockSpec((1, DV_C), lambda b, h: (0, 0))],
        out_specs=[pl.BlockSpec((tb, DV_C), lambda b, h: (b, h)),
                   pl.BlockSpec((seqs, 2, 1, DK_C, DV_C), lambda b, h: (b, 0, h, 0, 0))],
        out_shape=[jax.ShapeDtypeStruct((batch * t, C_V), F32),
                   jax.ShapeDtypeStruct((batch, 2, H_C, DK_C, DV_C), F32)],
        scratch_shapes=[pltpu.VMEM((tb, DV_C), F32), pltpu.VMEM((2, DK_C, DV_C), F32),
                        pltpu.VMEM((2, 2 * tb, DK_C), BF16), pltpu.VMEM((2, tb, DV_C), F32),
                        pltpu.VMEM((2, tb, GC), BF16), pltpu.VMEM((2, tb, DK_C), BF16)],
        compiler_params=_cparams("parallel", "parallel"),
    )(qkv, qkv, qkv, pm, gbeta, grow, s0, norm_g.reshape(1, DV_C))


FFN_TM = 512


def _ffn_pre_kernel(x_ref, shift_ref, scale_ref, rw_ref, h_ref, logit_ref):
    h = x_ref[...] * (1.0 + scale_ref[0]) + shift_ref[0]
    h_ref[...] = _pack_bf16_pairs(h)
    logit_ref[...] = _dot_hp(h, rw_ref[...])


def _ffn_pre(x, mod3, router_w_pad):
    n = x.shape[0]
    tm = FFN_TM
    row = lambda i: _mod_row_of_tile(i, tm)
    return pl.pallas_call(
        _ffn_pre_kernel,
        grid=(n // tm,),
        in_specs=[pl.BlockSpec((tm, D_MODEL), lambda i: (i, 0)),
                  pl.BlockSpec((1, 1, D_MODEL), lambda i: (row(i), 0, 3)),
                  pl.BlockSpec((1, 1, D_MODEL), lambda i: (row(i), 0, 4)),
                  pl.BlockSpec((D_MODEL, LANES), lambda i: (0, 0))],
        out_specs=[pl.BlockSpec((tm, D_PACK), lambda i: (i, 0)),
                   pl.BlockSpec((tm, LANES), lambda i: (i, 0))],
        out_shape=[jax.ShapeDtypeStruct((n, D_PACK), jnp.int32),
                   jax.ShapeDtypeStruct((n, LANES), F32)],
        compiler_params=_cparams("parallel"),
    )(x, mod3, mod3, router_w_pad)


def _shared_kernel(h_ref, w13_ref, w2_ref, sh_ref):
    hb = _unpack_bf16_pairs(h_ref[...]).astype(BF16)
    up = jnp.dot(hb, w13_ref[...], preferred_element_type=F32)
    hid = _silu(up[:, :D_SHARED]) * up[:, D_SHARED:]
    sh_ref[...] = jnp.dot(hid.astype(BF16), w2_ref[...], preferred_element_type=F32).astype(sh_ref.dtype)


def _shared_expert(hp, ws13, ws2):
    n = hp.shape[0]
    tm = FFN_TM
    return pl.pallas_call(
        _shared_kernel,
        grid=(n // tm,),
        in_specs=[pl.BlockSpec((tm, D_PACK), lambda i: (i, 0)),
                  pl.BlockSpec((D_MODEL, 2 * D_SHARED), lambda i: (0, 0)),
                  pl.BlockSpec((D_SHARED, D_MODEL), lambda i: (0, 0))],
        out_specs=pl.BlockSpec((tm, D_MODEL), lambda i: (i, 0)),
        out_shape=jax.ShapeDtypeStruct((n, D_MODEL), BF16),
        compiler_params=_cparams("parallel"),
    )(hp, ws13, ws2)


ROUTE_T = 512
PER_GROUP = N_EXPERTS // N_GROUPS
NEG_INF = float("-inf")


def _first_max(x, iota, n):
    m = jnp.max(x, axis=0, keepdims=True)
    first = jnp.min(jnp.where(x == m, iota, n), axis=0, keepdims=True)
    return m, iota == first


def _route_kernel(lg_ref, bias_ref, tri_ref, idx_ref, w_ref, rank_ref, cnt_ref, carry_ref):
    @pl.when(pl.program_id(0) == 0)
    def _():
        carry_ref[...] = jnp.zeros_like(carry_ref)

    t = lg_ref.shape[0]
    logits = jnp.transpose(lg_ref[...])[:N_EXPERTS]
    scores = 1.0 / (1.0 + jnp.exp(-logits))
    sel = scores + bias_ref[...]
    sub_g = lax.broadcasted_iota(jnp.int32, (PER_GROUP, t), 0)
    sub_e = lax.broadcasted_iota(jnp.int32, (N_EXPERTS, t), 0)
    grp_rows = []
    for g in range(N_GROUPS):
        x = sel[g * PER_GROUP:(g + 1) * PER_GROUP]
        m1, hit = _first_max(x, sub_g, PER_GROUP)
        m2 = jnp.max(jnp.where(hit, NEG_INF, x), axis=0, keepdims=True)
        grp_rows.append(m1 + m2)
    cur = jnp.concatenate(grp_rows, axis=0)
    sub_grp = lax.broadcasted_iota(jnp.int32, (N_GROUPS, t), 0)
    grp_on = jnp.zeros((N_GROUPS, t), F32)
    for _ in range(TOPK_GROUPS):
        _, hit = _first_max(cur, sub_grp, N_GROUPS)
        grp_on = jnp.where(hit, 1.0, grp_on)
        cur = jnp.where(hit, NEG_INF, cur)
    exp_on = jnp.concatenate([jnp.broadcast_to(grp_on[g:g + 1], (PER_GROUP, t)) for g in range(N_GROUPS)], axis=0)
    cur = jnp.where(exp_on > 0.0, sel, NEG_INF)
    hits, idx_rows = [], []
    for _ in range(TOP_K):
        _, hit = _first_max(cur, sub_e, N_EXPERTS)
        hits.append(hit)
        idx_rows.append(jnp.sum(jnp.where(hit, sub_e, 0), axis=0, keepdims=True))
        cur = jnp.where(hit, NEG_INF, cur)
    w_rows = [jnp.sum(jnp.where(hit, scores, 0.0), axis=0, keepdims=True) for hit in hits]
    total = functools.reduce(lambda a, b: a + b, w_rows)
    w_ref[...] = jnp.concatenate([w / total * ROUTED_SCALE for w in w_rows], axis=0)
    idx_ref[...] = jnp.concatenate(idx_rows, axis=0)
    chosen = functools.reduce(lambda a, b: a + b, [jnp.where(hit, 1.0, 0.0) for hit in hits])
    incl = jnp.dot(chosen.astype(BF16), tri_ref[...], preferred_element_type=F32)
    before = incl - chosen + carry_ref[:, :1]
    rank_ref[...] = jnp.concatenate(
        [jnp.sum(jnp.where(hit, before, 0.0), axis=0, keepdims=True) for hit in hits], axis=0).astype(jnp.int32)
    carry_ref[...] = carry_ref[...] + incl[:, t - 1:t]
    cnt_ref[...] = carry_ref[...]


def _route(logits, router_bias):
    n = logits.shape[0]
    t = ROUTE_T
    tri = (jnp.arange(t)[:, None] <= jnp.arange(t)[None, :]).astype(BF16)
    bias_b = jnp.broadcast_to(router_bias.astype(F32)[:, None], (N_EXPERTS, t))
    slot = pl.BlockSpec((TOP_K, t), lambda i: (0, i))
    idx, w, rank, cnt = pl.pallas_call(
        _route_kernel,
        grid=(n // t,),
        in_specs=[pl.BlockSpec((t, LANES), lambda i: (i, 0)),
                  pl.BlockSpec((N_EXPERTS, t), lambda i: (0, 0)),
                  pl.BlockSpec((t, t), lambda i: (0, 0))],
        out_specs=[slot, slot, slot, pl.BlockSpec((N_EXPERTS, LANES), lambda i: (0, 0))],
        out_shape=[jax.ShapeDtypeStruct((TOP_K, n), jnp.int32), jax.ShapeDtypeStruct((TOP_K, n), F32),
                   jax.ShapeDtypeStruct((TOP_K, n), jnp.int32), jax.ShapeDtypeStruct((N_EXPERTS, LANES), F32)],
        scratch_shapes=[pltpu.VMEM((N_EXPERTS, LANES), F32)],
        compiler_params=_cparams("arbitrary"),
    )(logits, bias_b, tri)
    return idx, w, rank, cnt[:, 0].astype(jnp.int32)


def _experts_kernel(be_ref, bv_ref, x_ref, w1_ref, w3_ref, w2_ref, o_ref, w1b, w3b, w2b):
    i = pl.program_id(0)
    changed = jnp.logical_or(i == 0, be_ref[i] != be_ref[jnp.maximum(i - 1, 0)])

    @pl.when(changed)
    def _():
        w1b[...] = w1_ref[0, 0].astype(BF16)
        w3b[...] = w3_ref[0, 0].astype(BF16)
        w2b[...] = w2_ref[0, 0].astype(BF16)

    @pl.when(bv_ref[i] > 0)
    def _():
        live = lax.broadcasted_iota(jnp.int32, (MOE_BLOCK, 1), 0) < bv_ref[i]
        xb = _unpack_bf16_pairs(jnp.where(live, x_ref[...], 0)).astype(BF16)
        hid = _silu(jnp.dot(xb, w1b[...], preferred_element_type=F32)) * \
            jnp.dot(xb, w3b[...], preferred_element_type=F32)
        o_ref[...] = _pack_bf16_pairs(jnp.dot(hid.astype(BF16), w2b[...], preferred_element_type=F32))

    @pl.when(bv_ref[i] == 0)
    def _():
        o_ref[...] = jnp.zeros_like(o_ref)


def _grouped_experts(xs, block_expert, block_valid, w1, w3, w2, layer):
    rows = xs.shape[0]
    n_blocks = rows // MOE_BLOCK
    grid_spec = pltpu.PrefetchScalarGridSpec(
        num_scalar_prefetch=2,
        grid=(n_blocks,),
        in_specs=[pl.BlockSpec((MOE_BLOCK, D_PACK), lambda i, be, bv: (i, 0)),
                  pl.BlockSpec((1, 1, D_MODEL, D_EXPERT), lambda i, be, bv: (layer, be[i], 0, 0)),
                  pl.BlockSpec((1, 1, D_MODEL, D_EXPERT), lambda i, be, bv: (layer, be[i], 0, 0)),
                  pl.BlockSpec((1, 1, D_EXPERT, D_MODEL), lambda i, be, bv: (layer, be[i], 0, 0))],
        out_specs=pl.BlockSpec((MOE_BLOCK, D_PACK), lambda i, be, bv: (i, 0)),
        scratch_shapes=[pltpu.VMEM((D_MODEL, D_EXPERT), BF16), pltpu.VMEM((D_MODEL, D_EXPERT), BF16),
                        pltpu.VMEM((D_EXPERT, D_MODEL), BF16)])
    return pl.pallas_call(
        _experts_kernel,
        grid_spec=grid_spec,
        out_shape=jax.ShapeDtypeStruct((rows, D_PACK), jnp.int32),
        compiler_params=_cparams("arbitrary"),
    )(block_expert, block_valid, xs, w1, w3, w2)


SC_CORES = 2
SC_SUBCORES = 16
SC_WORKERS = SC_CORES * SC_SUBCORES
SC_ROWS = 64


def _sc_mesh():
    return plsc.VectorSubcoreMesh(core_axis_name="c", subcore_axis_name="s",
                                  num_cores=SC_CORES, num_subcores=SC_SUBCORES)


def _sc_dispatch(table, dest, rows):
    n, d = table.shape
    kk = dest.shape[0]
    per_worker = n // SC_WORKERS
    n_chunks = per_worker // SC_ROWS
    assert per_worker * SC_WORKERS == n and n_chunks * SC_ROWS == per_worker
    idx = dest.reshape(kk, SC_WORKERS, n_chunks, SC_ROWS).transpose(1, 2, 0, 3).reshape(
        SC_WORKERS, n_chunks * kk, SC_ROWS)

    @functools.partial(
        pl.kernel, mesh=_sc_mesh(),
        out_type=jax.ShapeDtypeStruct((rows, d), table.dtype),
        scratch_types=[pltpu.VMEM((n_chunks * kk, SC_ROWS), jnp.int32),
                       pltpu.VMEM((SC_ROWS, d), table.dtype),
                       pltpu.SemaphoreType.DMA])
    def dispatch(table_hbm, idx_hbm, out_hbm, idx_v, rows_v, sem):
        wid = lax.axis_index("s") * SC_CORES + lax.axis_index("c")
        base = wid * per_worker
        pltpu.sync_copy(idx_hbm.at[wid], idx_v)

        @pl.loop(0, n_chunks)
        def _(j):
            pltpu.sync_copy(table_hbm.at[pl.ds(base + j * SC_ROWS, SC_ROWS)], rows_v)
            copies = [pltpu.async_copy(rows_v, out_hbm.at[idx_v.at[j * kk + k]], sem) for k in range(kk)]
            for cp in copies:
                cp.wait()

    return dispatch(table, idx)


def _sc_gather(table, idx):
    b = idx.shape[0]
    d = table.shape[1]
    per_worker = b // SC_WORKERS
    n_chunks = per_worker // SC_ROWS
    assert per_worker * SC_WORKERS == b and n_chunks * SC_ROWS == per_worker

    @functools.partial(
        pl.kernel, mesh=_sc_mesh(),
        out_type=jax.ShapeDtypeStruct((b, d), table.dtype),
        scratch_types=[pltpu.VMEM((n_chunks, SC_ROWS), jnp.int32),
                       pltpu.VMEM((SC_ROWS, d), table.dtype),
                       pltpu.SemaphoreType.DMA])
    def gather(table_hbm, idx_hbm, out_hbm, idx_v, rows_v, sem):
        wid = lax.axis_index("s") * SC_CORES + lax.axis_index("c")
        base = wid * per_worker
        pltpu.sync_copy(idx_hbm.at[wid], idx_v)

        @pl.loop(0, n_chunks)
        def _(j):
            pltpu.async_copy(table_hbm.at[idx_v.at[j]], rows_v, sem).wait()
            pltpu.sync_copy(rows_v, out_hbm.at[pl.ds(base + j * SC_ROWS, SC_ROWS)])

    return gather(table, idx.reshape(SC_WORKERS, n_chunks, SC_ROWS))


POST_TM = 512


def _ffn_post_kernel(x_ref, yg_ref, w_ref, s_ref, gate_ref, g_ref, b_ref, o_ref):
    w = w_ref[...]
    routed = _unpack_bf16_pairs(yg_ref[0]) * w[:, 0:1]
    for k in range(1, TOP_K):
        routed = routed + _unpack_bf16_pairs(yg_ref[k]) * w[:, k:k + 1]
    r = DEEPNORM_ALPHA * x_ref[...] + gate_ref[0] * (routed + s_ref[...])
    o_ref[...] = _layer_norm_rows(r, g_ref[...], b_ref[...])


def _ffn_post(x, yg, w_tok, shared, mod3, ln_g, ln_b):
    n = x.shape[0]
    tm = POST_TM
    row = lambda i: _mod_row_of_tile(i, tm)
    tile = pl.BlockSpec((tm, D_MODEL), lambda i: (i, 0))
    vec = pl.BlockSpec((1, D_MODEL), lambda i: (0, 0))
    return pl.pallas_call(
        _ffn_post_kernel,
        grid=(n // tm,),
        in_specs=[tile, pl.BlockSpec((TOP_K, tm, D_PACK), lambda i: (0, i, 0)),
                  pl.BlockSpec((tm, TOP_K), lambda i: (i, 0)), tile,
                  pl.BlockSpec((1, 1, D_MODEL), lambda i: (row(i), 0, 5)), vec, vec],
        out_specs=tile,
        out_shape=jax.ShapeDtypeStruct((n, D_MODEL), F32),
        compiler_params=_cparams("parallel"),
    )(x, yg, w_tok, shared, mod3, ln_g.reshape(1, D_MODEL), ln_b.reshape(1, D_MODEL))


def _moe_routed(h, idx, rank, counts, w1, w3, w2, layer):
    n = h.shape[0]
    padded = (counts + MOE_BLOCK - 1) // MOE_BLOCK * MOE_BLOCK
    pad_end = jnp.cumsum(padded)
    pad_start = pad_end - padded
    experts = jnp.arange(N_EXPERTS, dtype=jnp.int32)
    dest = jnp.sum(jnp.where(idx[:, :, None] == experts, pad_start, 0), axis=-1) + rank
    n_blocks = n * TOP_K // MOE_BLOCK + N_EXPERTS
    rows = n_blocks * MOE_BLOCK
    block_start = jnp.arange(n_blocks, dtype=jnp.int32) * MOE_BLOCK
    block_expert = jnp.minimum(jnp.sum(pad_end[None, :] <= block_start[:, None], axis=1), N_EXPERTS - 1).astype(jnp.int32)
    of_block = block_expert[:, None] == experts
    used = jnp.sum(jnp.where(of_block, counts, 0), axis=1) - (block_start - jnp.sum(jnp.where(of_block, pad_start, 0), axis=1))
    block_valid = jnp.clip(used, 0, MOE_BLOCK).astype(jnp.int32)
    xs = _sc_dispatch(h, dest, rows)
    ys = _grouped_experts(xs, block_expert, block_valid, w1, w3, w2, layer)
    return _sc_gather(ys, dest.reshape(-1)).reshape(TOP_K, n, D_PACK)


def _even_layer(x, mod3, j, w_in, w_out, q_norm, k_norm, log_decay, gn_g,
                cache_k, cache_v, state_ret):
    (p,) = _modulated_proj(x, mod3, 0, w_in.astype(BF16), [EVEN_IN], [F32])
    q, k = _attn_prep(p, q_norm, k_norm)
    v = p[:, A_Q + A_KV:A_Q + 2 * A_KV]
    k_p = k[:N_P].reshape(BATCH, SEQ, A_KV)
    v_p = v[:N_P].reshape(BATCH, SEQ, A_KV)
    o_p = _attention(q, k_p, v_p, 0, BATCH, SEQ)
    k_all = jnp.concatenate([k[N_P:].reshape(DEC_BATCH, DEC_SEQ, A_KV),
                             cache_k.reshape(DEC_BATCH, PAST_LEN, A_KV)], axis=1)
    v_all = jnp.concatenate([v[N_P:].reshape(DEC_BATCH, DEC_SEQ, A_KV),
                             cache_v.reshape(DEC_BATCH, PAST_LEN, A_KV)], axis=1)
    o_s = _attention(q, k_all, v_all, N_P // DEC_SEQ, DEC_BATCH, DEC_SEQ)
    o_attn = jnp.concatenate([o_p, o_s], axis=0)
    lg_rows = jnp.broadcast_to(log_decay.reshape(2 * H_B, 1), (2 * H_B, LANES))
    zeros_s = jnp.zeros((BATCH, 2, H_B, DK_B, DV_B), F32)
    r_p, s_p = _retention(p, 0, BATCH, SEQ, lg_rows, zeros_s, gn_g)
    r_s, _ = _retention(p, N_P // DEC_SEQ, DEC_BATCH, DEC_SEQ, lg_rows, state_ret, gn_g)
    o_ret = jnp.concatenate([r_p, r_s], axis=0)
    return ([o_attn, o_ret], w_out.astype(BF16), k_p.reshape(BATCH, SEQ, N_KV_A, HD_A),
            v_p.reshape(BATCH, SEQ, N_KV_A, HD_A), s_p)


def _chunk_cumsum(g, batch, t):
    gc = g.reshape(batch, t // GC, GC, 2, H_C)
    f = jnp.cumsum(gc[:, :, :, 0], axis=2)
    b = jnp.cumsum(gc[:, :, ::-1, 1], axis=2)[:, :, ::-1]
    return jnp.stack([f, b], axis=3)


def _odd_layer(x, mod3, j, w_in, conv_w, a_log, dt_bias, norm_g, w_out, state_gdn):
    w_main = w_in[:, :2 * C_QK + 2 * C_V].astype(BF16)
    w_ab = jnp.pad(w_in[:, 2 * C_QK + 2 * C_V:], ((0, 0), (0, LANES - 4 * H_C))).astype(BF16)
    w_cat = jnp.concatenate([w_main, w_ab], axis=1)
    pm, pab = _modulated_proj(x, mod3, 0, w_cat, [2 * C_QK + 2 * C_V, LANES], [F32, F32])
    qkv_p = _gdn_prep(pm, conv_w, 0, BATCH, SEQ)
    qkv_s = _gdn_prep(pm, conv_w, N_P // DEC_SEQ, DEC_BATCH, DEC_SEQ)
    ab = pab[:, :4 * H_C].reshape(N_TOK, 2, 2, H_C)
    beta = jax.nn.sigmoid(ab[:, 0])
    g = -jnp.exp(a_log.astype(F32)) * jax.nn.softplus(ab[:, 1] + dt_bias.astype(F32))

    gcs_p = _chunk_cumsum(g[:N_P], BATCH, SEQ)
    gcs_s = _chunk_cumsum(g[N_P:], DEC_BATCH, DEC_SEQ)
    gcol = jnp.concatenate([gcs_p.reshape(N_P, 2 * H_C), gcs_s.reshape(N_S, 2 * H_C)], axis=0)
    gbeta = jnp.concatenate([beta.reshape(N_TOK, 2 * H_C), gcol], axis=1)
    zeros_s = jnp.zeros((BATCH, 2, H_C, DK_C, DV_C), F32)
    o_p, s_p = _gdn(qkv_p, pm, gbeta, gcs_p.transpose(0, 3, 4, 1, 2), 0, BATCH, SEQ, zeros_s, norm_g)
    o_s, _ = _gdn(qkv_s, pm, gbeta, gcs_s.transpose(0, 3, 4, 1, 2), N_P // DEC_SEQ, DEC_BATCH, DEC_SEQ,
                  state_gdn, norm_g)
    return [jnp.concatenate([o_p, o_s], axis=0)], w_out.astype(BF16), s_p


def kernel(x_prompt, x_sample, cache_attn_k, cache_attn_v, state_ret, state_gdn, c, c_ctx, mod_w, mod_b, ln_g, ln_b, even_w_in, even_w_out, attn_q_norm, attn_k_norm, ret_log_decay, ret_norm_g, odd_w_in, gdn_conv_w, gdn_a_log, gdn_dt_bias, gdn_norm_g, odd_w_out, router_w, router_bias, expert_w1, expert_w3, expert_w2, shared_w1, shared_w3, shared_w2):
    x = jnp.concatenate([x_prompt.reshape(N_P, D_MODEL), x_sample.reshape(N_S, D_MODEL)], axis=0)
    cvec = jnp.concatenate([c_ctx[None, :], c, jnp.zeros((MOD_ROWS - N_MOD, D_MODEL), F32)], axis=0)
    mod_all = _mod_vectors(cvec, mod_w, mod_b)
    new_k, new_v, new_ret, new_gdn = [], [], [], []
    for l in range(DEPTH):
        j = l // 2
        mod3 = mod_all[l].reshape(MOD_ROWS, 1, 6 * D_MODEL)
        if l % 2 == 0:
            a_list, w_o, k_p, v_p, s_p = _even_layer(
                x, mod3, j, even_w_in[j], even_w_out[j], attn_q_norm[j], attn_k_norm[j],
                ret_log_decay[j], ret_norm_g[j], cache_attn_k[:, j], cache_attn_v[:, j], state_ret[:, j])
            new_k.append(k_p)
            new_v.append(v_p)
            new_ret.append(s_p)
        else:
            a_list, w_o, s_p = _odd_layer(
                x, mod3, j, odd_w_in[j], gdn_conv_w[j], gdn_a_log[j], gdn_dt_bias[j], gdn_norm_g[j],
                odd_w_out[j], state_gdn[:, j])
            new_gdn.append(s_p)
        x = _outproj_ln(a_list, w_o, x, mod3, 2, ln_g[l, 0], ln_b[l, 0])
        rw = jnp.pad(router_w[l], ((0, 0), (0, LANES - N_EXPERTS)))
        ws13 = jnp.concatenate([shared_w1[l], shared_w3[l]], axis=1).astype(BF16)
        h, logits = _ffn_pre(x, mod3, rw)
        idx, w, rank, counts = _route(logits, router_bias[l])
        yg = _moe_routed(h, idx, rank, counts, expert_w1, expert_w3, expert_w2, l)
        shared = _shared_expert(h, ws13, shared_w2[l].astype(BF16))
        x = _ffn_post(x, yg, w.T, shared, mod3, ln_g[l, 1], ln_b[l, 1])
    return (x[:N_P].reshape(BATCH, SEQ, D_MODEL), x[N_P:].reshape(DEC_BATCH, DEC_SEQ, D_MODEL),
            jnp.stack(new_k, axis=1), jnp.stack(new_v, axis=1),
            jnp.stack(new_ret, axis=1), jnp.stack(new_gdn, axis=1))
```
